```python
import math
import jax, jax.numpy as jnp
from jax import lax
import numpy as np

D_MODEL = 4096
BATCH = 1
SEQ = 16384
DEPTH = 4

D_MIX = D_MODEL
HY_WIDTH = D_MIX // 2
SSM_WIDTH = D_MIX - HY_WIDTH
SSM_HEAD_DIM = 64
SSM_HEADS = SSM_WIDTH // SSM_HEAD_DIM
SSM_GROUPS = 8
SSM_STATE = 128
SSM_CHUNK = 256
SSM_XBC = SSM_WIDTH + 2 * SSM_GROUPS * SSM_STATE
SHORT_CONV = 3
HY_ORDER = 2
HY_EMB = 33
HY_FILTER_HIDDEN = 64
HY_N_SIN = 3
HY_FAST_DECAY = 0.3
HY_SLOW_DECAY = 1.5
HY_DECAY_TARGET = 1e-2
D_FF = 4 * D_MODEL
NORM_EPS = 1e-5
N_IN = 3 * HY_WIDTH + SSM_WIDTH + SSM_XBC + SSM_HEADS

kernel_name = 'hyena_mamba2_parallel_hybrid_encoder'


def _rms(x):
    xf = x.astype(jnp.float32)
    return xf * lax.rsqrt(jnp.mean(jnp.square(xf), axis=-1, keepdims=True) + NORM_EPS)


def rmsnorm(x, g):
    return (_rms(x) * g.astype(jnp.float32)).astype(x.dtype)


def dwconv_centred(u, w, b):
    K = w.shape[0]
    p = K // 2
    L = u.shape[1]
    up = jnp.pad(u, ((0, 0), (p, p), (0, 0)))
    y = w[0] * up[:, 0:L]
    for k in range(1, K):
        y = y + w[k] * up[:, k:k + L]
    return y + b


def hyena_filters(L, w1, b1, w2, b2, w3, b3, sin_freq, wout):
    f32 = jnp.float32
    t = jnp.linspace(0.0, 1.0, L, dtype=f32)[:, None]
    bands = (HY_EMB - 1) // 2
    w = 2.0 * math.pi * jnp.arange(L, dtype=f32)[:, None] / L
    f = jnp.linspace(1e-4, bands - 1, bands, dtype=f32)[None, :]
    z = jnp.concatenate([t, jnp.cos(f * w), -jnp.sin(f * w)], axis=-1)
    fr = sin_freq.astype(f32)
    h = jnp.sin(fr[0] * (z @ w1.astype(f32) + b1.astype(f32)))
    h = jnp.sin(fr[1] * (h @ w2.astype(f32) + b2.astype(f32)))
    h = jnp.sin(fr[2] * (h @ w3.astype(f32) + b3.astype(f32)))
    h = (h @ wout.astype(f32)).reshape(L, HY_ORDER, 2, HY_WIDTH)
    deltas = jnp.linspace(math.log(HY_FAST_DECAY) / HY_DECAY_TARGET,
                          math.log(HY_SLOW_DECAY) / HY_DECAY_TARGET, HY_WIDTH, dtype=f32)
    decay = jnp.exp(-t * jnp.abs(deltas)[None, :])
    h = h * decay[:, None, None, :]
    g = jnp.concatenate([h[:, :, 0], jnp.zeros((1, HY_ORDER, HY_WIDTH), f32), h[:0:-1, :, 1]], axis=0)
    g = g / jnp.sum(jnp.abs(g), axis=0, keepdims=True)
    return jnp.fft.rfft(g, n=2 * L, axis=0)


def fft_long_conv(u, G):
    L = u.shape[1]
    U = jnp.fft.rfft(u, n=2 * L, axis=1)
    return jnp.fft.irfft(U * G[None], n=2 * L, axis=1)[:, :L]


def hyena_mixer(u_proj, conv_w, conv_b, w1, b1, w2, b2, w3, b3, sin_freq, wout, skip):
    L = u_proj.shape[1]
    u = dwconv_centred(u_proj, conv_w, conv_b).astype(jnp.float32)
    gate1, gate2, v = jnp.split(u, 3, axis=-1)
    G = hyena_filters(L, w1, b1, w2, b2, w3, b3, sin_freq, wout)
    sk = skip.astype(jnp.float32)
    z = v
    for o, gate in enumerate((gate1, gate2)):
        z = gate * (fft_long_conv(z, G[:, o]) + sk[o] * z)
    return z


def segsum(a):
    T = a.shape[-1]
    a_rep = jnp.broadcast_to(a[..., :, None], a.shape + (T,))
    strict = jnp.tril(jnp.ones((T, T), dtype=bool), -1)
    cs = jnp.cumsum(jnp.where(strict, a_rep, 0.0), axis=-2)
    return jnp.where(jnp.tril(jnp.ones((T, T), dtype=bool), 0), cs, -jnp.inf)


def ssd_chunked(xh, dt, A, Bg, Cg):
    b, L, H, P = xh.shape
    G, N = Bg.shape[-2:]
    R = H // G
    Q = SSM_CHUNK
    pad = (-L) % Q
    if pad:
        padfn = lambda a: jnp.pad(a, [(0, 0), (0, pad)] + [(0, 0)] * (a.ndim - 2))
        xh, dt, Bg, Cg = padfn(xh), padfn(dt), padfn(Bg), padfn(Cg)
    C = (L + pad) // Q
    X = (xh * dt[..., None]).reshape(b, C, Q, G, R, P)
    a = (dt * A).reshape(b, C, Q, G, R).transpose(0, 1, 3, 4, 2)
    Bc = Bg.reshape(b, C, Q, G, N)
    Cc = Cg.reshape(b, C, Q, G, N)
    a_cs = jnp.cumsum(a, axis=-1)
    Lmat = jnp.exp(segsum(a))
    CB = jnp.einsum('bclgn,bcsgn->bcgls', Cc, Bc)
    y_diag = jnp.einsum('bcgrls,bcsgrp->bclgrp', CB[:, :, :, None] * Lmat, X)
    decay_states = jnp.exp(a_cs[..., -1:] - a_cs)
    states = jnp.einsum('bcsgn,bcgrs,bcsgrp->bcgrpn', Bc, decay_states, X)
    chunk_tot = jnp.pad(a_cs[..., -1].transpose(0, 2, 3, 1), ((0, 0), (0, 0), (0, 0), (1, 0)))
    decay_chunk = jnp.exp(segsum(chunk_tot))
    states = jnp.pad(states, ((0, 0), (1, 0), (0, 0), (0, 0), (0, 0), (0, 0)))
    states_in = jnp.einsum('bgrzc,bcgrpn->bzgrpn', decay_chunk, states)[:, :-1]
    y_off = jnp.einsum('bclgn,bcgrpn,bcgrl->bclgrp', Cc, states_in, jnp.exp(a_cs))
    y = (y_diag + y_off).reshape(b, C * Q, H, P)
    return y[:, :L]


def mamba2_mixer(z, xbc, dt_raw, conv_w, conv_b, A_log, dt_bias, D, norm_g):
    f32 = jnp.float32
    xbc = jax.nn.silu(dwconv_centred(xbc, conv_w, conv_b).astype(f32))
    b, L, _ = xbc.shape
    gn = SSM_GROUPS * SSM_STATE
    xh = xbc[..., :SSM_WIDTH].reshape(b, L, SSM_HEADS, SSM_HEAD_DIM)
    Bg = xbc[..., SSM_WIDTH:SSM_WIDTH + gn].reshape(b, L, SSM_GROUPS, SSM_STATE)
    Cg = xbc[..., SSM_WIDTH + gn:].reshape(b, L, SSM_GROUPS, SSM_STATE)
    dt_raw = dt_raw.astype(f32)
    A = -jnp.exp(A_log.astype(f32))
    dtb = dt_bias.astype(f32)
    dt_f = jax.nn.softplus(dt_raw + dtb[0])
    dt_b = jax.nn.softplus(dt_raw + dtb[1])
    flip = lambda a: jnp.flip(a, axis=1)
    y_f = ssd_chunked(xh, dt_f, A[0], Bg, Cg)
    y_b = flip(ssd_chunked(flip(xh), flip(dt_b), A[1], flip(Bg), flip(Cg)))
    y = y_f + y_b + D.astype(f32)[:, None] * xh
    y = y.reshape(b, L, SSM_WIDTH) * jax.nn.silu(z.astype(f32))
    y = _rms(y.reshape(b, L, SSM_GROUPS, SSM_WIDTH // SSM_GROUPS)).reshape(b, L, SSM_WIDTH)
    return y * norm_g.astype(f32)


def setup_inputs(seed: int = 0) -> dict:
    key = jax.random.key(seed)
    ks = jax.random.split(key, 32)
    f32 = jnp.float32

    def normal(k, shape, scale):
        return jax.random.normal(k, shape, f32) * scale

    Hd = HY_FILTER_HIDDEN
    dt_init = jnp.exp(jax.random.uniform(ks[17], (DEPTH, 2, SSM_HEADS), f32,
                                         minval=math.log(1e-3), maxval=math.log(1e-1)))
    return {
        'x': normal(ks[0], (BATCH, SEQ, D_MODEL), 1.0),
        'norm_mix_g': 1.0 + normal(ks[1], (DEPTH, D_MODEL), 0.02),
        'w_in': normal(ks[2], (DEPTH, D_MODEL, N_IN), D_MODEL ** -0.5),
        'hy_conv_w': normal(ks[3], (DEPTH, SHORT_CONV, 3 * HY_WIDTH), SHORT_CONV ** -0.5),
        'hy_conv_b': normal(ks[4], (DEPTH, 3 * HY_WIDTH), 0.02),
        'hy_pos_w1': normal(ks[5], (DEPTH, HY_EMB, Hd), HY_EMB ** -0.5),
        'hy_pos_b1': normal(ks[6], (DEPTH, Hd), 0.1),
        'hy_pos_w2': normal(ks[7], (DEPTH, Hd, Hd), Hd ** -0.5),
        'hy_pos_b2': normal(ks[8], (DEPTH, Hd), 0.1),
        'hy_pos_w3': normal(ks[9], (DEPTH, Hd, Hd), Hd ** -0.5),
        'hy_pos_b3': normal(ks[10], (DEPTH, Hd), 0.1),
        'hy_sin_freq': 1.0 + normal(ks[11], (DEPTH, HY_N_SIN, Hd), 0.1),
        'hy_pos_wout': normal(ks[12], (DEPTH, Hd, HY_ORDER * 2 * HY_WIDTH), Hd ** -0.5),
        'hy_skip': normal(ks[13], (DEPTH, HY_ORDER, HY_WIDTH), 0.1),
        'hy_out_g': 1.0 + normal(ks[14], (DEPTH, HY_WIDTH), 0.02),
        'ssm_conv_w': normal(ks[15], (DEPTH, SHORT_CONV, SSM_XBC), SHORT_CONV ** -0.5),
        'ssm_conv_b': normal(ks[16], (DEPTH, SSM_XBC), 0.02),
        'ssm_A_log': jnp.log(jax.random.uniform(ks[18], (DEPTH, 2, SSM_HEADS), f32, minval=1.0, maxval=16.0)),
        'ssm_dt_bias': dt_init + jnp.log(-jnp.expm1(-dt_init)),
        'ssm_D': 1.0 + normal(ks[19], (DEPTH, SSM_HEADS), 0.1),
        'ssm_out_g': 1.0 + normal(ks[20], (DEPTH, SSM_WIDTH), 0.02),
        'w_out': normal(ks[21], (DEPTH, D_MIX, D_MODEL), D_MIX ** -0.5),
        'norm_mlp_g': 1.0 + normal(ks[22], (DEPTH, D_MODEL), 0.02),
        'w_up': normal(ks[23], (DEPTH, D_MODEL, D_FF), D_MODEL ** -0.5),
        'w_down': normal(ks[24], (DEPTH, D_FF, D_MODEL), D_FF ** -0.5),
        'final_norm_g': 1.0 + normal(ks[25], (D_MODEL,), 0.02),
    }


def reference(x, norm_mix_g, w_in, hy_conv_w, hy_conv_b, hy_pos_w1, hy_pos_b1, hy_pos_w2, hy_pos_b2,
              hy_pos_w3, hy_pos_b3, hy_sin_freq, hy_pos_wout, hy_skip, hy_out_g, ssm_conv_w, ssm_conv_b,
              ssm_A_log, ssm_dt_bias, ssm_D, ssm_out_g, w_out, norm_mlp_g, w_up, w_down, final_norm_g):
    dtype = x.dtype
    o1 = 3 * HY_WIDTH
    o2 = o1 + SSM_WIDTH
    o3 = o2 + SSM_XBC
    for l in range(DEPTH):
        h = rmsnorm(x, norm_mix_g[l])
        proj = h @ w_in[l]
        y_hy = hyena_mixer(proj[..., :o1], hy_conv_w[l], hy_conv_b[l], hy_pos_w1[l], hy_pos_b1[l],
                           hy_pos_w2[l], hy_pos_b2[l], hy_pos_w3[l], hy_pos_b3[l], hy_sin_freq[l],
                           hy_pos_wout[l], hy_skip[l])
        y_hy = _rms(y_hy) * hy_out_g[l].astype(jnp.float32)
        y_ssm = mamba2_mixer(proj[..., o1:o2], proj[..., o2:o3], proj[..., o3:], ssm_conv_w[l],
                             ssm_conv_b[l], ssm_A_log[l], ssm_dt_bias[l], ssm_D[l], ssm_out_g[l])
        mixed = jnp.concatenate([y_hy.astype(dtype), y_ssm.astype(dtype)], axis=-1) @ w_out[l]
        x = x + mixed
        h = rmsnorm(x, norm_mlp_g[l])
        x = x + jnp.square(jax.nn.relu(h @ w_up[l])) @ w_down[l]
    return rmsnorm(x, final_norm_g)
```

```python
import functools
import math

import numpy as np
import jax
import jax.numpy as jnp
from jax import lax
from jax.experimental import pallas as pl
from jax.experimental.pallas import tpu as pltpu

F32 = jnp.float32
BF16 = jnp.bfloat16

NORM_EPS = 1e-5
SSM_HEAD_DIM = 64
SSM_STATE = 128
SSM_CHUNK = 256
HY_FAST_DECAY = 0.3
HY_SLOW_DECAY = 1.5
HY_DECAY_TARGET = 1e-2

LANES = 128
SUBLANES = 8
DFT_N2 = 128
HY_CB = 128
VMEM_LIMIT_MB = 56


def _cparams(sem, vmem_mb=VMEM_LIMIT_MB):
    return pltpu.CompilerParams(dimension_semantics=sem, vmem_limit_bytes=vmem_mb * 1024 * 1024)


def _pick(n, target, mult):
    best = None
    for d in range(mult, min(n, target) + 1, mult):
        if n % d == 0:
            best = d
    assert best is not None, (n, target, mult)
    return best


def _dot(a, b):
    return jnp.dot(a, b, preferred_element_type=F32)


def _split3(a):
    hi = a.astype(BF16)
    r1 = a - hi.astype(F32)
    mid = r1.astype(BF16)
    lo = (r1 - mid.astype(F32)).astype(BF16)
    return hi, mid, lo


def _dot_f32(a, b):
    ah = a.astype(BF16)
    al = (a - ah.astype(F32)).astype(BF16)
    bh = b.astype(BF16)
    bl = (b - bh.astype(F32)).astype(BF16)
    return _dot(ah, bh) + (_dot(ah, bl) + _dot(al, bh))


def _silu(x):
    return x / (1.0 + jnp.exp(-x))


def _rmsnorm_body(x_ref, g_ref, h_ref):
    x = x_ref[...]
    ms = jnp.mean(x * x, axis=-1, keepdims=True)
    h_ref[...] = (x * lax.rsqrt(ms + NORM_EPS) * g_ref[...]).astype(h_ref.dtype)


def _rmsnorm(x, g, out_dtype):
    M, D = x.shape
    bm = _pick(M, 256, SUBLANES)
    return pl.pallas_call(
        _rmsnorm_body, grid=(M // bm,),
        in_specs=[pl.BlockSpec((bm, D), lambda i: (i, 0)), pl.BlockSpec((1, D), lambda i: (0, 0))],
        out_specs=pl.BlockSpec((bm, D), lambda i: (i, 0)),
        out_shape=jax.ShapeDtypeStruct((M, D), out_dtype),
        compiler_params=_cparams(("parallel",)), name="rmsnorm",
    )(x, g.reshape(1, D))


def _add_rmsnorm_body(x_ref, d_ref, g_ref, xo_ref, h_ref):
    x = x_ref[...] + d_ref[...]
    xo_ref[...] = x
    ms = jnp.mean(x * x, axis=-1, keepdims=True)
    h_ref[...] = (x * lax.rsqrt(ms + NORM_EPS) * g_ref[...]).astype(h_ref.dtype)


def _add_rmsnorm(x, delta, g, out_dtype):
    M, D = x.shape
    bm = _pick(M, 256, SUBLANES)
    row = pl.BlockSpec((bm, D), lambda i: (i, 0))
    return pl.pallas_call(
        _add_rmsnorm_body, grid=(M // bm,),
        in_specs=[row, row, pl.BlockSpec((1, D), lambda i: (0, 0))],
        out_specs=[row, row],
        out_shape=[jax.ShapeDtypeStruct((M, D), F32), jax.ShapeDtypeStruct((M, D), out_dtype)],
        compiler_params=_cparams(("parallel",)), name="add_rmsnorm",
    )(x, delta, g.reshape(1, D))


def _mm_body(a_ref, b_ref, o_ref):
    o_ref[...] = _dot(a_ref[...], b_ref[...]).astype(o_ref.dtype)


def _matmul(a, b, out_dtype=F32, bm_t=1024, bn_t=1024):
    M, K = a.shape
    N = b.shape[1]
    bm = _pick(M, bm_t, SUBLANES)
    bn = _pick(N, bn_t, LANES)
    return pl.pallas_call(
        _mm_body, grid=(M // bm, N // bn),
        in_specs=[pl.BlockSpec((bm, K), lambda i, j: (i, 0)), pl.BlockSpec((K, bn), lambda i, j: (0, j))],
        out_specs=pl.BlockSpec((bm, bn), lambda i, j: (i, j)),
        out_shape=jax.ShapeDtypeStruct((M, N), out_dtype),
        compiler_params=_cparams(("parallel", "parallel")), name="proj_matmul",
    )(a, b)


def _out_mm_body(a1_ref, a2_ref, b1_ref, b2_ref, r_ref, o_ref):
    acc = _dot(a1_ref[...], b1_ref[...]) + _dot(a2_ref[...], b2_ref[...])
    o_ref[...] = r_ref[...] + acc


def _out_matmul(a1, a2, b1, b2, res):
    M, K1 = a1.shape
    K2 = a2.shape[1]
    N = b1.shape[1]
    bm = _pick(M, 1024, SUBLANES)
    bn = _pick(N, 1024, LANES)
    return pl.pallas_call(
        _out_mm_body, grid=(M // bm, N // bn),
        in_specs=[pl.BlockSpec((bm, K1), lambda i, j: (i, 0)), pl.BlockSpec((bm, K2), lambda i, j: (i, 0)),
                  pl.BlockSpec((K1, bn), lambda i, j: (0, j)), pl.BlockSpec((K2, bn), lambda i, j: (0, j)),
                  pl.BlockSpec((bm, bn), lambda i, j: (i, j))],
        out_specs=pl.BlockSpec((bm, bn), lambda i, j: (i, j)),
        out_shape=jax.ShapeDtypeStruct((M, N), F32),
        compiler_params=_cparams(("parallel", "parallel")), name="out_matmul",
    )(a1, a2, b1, b2, res)


def _mlp_body(h_ref, wu_ref, wd_ref, o_ref):
    j = pl.program_id(1)
    t = _dot(h_ref[...], wu_ref[...])
    t = jnp.square(jnp.maximum(t, 0.0)).astype(BF16)
    d = _dot(t, wd_ref[...])

    @pl.when(j == 0)
    def _():
        o_ref[...] = d

    @pl.when(j > 0)
    def _():
        o_ref[...] += d


def _mlp(h, w_up, w_down):
    M, D = h.shape
    F = w_up.shape[1]
    bm = _pick(M, 512, SUBLANES)
    bf = _pick(F, 512, LANES)
    return pl.pallas_call(
        _mlp_body, grid=(M // bm, F // bf),
        in_specs=[pl.BlockSpec((bm, D), lambda i, j: (i, 0)), pl.BlockSpec((D, bf), lambda i, j: (0, j)),
                  pl.BlockSpec((bf, D), lambda i, j: (j, 0))],
        out_specs=pl.BlockSpec((bm, D), lambda i, j: (i, 0)),
        out_shape=jax.ShapeDtypeStruct((M, D), F32),
        compiler_params=_cparams(("parallel", "arbitrary")), name="mlp",
    )(h, w_up, w_down)


def _dwconv_body(p_ref, pv_ref, nx_ref, w_ref, b_ref, o_ref, *, act):
    i = pl.program_id(0)
    last = pl.num_programs(0) - 1
    u = p_ref[...]
    bm = u.shape[0]
    prev_row = jnp.where(i > 0, pv_ref[SUBLANES - 1:SUBLANES, :], 0.0)
    next_row = jnp.where(i < last, nx_ref[0:1, :], 0.0)
    rows = lax.broadcasted_iota(jnp.int32, u.shape, 0)
    um = jnp.where(rows == 0, prev_row, pltpu.roll(u, 1, 0))
    up = jnp.where(rows == bm - 1, next_row, pltpu.roll(u, bm - 1, 0))
    y = w_ref[0:1, :] * um
    y = y + w_ref[1:2, :] * u
    y = y + w_ref[2:3, :] * up
    y = y + b_ref[...]
    if act == "silu":
        y = _silu(y)
    o_ref[...] = y


def _dwconv(p, col_off, C, w, b, act):
    L = p.shape[0]
    bm = _pick(L, 512, SUBLANES)
    bc = _pick(math.gcd(C, col_off) if col_off else C, 512, LANES)
    co = col_off // bc
    hb = bm // SUBLANES
    nrb = L // SUBLANES
    return pl.pallas_call(
        functools.partial(_dwconv_body, act=act), grid=(L // bm, C // bc),
        in_specs=[pl.BlockSpec((bm, bc), lambda i, j: (i, co + j)),
                  pl.BlockSpec((SUBLANES, bc), lambda i, j: (jnp.maximum(i * hb - 1, 0), co + j)),
                  pl.BlockSpec((SUBLANES, bc), lambda i, j: (jnp.minimum((i + 1) * hb, nrb - 1), co + j)),
                  pl.BlockSpec((3, bc), lambda i, j: (0, j)),
                  pl.BlockSpec((1, bc), lambda i, j: (0, j))],
        out_specs=pl.BlockSpec((bm, bc), lambda i, j: (i, j)),
        out_shape=jax.ShapeDtypeStruct((L, C), F32),
        compiler_params=_cparams(("parallel", "parallel")), name="dwconv_" + act,
    )(p, p, p, w, b.reshape(1, C))


def _lane_groups(cols, width):
    q = cols[0].shape[0]
    n = len(cols)
    lane = lax.broadcasted_iota(jnp.int32, (q, n * width), 1)
    out = jnp.broadcast_to(cols[n - 1], (q, n * width))
    for r in range(n - 2, -1, -1):
        out = jnp.where(lane < (r + 1) * width, jnp.broadcast_to(cols[r], (q, n * width)), out)
    return out


def _ssd_body(xbc_ref, dt_ref, bias_ref, alog_ref, y_ref, s_ref, *, reverse, heads, groups):
    Q = SSM_CHUNK
    P = SSM_HEAD_DIM
    N = SSM_STATE
    R = heads // groups
    W = heads * P
    step = pl.program_id(0)

    @pl.when(step == 0)
    def _():
        s_ref[...] = jnp.zeros_like(s_ref)

    x = dt_ref[...] + bias_ref[...]
    dt = jnp.maximum(x, 0.0) + jnp.log1p(jnp.exp(-jnp.abs(x)))
    a = dt * (-jnp.exp(alog_ref[...]))
    ri = lax.broadcasted_iota(jnp.int32, (Q, Q), 0)
    ci = lax.broadcasted_iota(jnp.int32, (Q, Q), 1)
    mask = (ci >= ri) if reverse else (ri >= ci)
    tri = jnp.where(mask, 1.0, 0.0).astype(BF16)
    ah, am, al = _split3(a)
    acs = _dot(tri, ah) + (_dot(tri, am) + _dot(tri, al))
    tot = acs[0:1, :] if reverse else acs[Q - 1:Q, :]
    e_col = jnp.exp(acs)
    w_col = jnp.exp(tot - acs) * dt
    acs_row = acs.T
    dt_row = dt.T
    tcol = 0 if reverse else Q - 1

    for g in range(groups):
        bg = xbc_ref[:, W + g * N:W + (g + 1) * N].astype(BF16)
        cg = xbc_ref[:, W + groups * N + g * N:W + groups * N + (g + 1) * N].astype(BF16)
        xg = xbc_ref[:, g * R * P:(g + 1) * R * P]
        cb = lax.dot_general(cg, bg, (((1,), (1,)), ((), ())), preferred_element_type=F32)
        lane = lax.broadcasted_iota(jnp.int32, (Q, R * P), 1)
        ws, xs = [], []
        for r in range(R):
            h = g * R + r
            diff = acs[:, h:h + 1] - acs_row[h:h + 1, :]
            lm = jnp.exp(jnp.where(mask, diff, -jnp.inf))
            ws.append((cb * lm * dt_row[h:h + 1, :]).astype(BF16))
            xs.append(jnp.where((lane >= r * P) & (lane < (r + 1) * P), xg, 0.0).astype(BF16))
        y_diag = _dot(jnp.concatenate(ws, axis=1), jnp.concatenate(xs, axis=0))
        heads_g = range(g * R, (g + 1) * R)
        e_g = _lane_groups([e_col[:, h:h + 1] for h in heads_g], P)
        w_g = _lane_groups([w_col[:, h:h + 1] for h in heads_g], P)
        t_g = _lane_groups([jnp.exp(acs_row[h:h + 1, tcol:tcol + 1]) for h in heads_g], P)
        s_old = s_ref[g]
        y_off = _dot(cg, s_old.astype(BF16)) * e_g
        y_ref[:, g * R * P:(g + 1) * R * P] = y_diag + y_off
        s_new = lax.dot_general(bg, (xg * w_g).astype(BF16), (((0,), (0,)), ((), ())),
                                preferred_element_type=F32)
        s_ref[g] = s_old * t_g + s_new


def _ssd(xbc, dt_raw, dt_bias, a_log, *, reverse, heads, groups):
    L = xbc.shape[0]
    Q = SSM_CHUNK
    assert L % Q == 0
    nc = L // Q
    W = heads * SSM_HEAD_DIM
    R = heads // groups
    cmap = (lambda i: (nc - 1 - i, 0)) if reverse else (lambda i: (i, 0))
    return pl.pallas_call(
        functools.partial(_ssd_body, reverse=reverse, heads=heads, groups=groups), grid=(nc,),
        in_specs=[pl.BlockSpec((Q, xbc.shape[1]), cmap), pl.BlockSpec((Q, LANES), cmap),
                  pl.BlockSpec((1, LANES), lambda i: (0, 0)), pl.BlockSpec((1, LANES), lambda i: (0, 0))],
        out_specs=pl.BlockSpec((Q, W), cmap),
        out_shape=jax.ShapeDtypeStruct((L, W), F32),
        scratch_shapes=[pltpu.VMEM((groups, SSM_STATE, R * SSM_HEAD_DIM), F32)],
        compiler_params=_cparams(("arbitrary",)), name="ssd_bwd" if reverse else "ssd_fwd",
    )(xbc, dt_raw, dt_bias, a_log)


def _ssm_gate_body(yf_ref, yb_ref, xh_ref, z_ref, d_ref, g_ref, o_ref, *, groups):
    y = (yf_ref[...] + yb_ref[...] + d_ref[...] * xh_ref[...]) * _silu(z_ref[...])
    W = y.shape[1]
    gw = W // groups
    for g in range(groups):
        yg = y[:, g * gw:(g + 1) * gw]
        ms = jnp.mean(yg * yg, axis=-1, keepdims=True)
        o_ref[:, g * gw:(g + 1) * gw] = (yg * lax.rsqrt(ms + NORM_EPS) * g_ref[:, g * gw:(g + 1) * gw]).astype(o_ref.dtype)


def _ssm_gate(yf, yb, xbc, proj, z_off, d_rep, norm_g, groups):
    L, W = yf.shape
    bm = _pick(L, 256, SUBLANES)
    assert z_off % W == 0
    zo = z_off // W
    row = pl.BlockSpec((bm, W), lambda i: (i, 0))
    vec = pl.BlockSpec((1, W), lambda i: (0, 0))
    return pl.pallas_call(
        functools.partial(_ssm_gate_body, groups=groups), grid=(L // bm,),
        in_specs=[row, row, row, pl.BlockSpec((bm, W), lambda i: (i, zo)), vec, vec],
        out_specs=row, out_shape=jax.ShapeDtypeStruct((L, W), BF16),
        compiler_params=_cparams(("parallel",)), name="ssm_gate",
    )(yf, yb, xbc, proj, d_rep.reshape(1, W), norm_g.reshape(1, W))


def _dft_consts(L):
    N2 = DFT_N2
    N = 2 * L
    H1 = L // N2
    n2 = np.arange(N2)[:, None, None]
    k1 = np.arange(H1)[None, :, None]
    n1 = np.arange(H1)[None, None, :]
    m = ((2 * k1 + 1) * (N2 * n1 + n2)) % (2 * N)
    ph = np.pi * m.astype(np.float64) / N
    m_fwd = np.concatenate([np.cos(ph), -np.sin(ph)], axis=1)
    m_inv = (2.0 / N) * np.concatenate([np.cos(ph), -np.sin(ph)], axis=1).transpose(0, 2, 1)
    kk = (np.arange(N2)[:, None] * np.arange(N2)[None, :]) % N2
    c2 = np.cos(2 * np.pi * kk / N2)
    s2 = np.sin(2 * np.pi * kk / N2)
    f2_fwd = np.block([[c2, s2], [-s2, c2]])
    f2_inv = np.block([[c2, -s2], [s2, c2]])
    as_bf = lambda a: jnp.asarray(a.astype(BF16))
    return as_bf(m_fwd), as_bf(m_inv), as_bf(f2_fwd), as_bf(f2_inv)


def _pitch(rows):
    p = rows + SUBLANES
    assert (p // SUBLANES) % 2 == 1
    return p


def _stage3(a_s, f2_ref, k1, H1, pitch):
    N2 = DFT_N2
    ar = a_s[pl.ds(k1, N2, stride=pitch), :]
    ai = a_s[pl.ds(H1 + k1, N2, stride=pitch), :]
    av = jnp.concatenate([ar, ai], axis=0).astype(BF16)
    return _dot(f2_ref[...], av)


def _fwd_body(u_ref, m_ref, f2_ref, x_ref, a_s, *, H1, ks):
    N2 = DFT_N2
    pitch = _pitch(2 * H1)
    s = pl.program_id(1)

    @pl.when(s == 0)
    def _():
        def body(n2, carry):
            u = u_ref[pl.ds(n2, H1, stride=N2), :]
            a_s[pl.ds(pl.multiple_of(n2 * pitch, SUBLANES), 2 * H1), :] = _dot(m_ref[n2], u.astype(BF16))
            return carry
        lax.fori_loop(0, N2, body, 0)

    def kbody(kk, carry):
        x = _stage3(a_s, f2_ref, s * ks + kk, H1, pitch)
        x_ref[kk, 0] = x[:N2]
        x_ref[kk, 1] = x[N2:]
        return carry
    lax.fori_loop(0, ks, kbody, 0)


def _hy_fwd(u, col_off, C, consts):
    L = u.shape[0]
    N2, cb = DFT_N2, HY_CB
    H1 = L // N2
    ks = _pick(H1, 16, 1)
    m_fwd, _, f2_fwd, _ = consts
    co = col_off // cb
    return pl.pallas_call(
        functools.partial(_fwd_body, H1=H1, ks=ks), grid=(C // cb, H1 // ks),
        in_specs=[pl.BlockSpec((L, cb), lambda j, s: (0, co + j)),
                  pl.BlockSpec((N2, 2 * H1, H1), lambda j, s: (0, 0, 0)),
                  pl.BlockSpec((2 * N2, 2 * N2), lambda j, s: (0, 0))],
        out_specs=pl.BlockSpec((ks, 2, N2, cb), lambda j, s: (s, 0, 0, j)),
        out_shape=jax.ShapeDtypeStruct((H1, 2, N2, C), F32),
        scratch_shapes=[pltpu.VMEM((N2 * _pitch(2 * H1), cb), F32)],
        compiler_params=_cparams(("parallel", "arbitrary")), name="hy_fwd",
    )(u, m_fwd, f2_fwd)


def _inv_body(x_ref, g_ref, m_ref, f2_ref, y_ref, b_s, *, H1, ks):
    N2 = DFT_N2
    pitch = _pitch(2 * N2)
    s = pl.program_id(1)

    def kbody(kk, carry):
        xr, xi = x_ref[kk, 0], x_ref[kk, 1]
        gr, gi = g_ref[kk, 0], g_ref[kk, 1]
        yr = xr * gr - xi * gi
        yi = xr * gi + xi * gr
        b = _dot(f2_ref[...], jnp.concatenate([yr, yi], axis=0).astype(BF16))
        b_s[pl.ds(pl.multiple_of((s * ks + kk) * pitch, SUBLANES), 2 * N2), :] = b
        return carry
    lax.fori_loop(0, ks, kbody, 0)

    @pl.when(s == pl.num_programs(1) - 1)
    def _():
        def body(n2, carry):
            br = b_s[pl.ds(n2, H1, stride=pitch), :]
            bi = b_s[pl.ds(N2 + n2, H1, stride=pitch), :]
            y = _dot(m_ref[n2], jnp.concatenate([br, bi], axis=0).astype(BF16))
            y_ref[pl.ds(n2, H1, stride=N2), :] = y
            return carry
        lax.fori_loop(0, N2, body, 0)


def _hy_inv(x, g_all, order, consts, L):
    H1, _, N2, C = x.shape
    cb = HY_CB
    ks = _pick(H1, 16, 1)
    _, m_inv, _, f2_inv = consts
    return pl.pallas_call(
        functools.partial(_inv_body, H1=H1, ks=ks), grid=(C // cb, H1 // ks),
        in_specs=[pl.BlockSpec((ks, 2, N2, cb), lambda j, s: (s, 0, 0, j)),
                  pl.BlockSpec((None, ks, 2, N2, cb), lambda j, s: (order, s, 0, 0, j)),
                  pl.BlockSpec((N2, H1, 2 * H1), lambda j, s: (0, 0, 0)),
                  pl.BlockSpec((2 * N2, 2 * N2), lambda j, s: (0, 0))],
        out_specs=pl.BlockSpec((L, cb), lambda j, s: (0, j)),
        out_shape=jax.ShapeDtypeStruct((L, C), F32),
        scratch_shapes=[pltpu.VMEM((H1 * _pitch(2 * N2), cb), F32)],
        compiler_params=_cparams(("parallel", "arbitrary")), name="hy_inv",
    )(x, g_all, m_inv, f2_inv)


def _pos_features(L, emb):
    N2 = DFT_N2
    H1 = L // N2
    bands = (emb - 1) // 2
    r = jnp.arange(L)
    lag = (r % H1) * N2 + (r // H1)
    f = jnp.linspace(1e-4, bands - 1, bands, dtype=F32)[None, :]

    def feats(lg):
        lgf = lg.astype(F32)[:, None]
        t = lgf / (L - 1)
        w = 2.0 * math.pi * lgf / L
        z = jnp.concatenate([t, jnp.cos(f * w), -jnp.sin(f * w)], axis=-1)
        return jnp.pad(z, ((0, 0), (0, LANES - emb)))
    return jnp.concatenate([feats(lag), feats(jnp.where(lag == 0, 0, L - lag))], axis=-1)


def _filter_mlp_body(z_ref, w1_ref, b1_ref, w2_ref, b2_ref, w3_ref, b3_ref, fr_ref, o_ref):
    h = jnp.sin(fr_ref[0:1, :] * (_dot_f32(z_ref[...], w1_ref[...]) + b1_ref[...]))
    h = jnp.sin(fr_ref[1:2, :] * (_dot_f32(h, w2_ref[...]) + b2_ref[...]))
    h = jnp.sin(fr_ref[2:3, :] * (_dot_f32(h, w3_ref[...]) + b3_ref[...]))
    o_ref[...] = h.astype(o_ref.dtype)


def _bdiag(w):
    a, b = w.shape
    z = jnp.zeros((a, b), w.dtype)
    return jnp.concatenate([jnp.concatenate([w, z], axis=1), jnp.concatenate([z, w], axis=1)], axis=0)


def _filter_mlp(z2, w1, b1, w2, b2, w3, b3, fr):
    L = z2.shape[0]
    emb, hd = w1.shape
    assert 2 * hd == LANES
    w1p = jnp.zeros((2 * LANES, LANES), F32).at[:emb, :hd].set(w1).at[LANES:LANES + emb, hd:].set(w1)
    tile2 = lambda v: jnp.concatenate([v, v], axis=-1)
    bm = _pick(L, 1024, SUBLANES)
    full = lambda a: pl.BlockSpec(a.shape, lambda i: (0,) * a.ndim)
    args = (w1p, tile2(b1).reshape(1, LANES), _bdiag(w2), tile2(b2).reshape(1, LANES),
            _bdiag(w3), tile2(b3).reshape(1, LANES), tile2(fr))
    return pl.pallas_call(
        _filter_mlp_body, grid=(L // bm,),
        in_specs=[pl.BlockSpec((bm, 2 * LANES), lambda i: (i, 0))] + [full(a) for a in args],
        out_specs=pl.BlockSpec((bm, LANES), lambda i: (i, 0)),
        out_shape=jax.ShapeDtypeStruct((L, LANES), BF16),
        compiler_params=_cparams(("parallel",)), name="hy_filter_mlp",
    )(z2, *args)


def _filt_body(hm_ref, w2_ref, ad_ref, m_ref, f2_ref, g_ref, a_s, inv_s, *, L, H1, ks):
    N2, cb = DFT_N2, HY_CB
    pitch = _pitch(2 * H1)
    s = pl.program_id(2)

    @pl.when(s == 0)
    def _():
        absd = ad_ref[...]
        n1 = lax.broadcasted_iota(jnp.int32, (H1, cb), 0)
        sgn = jnp.where(n1 % 2 == 0, 1.0, -1.0)

        def body(n2, nrm):
            hs = hm_ref[pl.ds(pl.multiple_of(n2 * H1, SUBLANES), H1), :]
            gp = _dot(hs, w2_ref[...])
            lag = n1 * N2 + n2
            tf = lag.astype(F32) / (L - 1)
            tb = (L - lag).astype(F32) / (L - 1)
            gf = gp[:, :cb] * jnp.exp(-tf * absd)
            gb = jnp.where(lag == 0, 0.0, -gp[:, cb:] * jnp.exp(-tb * absd))
            nrm = nrm + jnp.sum(jnp.abs(gf) + jnp.abs(gb), axis=0, keepdims=True)
            m = m_ref[n2]
            a1 = _dot(m, gf.astype(BF16))
            a2 = _dot(m, gb.astype(BF16))
            base = pl.multiple_of(n2 * pitch, SUBLANES)
            a_s[pl.ds(base, H1), :] = a1[:H1] + sgn * a2[H1:]
            a_s[pl.ds(base + H1, H1), :] = a1[H1:] - sgn * a2[:H1]
            return nrm
        nrm = lax.fori_loop(0, N2, body, jnp.zeros((1, cb), F32))
        inv_s[...] = jnp.broadcast_to(1.0 / nrm, inv_s.shape)

    scale = inv_s[0:1, :]

    def kbody(kk, carry):
        x = _stage3(a_s, f2_ref, s * ks + kk, H1, pitch) * scale
        g_ref[kk, 0] = x[:N2]
        g_ref[kk, 1] = x[N2:]
        return carry
    lax.fori_loop(0, ks, kbody, 0)


def _hy_filters(hm, wout, C, L, consts):
    N2, cb = DFT_N2, HY_CB
    H1 = L // N2
    ks = _pick(H1, 16, 1)
    hd = wout.shape[0]
    nj = C // cb
    m_fwd, _, f2_fwd, _ = consts
    w4 = wout.reshape(hd, 2, 2, nj, cb)
    wf = jnp.transpose(w4[:, :, 0], (1, 2, 0, 3))
    wb = jnp.transpose(w4[:, :, 1], (1, 2, 0, 3))
    z = jnp.zeros_like(wf)
    w2 = jnp.concatenate([jnp.concatenate([wf, z], axis=-1), jnp.concatenate([z, wb], axis=-1)], axis=2)
    w2 = w2.astype(BF16)
    deltas = jnp.linspace(math.log(HY_FAST_DECAY) / HY_DECAY_TARGET,
                          math.log(HY_SLOW_DECAY) / HY_DECAY_TARGET, C, dtype=F32)
    absd = jnp.abs(deltas).reshape(1, C)
    return pl.pallas_call(
        functools.partial(_filt_body, L=L, H1=H1, ks=ks), grid=(2, nj, H1 // ks),
        in_specs=[pl.BlockSpec((L, LANES), lambda o, j, s: (0, 0)),
                  pl.BlockSpec((None, None, 2 * hd, 2 * cb), lambda o, j, s: (o, j, 0, 0)),
                  pl.BlockSpec((1, cb), lambda o, j, s: (0, j)),
                  pl.BlockSpec((N2, 2 * H1, H1), lambda o, j, s: (0, 0, 0)),
                  pl.BlockSpec((2 * N2, 2 * N2), lambda o, j, s: (0, 0))],
        out_specs=pl.BlockSpec((None, ks, 2, N2, cb), lambda o, j, s: (o, s, 0, 0, j)),
        out_shape=jax.ShapeDtypeStruct((2, H1, 2, N2, C), F32),
        scratch_shapes=[pltpu.VMEM((N2 * _pitch(2 * H1), cb), F32), pltpu.VMEM((SUBLANES, cb), F32)],
        compiler_params=_cparams(("parallel", "parallel", "arbitrary")), name="hy_filters",
    )(hm, w2, absd, m_fwd, f2_fwd)


def _hy_gate1_body(gate_ref, v_ref, y_ref, sk_ref, o_ref):
    o_ref[...] = gate_ref[...] * (y_ref[...] + sk_ref[...] * v_ref[...])


def _hy_gate1(uh, gate_off, v_off, y, sk):
    L, C = y.shape
    bm = _pick(L, 512, SUBLANES)
    go, vo = gate_off // C, v_off // C
    row = pl.BlockSpec((bm, C), lambda i: (i, 0))
    return pl.pallas_call(
        _hy_gate1_body, grid=(L // bm,),
        in_specs=[pl.BlockSpec((bm, C), lambda i: (i, go)), pl.BlockSpec((bm, C), lambda i: (i, vo)), row,
                  pl.BlockSpec((1, C), lambda i: (0, 0))],
        out_specs=row, out_shape=jax.ShapeDtypeStruct((L, C), F32),
        compiler_params=_cparams(("parallel",)), name="hy_gate1",
    )(uh, uh, y, sk.reshape(1, C))


def _hy_gate2_body(gate_ref, v_ref, y_ref, sk_ref, g_ref, o_ref):
    z = gate_ref[...] * (y_ref[...] + sk_ref[...] * v_ref[...])
    ms = jnp.mean(z * z, axis=-1, keepdims=True)
    o_ref[...] = (z * lax.rsqrt(ms + NORM_EPS) * g_ref[...]).astype(o_ref.dtype)


def _hy_gate2(uh, gate_off, v, y, sk, g):
    L, C = y.shape
    bm = _pick(L, 512, SUBLANES)
    go = gate_off // C
    row = pl.BlockSpec((bm, C), lambda i: (i, 0))
    vec = pl.BlockSpec((1, C), lambda i: (0, 0))
    return pl.pallas_call(
        _hy_gate2_body, grid=(L // bm,),
        in_specs=[pl.BlockSpec((bm, C), lambda i: (i, go)), row, row, vec, vec],
        out_specs=row, out_shape=jax.ShapeDtypeStruct((L, C), BF16),
        compiler_params=_cparams(("parallel",)), name="hy_gate2",
    )(uh, v, y, sk.reshape(1, C), g.reshape(1, C))


def kernel(x, norm_mix_g, w_in, hy_conv_w, hy_conv_b, hy_pos_w1, hy_pos_b1, hy_pos_w2, hy_pos_b2, hy_pos_w3, hy_pos_b3, hy_sin_freq, hy_pos_wout, hy_skip, hy_out_g, ssm_conv_w, ssm_conv_b, ssm_A_log, ssm_dt_bias, ssm_D, ssm_out_g, w_out, norm_mlp_g, w_up, w_down, final_norm_g):
    B, L, D = x.shape
    depth = w_in.shape[0]
    CH = hy_out_g.shape[1]
    WS = ssm_out_g.shape[1]
    heads = ssm_D.shape[1]
    xbc_w = ssm_conv_w.shape[2]
    groups = (xbc_w - WS) // (2 * SSM_STATE)
    o1 = 3 * CH
    o2 = o1 + WS
    o3 = o2 + xbc_w
    assert WS == heads * SSM_HEAD_DIM and heads <= LANES and L % DFT_N2 == 0 and CH % HY_CB == 0

    consts = _dft_consts(L)
    z2 = _pos_features(L, hy_pos_w1.shape[1])
    w_in_main = w_in[:, :, :o3].astype(BF16)
    w_in_dt = jnp.pad(w_in[:, :, o3:], ((0, 0), (0, 0), (0, LANES - heads))).astype(BF16)
    w_out_b = w_out.astype(BF16)
    w_up_b = w_up.astype(BF16)
    w_down_b = w_down.astype(BF16)
    pad_h = lambda v: jnp.pad(v, (0, LANES - heads)).reshape(1, LANES)
    d_rep = jnp.repeat(ssm_D, SSM_HEAD_DIM, axis=-1)

    outs = []
    for b in range(B):
        xb = x[b]
        delta = None
        for l in range(depth):
            if delta is None:
                h = _rmsnorm(xb, norm_mix_g[l], BF16)
            else:
                xb, h = _add_rmsnorm(xb, delta, norm_mix_g[l], BF16)
            proj = _matmul(h, w_in_main[l])
            dt_raw = _matmul(h, w_in_dt[l], bm_t=2048)

            uh = _dwconv(proj, 0, o1, hy_conv_w[l], hy_conv_b[l], "none")
            hm = _filter_mlp(z2, hy_pos_w1[l], hy_pos_b1[l], hy_pos_w2[l], hy_pos_b2[l],
                             hy_pos_w3[l], hy_pos_b3[l], hy_sin_freq[l])
            g_all = _hy_filters(hm, hy_pos_wout[l], CH, L, consts)
            y0 = _hy_inv(_hy_fwd(uh, 2 * CH, CH, consts), g_all, 0, consts, L)
            z1 = _hy_gate1(uh, 0, 2 * CH, y0, hy_skip[l, 0])
            y1 = _hy_inv(_hy_fwd(z1, 0, CH, consts), g_all, 1, consts, L)
            y_hy = _hy_gate2(uh, CH, z1, y1, hy_skip[l, 1], hy_out_g[l])

            xbc = _dwconv(proj, o2, xbc_w, ssm_conv_w[l], ssm_conv_b[l], "silu")
            yf = _ssd(xbc, dt_raw, pad_h(ssm_dt_bias[l, 0]), pad_h(ssm_A_log[l, 0]),
                      reverse=False, heads=heads, groups=groups)
            yb = _ssd(xbc, dt_raw, pad_h(ssm_dt_bias[l, 1]), pad_h(ssm_A_log[l, 1]),
                      reverse=True, heads=heads, groups=groups)
            y_ssm = _ssm_gate(yf, yb, xbc, proj, o1, d_rep[l], ssm_out_g[l], groups)

            xb = _out_matmul(y_hy, y_ssm, w_out_b[l, :CH], w_out_b[l, CH:], xb)
            h2 = _rmsnorm(xb, norm_mlp_g[l], BF16)
            delta = _mlp(h2, w_up_b[l], w_down_b[l])
        _, y = _add_rmsnorm(xb, delta, final_norm_g, F32)
        outs.append(y)
    return jnp.stack(outs, axis=0)
```

```python
import functools
import math

import numpy as np
import jax
import jax.numpy as jnp
from jax import lax
from jax.experimental import pallas as pl
from jax.experimental.pallas import tpu as pltpu

F32 = jnp.float32
BF16 = jnp.bfloat16

NORM_EPS = 1e-5
SSM_HEAD_DIM = 64
SSM_STATE = 128
SSM_CHUNK = 256
HY_FAST_DECAY = 0.3
HY_SLOW_DECAY = 1.5
HY_DECAY_TARGET = 1e-2

LANES = 128
SUBLANES = 8
DFT_N2 = 128
HY_CB = 128
DFT_UNROLL = 8
VMEM_LIMIT_MB = 56


def _cparams(sem, vmem_mb=VMEM_LIMIT_MB):
    return pltpu.CompilerParams(dimension_semantics=sem, vmem_limit_bytes=vmem_mb * 1024 * 1024)


def _pick(n, target, mult):
    best = None
    for d in range(mult, min(n, target) + 1, mult):
        if n % d == 0:
            best = d
    assert best is not None, (n, target, mult)
    return best


def _dot(a, b):
    return jnp.dot(a, b, preferred_element_type=F32)


def _split3(a):
    hi = a.astype(BF16)
    r1 = a - hi.astype(F32)
    mid = r1.astype(BF16)
    lo = (r1 - mid.astype(F32)).astype(BF16)
    return hi, mid, lo


def _dot_f32(a, b):
    ah = a.astype(BF16)
    al = (a - ah.astype(F32)).astype(BF16)
    bh = b.astype(BF16)
    bl = (b - bh.astype(F32)).astype(BF16)
    return _dot(ah, bh) + (_dot(ah, bl) + _dot(al, bh))


def _silu(x):
    return x / (1.0 + jnp.exp(-x))


def _rmsnorm_body(x_ref, g_ref, h_ref):
    x = x_ref[...]
    ms = jnp.mean(x * x, axis=-1, keepdims=True)
    h_ref[...] = (x * lax.rsqrt(ms + NORM_EPS) * g_ref[...]).astype(h_ref.dtype)


def _rmsnorm(x, g, out_dtype):
    M, D = x.shape
    bm = _pick(M, 256, SUBLANES)
    return pl.pallas_call(
        _rmsnorm_body, grid=(M // bm,),
        in_specs=[pl.BlockSpec((bm, D), lambda i: (i, 0)), pl.BlockSpec((1, D), lambda i: (0, 0))],
        out_specs=pl.BlockSpec((bm, D), lambda i: (i, 0)),
        out_shape=jax.ShapeDtypeStruct((M, D), out_dtype),
        compiler_params=_cparams(("parallel",)), name="rmsnorm",
    )(x, g.reshape(1, D))


def _add_rmsnorm_body(x_ref, d_ref, g_ref, xo_ref, h_ref):
    x = x_ref[...] + d_ref[...]
    xo_ref[...] = x
    ms = jnp.mean(x * x, axis=-1, keepdims=True)
    h_ref[...] = (x * lax.rsqrt(ms + NORM_EPS) * g_ref[...]).astype(h_ref.dtype)


def _add_rmsnorm(x, delta, g, out_dtype):
    M, D = x.shape
    bm = _pick(M, 256, SUBLANES)
    row = pl.BlockSpec((bm, D), lambda i: (i, 0))
    return pl.pallas_call(
        _add_rmsnorm_body, grid=(M // bm,),
        in_specs=[row, row, pl.BlockSpec((1, D), lambda i: (0, 0))],
        out_specs=[row, row],
        out_shape=[jax.ShapeDtypeStruct((M, D), F32), jax.ShapeDtypeStruct((M, D), out_dtype)],
        compiler_params=_cparams(("parallel",)), name="add_rmsnorm",
    )(x, delta, g.reshape(1, D))


def _mm_body(a_ref, b_ref, o_ref):
    o_ref[...] = _dot(a_ref[...], b_ref[...]).astype(o_ref.dtype)


def _matmul(a, b_stack, layer, out_dtype=F32, bm_t=1024, bn_t=1024):
    M, K = a.shape
    N = b_stack.shape[2]
    bm = _pick(M, bm_t, SUBLANES)
    bn = _pick(N, bn_t, LANES)
    return pl.pallas_call(
        _mm_body, grid=(M // bm, N // bn),
        in_specs=[pl.BlockSpec((bm, K), lambda i, j: (i, 0)),
                  pl.BlockSpec((None, K, bn), lambda i, j: (layer, 0, j))],
        out_specs=pl.BlockSpec((bm, bn), lambda i, j: (i, j)),
        out_shape=jax.ShapeDtypeStruct((M, N), out_dtype),
        compiler_params=_cparams(("parallel", "parallel")), name="proj_matmul",
    )(a, b_stack)


def _out_mm_body(a1_ref, a2_ref, b1_ref, b2_ref, r_ref, o_ref):
    acc = _dot(a1_ref[...], b1_ref[...]) + _dot(a2_ref[...], b2_ref[...])
    o_ref[...] = r_ref[...] + acc


def _out_matmul(a1, a2, b_stack, layer, res):
    M, K1 = a1.shape
    assert a2.shape[1] == K1 and b_stack.shape[1] == 2 * K1
    N = b_stack.shape[2]
    bm = _pick(M, 1024, SUBLANES)
    bn = _pick(N, 1024, LANES)
    return pl.pallas_call(
        _out_mm_body, grid=(M // bm, N // bn),
        in_specs=[pl.BlockSpec((bm, K1), lambda i, j: (i, 0)), pl.BlockSpec((bm, K1), lambda i, j: (i, 0)),
                  pl.BlockSpec((None, K1, bn), lambda i, j: (layer, 0, j)),
                  pl.BlockSpec((None, K1, bn), lambda i, j: (layer, 1, j)),
                  pl.BlockSpec((bm, bn), lambda i, j: (i, j))],
        out_specs=pl.BlockSpec((bm, bn), lambda i, j: (i, j)),
        out_shape=jax.ShapeDtypeStruct((M, N), F32),
        compiler_params=_cparams(("parallel", "parallel")), name="out_matmul",
    )(a1, a2, b_stack, b_stack, res)


def _mlp_body(h_ref, wu_ref, wd_ref, o_ref, *, cw):
    j = pl.program_id(1)

    @pl.when(j == 0)
    def _():
        o_ref[...] = jnp.zeros_like(o_ref)

    t = _dot(h_ref[...], wu_ref[...])
    t = jnp.square(jnp.maximum(t, 0.0)).astype(BF16)
    for c in range(o_ref.shape[1] // cw):
        o_ref[:, c * cw:(c + 1) * cw] += _dot(t, wd_ref[:, c * cw:(c + 1) * cw])


def _mlp(h, w_up_stack, w_down_stack, layer):
    M, D = h.shape
    F = w_up_stack.shape[2]
    bm = _pick(M, 512, SUBLANES)
    bf = _pick(F, 512, LANES)
    cw = _pick(D, 512, LANES)
    return pl.pallas_call(
        functools.partial(_mlp_body, cw=cw), grid=(M // bm, F // bf),
        in_specs=[pl.BlockSpec((bm, D), lambda i, j: (i, 0)),
                  pl.BlockSpec((None, D, bf), lambda i, j: (layer, 0, j)),
                  pl.BlockSpec((None, bf, D), lambda i, j: (layer, j, 0))],
        out_specs=pl.BlockSpec((bm, D), lambda i, j: (i, 0)),
        out_shape=jax.ShapeDtypeStruct((M, D), F32),
        compiler_params=_cparams(("parallel", "arbitrary")), name="mlp",
    )(h, w_up_stack, w_down_stack)


def _dwconv_body(p_ref, pv_ref, nx_ref, w_ref, b_ref, o_ref, *, act):
    i = pl.program_id(0)
    last = pl.num_programs(0) - 1
    u = p_ref[...]
    bm = u.shape[0]
    prev_row = jnp.where(i > 0, pv_ref[SUBLANES - 1:SUBLANES, :], 0.0)
    next_row = jnp.where(i < last, nx_ref[0:1, :], 0.0)
    rows = lax.broadcasted_iota(jnp.int32, u.shape, 0)
    um = jnp.where(rows == 0, prev_row, pltpu.roll(u, 1, 0))
    up = jnp.where(rows == bm - 1, next_row, pltpu.roll(u, bm - 1, 0))
    y = w_ref[0:1, :] * um
    y = y + w_ref[1:2, :] * u
    y = y + w_ref[2:3, :] * up
    y = y + b_ref[...]
    if act == "silu":
        y = _silu(y)
    o_ref[...] = y


def _dwconv(p, col_off, C, w, b, act):
    L = p.shape[0]
    bm = _pick(L, 1024, SUBLANES)
    bc = _pick(math.gcd(C, col_off) if col_off else C, 1024, LANES)
    co = col_off // bc
    hb = bm // SUBLANES
    nrb = L // SUBLANES
    return pl.pallas_call(
        functools.partial(_dwconv_body, act=act), grid=(L // bm, C // bc),
        in_specs=[pl.BlockSpec((bm, bc), lambda i, j: (i, co + j)),
                  pl.BlockSpec((SUBLANES, bc), lambda i, j: (jnp.maximum(i * hb - 1, 0), co + j)),
                  pl.BlockSpec((SUBLANES, bc), lambda i, j: (jnp.minimum((i + 1) * hb, nrb - 1), co + j)),
                  pl.BlockSpec((3, bc), lambda i, j: (0, j)),
                  pl.BlockSpec((1, bc), lambda i, j: (0, j))],
        out_specs=pl.BlockSpec((bm, bc), lambda i, j: (i, j)),
        out_shape=jax.ShapeDtypeStruct((L, C), F32),
        compiler_params=_cparams(("parallel", "parallel")), name="dwconv_" + act,
    )(p, p, p, w, b.reshape(1, C))


def _lane_groups(cols, width):
    q = cols[0].shape[0]
    n = len(cols)
    lane = lax.broadcasted_iota(jnp.int32, (q, n * width), 1)
    out = jnp.broadcast_to(cols[n - 1], (q, n * width))
    for r in range(n - 2, -1, -1):
        out = jnp.where(lane < (r + 1) * width, jnp.broadcast_to(cols[r], (q, n * width)), out)
    return out


def _ssd_body(xbc_ref, dt_ref, bias_ref, alog_ref, y_ref, s_ref, *, reverse, heads, groups):
    Q = SSM_CHUNK
    P = SSM_HEAD_DIM
    N = SSM_STATE
    R = heads // groups
    W = heads * P
    step = pl.program_id(0)

    @pl.when(step == 0)
    def _():
        s_ref[...] = jnp.zeros_like(s_ref)

    x = dt_ref[...] + bias_ref[...]
    dt = jnp.maximum(x, 0.0) + jnp.log1p(jnp.exp(-jnp.abs(x)))
    a = dt * (-jnp.exp(alog_ref[...]))
    ri = lax.broadcasted_iota(jnp.int32, (Q, Q), 0)
    ci = lax.broadcasted_iota(jnp.int32, (Q, Q), 1)
    mask = (ci >= ri) if reverse else (ri >= ci)
    tri = jnp.where(mask, 1.0, 0.0).astype(BF16)
    ah, am, al = _split3(a)
    acs = _dot(tri, ah) + (_dot(tri, am) + _dot(tri, al))
    tot = acs[0:1, :] if reverse else acs[Q - 1:Q, :]
    e_col = jnp.exp(acs)
    w_col = jnp.exp(tot - acs) * dt
    acs_row = acs.T
    dt_row = dt.T
    tcol = 0 if reverse else Q - 1

    for g in range(groups):
        bg = xbc_ref[:, W + g * N:W + (g + 1) * N].astype(BF16)
        cg = xbc_ref[:, W + groups * N + g * N:W + groups * N + (g + 1) * N].astype(BF16)
        xg = xbc_ref[:, g * R * P:(g + 1) * R * P]
        cb = lax.dot_general(cg, bg, (((1,), (1,)), ((), ())), preferred_element_type=F32)
        lane = lax.broadcasted_iota(jnp.int32, (Q, R * P), 1)
        ws, xs = [], []
        for r in range(R):
            h = g * R + r
            diff = acs[:, h:h + 1] - acs_row[h:h + 1, :]
            lm = jnp.exp(jnp.where(mask, diff, -jnp.inf))
            ws.append((cb * lm * dt_row[h:h + 1, :]).astype(BF16))
            xs.append(jnp.where((lane >= r * P) & (lane < (r + 1) * P), xg, 0.0).astype(BF16))
        y_diag = _dot(jnp.concatenate(ws, axis=1), jnp.concatenate(xs, axis=0))
        heads_g = range(g * R, (g + 1) * R)
        e_g = _lane_groups([e_col[:, h:h + 1] for h in heads_g], P)
        w_g = _lane_groups([w_col[:, h:h + 1] for h in heads_g], P)
        t_g = _lane_groups([jnp.exp(acs_row[h:h + 1, tcol:tcol + 1]) for h in heads_g], P)
        s_old = s_ref[g]
        y_off = _dot(cg, s_old.astype(BF16)) * e_g
        y_ref[:, g * R * P:(g + 1) * R * P] = y_diag + y_off
        s_new = lax.dot_general(bg, (xg * w_g).astype(BF16), (((0,), (0,)), ((), ())),
                                preferred_element_type=F32)
        s_ref[g] = s_old * t_g + s_new


def _ssd(xbc, dt_raw, dt_bias, a_log, *, reverse, heads, groups):
    L = xbc.shape[0]
    Q = SSM_CHUNK
    assert L % Q == 0
    nc = L // Q
    W = heads * SSM_HEAD_DIM
    R = heads // groups
    cmap = (lambda i: (nc - 1 - i, 0)) if reverse else (lambda i: (i, 0))
    return pl.pallas_call(
        functools.partial(_ssd_body, reverse=reverse, heads=heads, groups=groups), grid=(nc,),
        in_specs=[pl.BlockSpec((Q, xbc.shape[1]), cmap), pl.BlockSpec((Q, LANES), cmap),
                  pl.BlockSpec((1, LANES), lambda i: (0, 0)), pl.BlockSpec((1, LANES), lambda i: (0, 0))],
        out_specs=pl.BlockSpec((Q, W), cmap),
        out_shape=jax.ShapeDtypeStruct((L, W), F32),
        scratch_shapes=[pltpu.VMEM((groups, SSM_STATE, R * SSM_HEAD_DIM), F32)],
        compiler_params=_cparams(("arbitrary",)), name="ssd_bwd" if reverse else "ssd_fwd",
    )(xbc, dt_raw, dt_bias, a_log)


def _ssm_gate_body(yf_ref, yb_ref, xh_ref, z_ref, d_ref, g_ref, o_ref, *, groups):
    y = (yf_ref[...] + yb_ref[...] + d_ref[...] * xh_ref[...]) * _silu(z_ref[...])
    W = y.shape[1]
    gw = W // groups
    for g in range(groups):
        yg = y[:, g * gw:(g + 1) * gw]
        ms = jnp.mean(yg * yg, axis=-1, keepdims=True)
        o_ref[:, g * gw:(g + 1) * gw] = (yg * lax.rsqrt(ms + NORM_EPS) * g_ref[:, g * gw:(g + 1) * gw]).astype(o_ref.dtype)


def _ssm_gate(yf, yb, xbc, proj, z_off, d_rep, norm_g, groups):
    L, W = yf.shape
    bm = _pick(L, 256, SUBLANES)
    assert z_off % W == 0
    zo = z_off // W
    row = pl.BlockSpec((bm, W), lambda i: (i, 0))
    vec = pl.BlockSpec((1, W), lambda i: (0, 0))
    return pl.pallas_call(
        functools.partial(_ssm_gate_body, groups=groups), grid=(L // bm,),
        in_specs=[row, row, row, pl.BlockSpec((bm, W), lambda i: (i, zo)), vec, vec],
        out_specs=row, out_shape=jax.ShapeDtypeStruct((L, W), BF16),
        compiler_params=_cparams(("parallel",)), name="ssm_gate",
    )(yf, yb, xbc, proj, d_rep.reshape(1, W), norm_g.reshape(1, W))


def _dft_consts(L):
    N2 = DFT_N2
    N = 2 * L
    H1 = L // N2
    n2 = np.arange(N2)[:, None, None]
    k1 = np.arange(H1)[None, :, None]
    n1 = np.arange(H1)[None, None, :]
    m = ((2 * k1 + 1) * (N2 * n1 + n2)) % (2 * N)
    ph = np.pi * m.astype(np.float64) / N
    m_fwd = np.concatenate([np.cos(ph), -np.sin(ph)], axis=1)
    m_inv = (2.0 / N) * np.concatenate([np.cos(ph), -np.sin(ph)], axis=1).transpose(0, 2, 1)
    kk = (np.arange(N2)[:, None] * np.arange(N2)[None, :]) % N2
    c2 = np.cos(2 * np.pi * kk / N2)
    s2 = np.sin(2 * np.pi * kk / N2)
    f2_fwd = np.block([[c2, s2], [-s2, c2]])
    f2_inv = np.block([[c2, -s2], [s2, c2]])
    as_bf = lambda a: jnp.asarray(a.astype(BF16))
    return as_bf(m_fwd), as_bf(m_inv), as_bf(f2_fwd), as_bf(f2_inv)


def _pitch(rows):
    p = rows + SUBLANES
    assert (p // SUBLANES) % 2 == 1
    return p


def _stage3(a_s, f2_ref, k1, H1, pitch):
    N2 = DFT_N2
    ar = a_s[pl.ds(k1, N2, stride=pitch), :]
    ai = a_s[pl.ds(H1 + k1, N2, stride=pitch), :]
    av = jnp.concatenate([ar, ai], axis=0).astype(BF16)
    return _dot(f2_ref[...], av)


def _fwd_body(u_ref, m_ref, f2_ref, x_ref, a_s, *, H1, ks):
    N2 = DFT_N2
    pitch = _pitch(2 * H1)
    s = pl.program_id(1)

    @pl.when(s == 0)
    def _():
        def body(n2, carry):
            u = u_ref[pl.ds(n2, H1, stride=N2), :]
            a_s[pl.ds(pl.multiple_of(n2 * pitch, SUBLANES), 2 * H1), :] = _dot(m_ref[n2], u.astype(BF16))
            return carry
        lax.fori_loop(0, N2, body, 0, unroll=DFT_UNROLL)

    def kbody(kk, carry):
        x = _stage3(a_s, f2_ref, s * ks + kk, H1, pitch)
        x_ref[kk, 0] = x[:N2]
        x_ref[kk, 1] = x[N2:]
        return carry
    lax.fori_loop(0, ks, kbody, 0, unroll=DFT_UNROLL)


def _hy_fwd(u, col_off, C, consts):
    L = u.shape[0]
    N2, cb = DFT_N2, HY_CB
    H1 = L // N2
    ks = _pick(H1, 16, 1)
    m_fwd, _, f2_fwd, _ = consts
    co = col_off // cb
    return pl.pallas_call(
        functools.partial(_fwd_body, H1=H1, ks=ks), grid=(C // cb, H1 // ks),
        in_specs=[pl.BlockSpec((L, cb), lambda j, s: (0, co + j)),
                  pl.BlockSpec((N2, 2 * H1, H1), lambda j, s: (0, 0, 0)),
                  pl.BlockSpec((2 * N2, 2 * N2), lambda j, s: (0, 0))],
        out_specs=pl.BlockSpec((ks, 2, N2, cb), lambda j, s: (s, 0, 0, j)),
        out_shape=jax.ShapeDtypeStruct((H1, 2, N2, C), F32),
        scratch_shapes=[pltpu.VMEM((N2 * _pitch(2 * H1), cb), F32)],
        compiler_params=_cparams(("parallel", "arbitrary")), name="hy_fwd",
    )(u, m_fwd, f2_fwd)


def _inv_body(x_ref, g_ref, m_ref, f2_ref, y_ref, b_s, *, H1, ks):
    N2 = DFT_N2
    pitch = _pitch(2 * N2)
    s = pl.program_id(1)

    def kbody(kk, carry):
        xr, xi = x_ref[kk, 0], x_ref[kk, 1]
        gr, gi = g_ref[kk, 0], g_ref[kk, 1]
        yr = xr * gr - xi * gi
        yi = xr * gi + xi * gr
        b = _dot(f2_ref[...], jnp.concatenate([yr, yi], axis=0).astype(BF16))
        b_s[pl.ds(pl.multiple_of((s * ks + kk) * pitch, SUBLANES), 2 * N2), :] = b
        return carry
    lax.fori_loop(0, ks, kbody, 0, unroll=DFT_UNROLL)

    @pl.when(s == pl.num_programs(1) - 1)
    def _():
        def body(n2, carry):
            br = b_s[pl.ds(n2, H1, stride=pitch), :]
            bi = b_s[pl.ds(N2 + n2, H1, stride=pitch), :]
            y = _dot(m_ref[n2], jnp.concatenate([br, bi], axis=0).astype(BF16))
            y_ref[pl.ds(n2, H1, stride=N2), :] = y
            return carry
        lax.fori_loop(0, N2, body, 0, unroll=DFT_UNROLL)


def _hy_inv(x, g_all, order, consts, L):
    H1, _, N2, C = x.shape
    cb = HY_CB
    ks = _pick(H1, 16, 1)
    _, m_inv, _, f2_inv = consts
    return pl.pallas_call(
        functools.partial(_inv_body, H1=H1, ks=ks), grid=(C // cb, H1 // ks),
        in_specs=[pl.BlockSpec((ks, 2, N2, cb), lambda j, s: (s, 0, 0, j)),
                  pl.BlockSpec((None, ks, 2, N2, cb), lambda j, s: (order, s, 0, 0, j)),
                  pl.BlockSpec((N2, H1, 2 * H1), lambda j, s: (0, 0, 0)),
                  pl.BlockSpec((2 * N2, 2 * N2), lambda j, s: (0, 0))],
        out_specs=pl.BlockSpec((L, cb), lambda j, s: (0, j)),
        out_shape=jax.ShapeDtypeStruct((L, C), F32),
        scratch_shapes=[pltpu.VMEM((H1 * _pitch(2 * N2), cb), F32)],
        compiler_params=_cparams(("parallel", "arbitrary")), name="hy_inv",
    )(x, g_all, m_inv, f2_inv)


def _pos_features(L, emb):
    N2 = DFT_N2
    H1 = L // N2
    bands = (emb - 1) // 2
    r = jnp.arange(L)
    lag = (r % H1) * N2 + (r // H1)
    f = jnp.linspace(1e-4, bands - 1, bands, dtype=F32)[None, :]

    def feats(lg):
        lgf = lg.astype(F32)[:, None]
        t = lgf / (L - 1)
        w = 2.0 * math.pi * lgf / L
        z = jnp.concatenate([t, jnp.cos(f * w), -jnp.sin(f * w)], axis=-1)
        return jnp.pad(z, ((0, 0), (0, LANES - emb)))
    return jnp.concatenate([feats(lag), feats(jnp.where(lag == 0, 0, L - lag))], axis=-1)


def _filter_mlp_body(z_ref, w1_ref, b1_ref, w2_ref, b2_ref, w3_ref, b3_ref, fr_ref, o_ref):
    h = jnp.sin(fr_ref[0:1, :] * (_dot_f32(z_ref[...], w1_ref[...]) + b1_ref[...]))
    h = jnp.sin(fr_ref[1:2, :] * (_dot_f32(h, w2_ref[...]) + b2_ref[...]))
    h = jnp.sin(fr_ref[2:3, :] * (_dot_f32(h, w3_ref[...]) + b3_ref[...]))
    o_ref[...] = h.astype(o_ref.dtype)


def _bdiag(w):
    a, b = w.shape
    z = jnp.zeros((a, b), w.dtype)
    return jnp.concatenate([jnp.concatenate([w, z], axis=1), jnp.concatenate([z, w], axis=1)], axis=0)


def _filter_mlp(z2, w1, b1, w2, b2, w3, b3, fr):
    L = z2.shape[0]
    emb, hd = w1.shape
    assert 2 * hd == LANES
    w1p = jnp.zeros((2 * LANES, LANES), F32).at[:emb, :hd].set(w1).at[LANES:LANES + emb, hd:].set(w1)
    tile2 = lambda v: jnp.concatenate([v, v], axis=-1)
    bm = _pick(L, 1024, SUBLANES)
    full = lambda a: pl.BlockSpec(a.shape, lambda i: (0,) * a.ndim)
    args = (w1p, tile2(b1).reshape(1, LANES), _bdiag(w2), tile2(b2).reshape(1, LANES),
            _bdiag(w3), tile2(b3).reshape(1, LANES), tile2(fr))
    return pl.pallas_call(
        _filter_mlp_body, grid=(L // bm,),
        in_specs=[pl.BlockSpec((bm, 2 * LANES), lambda i: (i, 0))] + [full(a) for a in args],
        out_specs=pl.BlockSpec((bm, LANES), lambda i: (i, 0)),
        out_shape=jax.ShapeDtypeStruct((L, LANES), BF16),
        compiler_params=_cparams(("parallel",)), name="hy_filter_mlp",
    )(z2, *args)


def _filt_body(hm_ref, w2_ref, ad_ref, m_ref, f2_ref, g_ref, a_s, inv_s, *, L, H1, ks):
    N2, cb = DFT_N2, HY_CB
    pitch = _pitch(2 * H1)
    s = pl.program_id(2)

    @pl.when(s == 0)
    def _():
        absd = ad_ref[...]
        n1 = lax.broadcasted_iota(jnp.int32, (H1, cb), 0)
        sgn = jnp.where(n1 % 2 == 0, 1.0, -1.0)

        def body(n2, nrm):
            hs = hm_ref[pl.ds(pl.multiple_of(n2 * H1, SUBLANES), H1), :]
            gp = _dot(hs, w2_ref[...])
            lag = n1 * N2 + n2
            tf = lag.astype(F32) / (L - 1)
            tb = (L - lag).astype(F32) / (L - 1)
            gf = gp[:, :cb] * jnp.exp(-tf * absd)
            gb = jnp.where(lag == 0, 0.0, -gp[:, cb:] * jnp.exp(-tb * absd))
            nrm = nrm + jnp.sum(jnp.abs(gf) + jnp.abs(gb), axis=0, keepdims=True)
            m = m_ref[n2]
            a1 = _dot(m, gf.astype(BF16))
            a2 = _dot(m, gb.astype(BF16))
            base = pl.multiple_of(n2 * pitch, SUBLANES)
            a_s[pl.ds(base, H1), :] = a1[:H1] + sgn * a2[H1:]
            a_s[pl.ds(base + H1, H1), :] = a1[H1:] - sgn * a2[:H1]
            return nrm
        nrm = lax.fori_loop(0, N2, body, jnp.zeros((1, cb), F32), unroll=DFT_UNROLL)
        inv_s[...] = jnp.broadcast_to(1.0 / nrm, inv_s.shape)

    scale = inv_s[0:1, :]

    def kbody(kk, carry):
        x = _stage3(a_s, f2_ref, s * ks + kk, H1, pitch) * scale
        g_ref[kk, 0] = x[:N2]
        g_ref[kk, 1] = x[N2:]
        return carry
    lax.fori_loop(0, ks, kbody, 0, unroll=DFT_UNROLL)


def _hy_filters(hm, wout, C, L, consts):
    N2, cb = DFT_N2, HY_CB
    H1 = L // N2
    ks = _pick(H1, 16, 1)
    hd = wout.shape[0]
    nj = C // cb
    m_fwd, _, f2_fwd, _ = consts
    w4 = wout.reshape(hd, 2, 2, nj, cb)
    wf = jnp.transpose(w4[:, :, 0], (1, 2, 0, 3))
    wb = jnp.transpose(w4[:, :, 1], (1, 2, 0, 3))
    z = jnp.zeros_like(wf)
    w2 = jnp.concatenate([jnp.concatenate([wf, z], axis=-1), jnp.concatenate([z, wb], axis=-1)], axis=2)
    w2 = w2.astype(BF16)
    deltas = jnp.linspace(math.log(HY_FAST_DECAY) / HY_DECAY_TARGET,
                          math.log(HY_SLOW_DECAY) / HY_DECAY_TARGET, C, dtype=F32)
    absd = jnp.abs(deltas).reshape(1, C)
    return pl.pallas_call(
        functools.partial(_filt_body, L=L, H1=H1, ks=ks), grid=(2, nj, H1 // ks),
        in_specs=[pl.BlockSpec((L, LANES), lambda o, j, s: (0, 0)),
                  pl.BlockSpec((None, None, 2 * hd, 2 * cb), lambda o, j, s: (o, j, 0, 0)),
                  pl.BlockSpec((1, cb), lambda o, j, s: (0, j)),
                  pl.BlockSpec((N2, 2 * H1, H1), lambda o, j, s: (0, 0, 0)),
                  pl.BlockSpec((2 * N2, 2 * N2), lambda o, j, s: (0, 0))],
        out_specs=pl.BlockSpec((None, ks, 2, N2, cb), lambda o, j, s: (o, s, 0, 0, j)),
        out_shape=jax.ShapeDtypeStruct((2, H1, 2, N2, C), F32),
        scratch_shapes=[pltpu.VMEM((N2 * _pitch(2 * H1), cb), F32), pltpu.VMEM((SUBLANES, cb), F32)],
        compiler_params=_cparams(("parallel", "parallel", "arbitrary")), name="hy_filters",
    )(hm, w2, absd, m_fwd, f2_fwd)


def _hy_gate1_body(gate_ref, v_ref, y_ref, sk_ref, o_ref):
    o_ref[...] = gate_ref[...] * (y_ref[...] + sk_ref[...] * v_ref[...])


def _hy_gate1(uh, gate_off, v_off, y, sk):
    L, C = y.shape
    bm = _pick(L, 512, SUBLANES)
    go, vo = gate_off // C, v_off // C
    row = pl.BlockSpec((bm, C), lambda i: (i, 0))
    return pl.pallas_call(
        _hy_gate1_body, grid=(L // bm,),
        in_specs=[pl.BlockSpec((bm, C), lambda i: (i, go)), pl.BlockSpec((bm, C), lambda i: (i, vo)), row,
                  pl.BlockSpec((1, C), lambda i: (0, 0))],
        out_specs=row, out_shape=jax.ShapeDtypeStruct((L, C), F32),
        compiler_params=_cparams(("parallel",)), name="hy_gate1",
    )(uh, uh, y, sk.reshape(1, C))


def _hy_gate2_body(gate_ref, v_ref, y_ref, sk_ref, g_ref, o_ref):
    z = gate_ref[...] * (y_ref[...] + sk_ref[...] * v_ref[...])
    ms = jnp.mean(z * z, axis=-1, keepdims=True)
    o_ref[...] = (z * lax.rsqrt(ms + NORM_EPS) * g_ref[...]).astype(o_ref.dtype)


def _hy_gate2(uh, gate_off, v, y, sk, g):
    L, C = y.shape
    bm = _pick(L, 512, SUBLANES)
    go = gate_off // C
    row = pl.BlockSpec((bm, C), lambda i: (i, 0))
    vec = pl.BlockSpec((1, C), lambda i: (0, 0))
    return pl.pallas_call(
        _hy_gate2_body, grid=(L // bm,),
        in_specs=[pl.BlockSpec((bm, C), lambda i: (i, go)), row, row, vec, vec],
        out_specs=row, out_shape=jax.ShapeDtypeStruct((L, C), BF16),
        compiler_params=_cparams(("parallel",)), name="hy_gate2",
    )(uh, v, y, sk.reshape(1, C), g.reshape(1, C))


def kernel(x, norm_mix_g, w_in, hy_conv_w, hy_conv_b, hy_pos_w1, hy_pos_b1, hy_pos_w2, hy_pos_b2, hy_pos_w3, hy_pos_b3, hy_sin_freq, hy_pos_wout, hy_skip, hy_out_g, ssm_conv_w, ssm_conv_b, ssm_A_log, ssm_dt_bias, ssm_D, ssm_out_g, w_out, norm_mlp_g, w_up, w_down, final_norm_g):
    B, L, D = x.shape
    depth = w_in.shape[0]
    CH = hy_out_g.shape[1]
    WS = ssm_out_g.shape[1]
    heads = ssm_D.shape[1]
    xbc_w = ssm_conv_w.shape[2]
    groups = (xbc_w - WS) // (2 * SSM_STATE)
    o1 = 3 * CH
    o2 = o1 + WS
    o3 = o2 + xbc_w
    assert WS == heads * SSM_HEAD_DIM and heads <= LANES and L % DFT_N2 == 0 and CH % HY_CB == 0

    consts = _dft_consts(L)
    z2 = _pos_features(L, hy_pos_w1.shape[1])
    w_in_main = w_in[:, :, :o3].astype(BF16)
    w_in_dt = jnp.pad(w_in[:, :, o3:], ((0, 0), (0, 0), (0, LANES - heads))).astype(BF16)
    w_out_b = w_out.astype(BF16)
    w_up_b = w_up.astype(BF16)
    w_down_b = w_down.astype(BF16)
    pad_h = lambda v: jnp.pad(v, (0, LANES - heads)).reshape(1, LANES)
    d_rep = jnp.repeat(ssm_D, SSM_HEAD_DIM, axis=-1)

    outs = []
    for b in range(B):
        xb = x[b]
        delta = None
        for l in range(depth):
            if delta is None:
                h = _rmsnorm(xb, norm_mix_g[l], BF16)
            else:
                xb, h = _add_rmsnorm(xb, delta, norm_mix_g[l], BF16)
            proj = _matmul(h, w_in_main, l)
            dt_raw = _matmul(h, w_in_dt, l, bm_t=2048)

            uh = _dwconv(proj, 0, o1, hy_conv_w[l], hy_conv_b[l], "none")
            hm = _filter_mlp(z2, hy_pos_w1[l], hy_pos_b1[l], hy_pos_w2[l], hy_pos_b2[l],
                             hy_pos_w3[l], hy_pos_b3[l], hy_sin_freq[l])
            g_all = _hy_filters(hm, hy_pos_wout[l], CH, L, consts)
            y0 = _hy_inv(_hy_fwd(uh, 2 * CH, CH, consts), g_all, 0, consts, L)
            z1 = _hy_gate1(uh, 0, 2 * CH, y0, hy_skip[l, 0])
            y1 = _hy_inv(_hy_fwd(z1, 0, CH, consts), g_all, 1, consts, L)
            y_hy = _hy_gate2(uh, CH, z1, y1, hy_skip[l, 1], hy_out_g[l])

            xbc = _dwconv(proj, o2, xbc_w, ssm_conv_w[l], ssm_conv_b[l], "silu")
            yf = _ssd(xbc, dt_raw, pad_h(ssm_dt_bias[l, 0]), pad_h(ssm_A_log[l, 0]),
                      reverse=False, heads=heads, groups=groups)
            yb = _ssd(xbc, dt_raw, pad_h(ssm_dt_bias[l, 1]), pad_h(ssm_A_log[l, 1]),
                      reverse=True, heads=heads, groups=groups)
            y_ssm = _ssm_gate(yf, yb, xbc, proj, o1, d_rep[l], ssm_out_g[l], groups)

            xb = _out_matmul(y_hy, y_ssm, w_out_b, l, xb)
            h2 = _rmsnorm(xb, norm_mlp_g[l], BF16)
            delta = _mlp(h2, w_up_b, w_down_b, l)
        _, y = _add_rmsnorm(xb, delta, final_norm_g, F32)
        outs.append(y)
    return jnp.stack(outs, axis=0)
```

```python
import functools
import math

import numpy as np
import jax
import jax.numpy as jnp
from jax import lax
from jax.experimental import pallas as pl
from jax.experimental.pallas import tpu as pltpu

F32 = jnp.float32
BF16 = jnp.bfloat16

NORM_EPS = 1e-5
SSM_HEAD_DIM = 64
SSM_STATE = 128
SSM_CHUNK = 256
HY_FAST_DECAY = 0.3
HY_SLOW_DECAY = 1.5
HY_DECAY_TARGET = 1e-2

LANES = 128
SUBLANES = 8
DFT_N2 = 128
HY_CB = 128
DFT_UNROLL = 8
FILT_GROUP = 16
VMEM_LIMIT_MB = 56


def _cparams(sem, vmem_mb=VMEM_LIMIT_MB):
    return pltpu.CompilerParams(dimension_semantics=sem, vmem_limit_bytes=vmem_mb * 1024 * 1024)


def _pick(n, target, mult):
    best = None
    for d in range(mult, min(n, target) + 1, mult):
        if n % d == 0:
            best = d
    assert best is not None, (n, target, mult)
    return best


def _dot(a, b):
    return jnp.dot(a, b, preferred_element_type=F32)


def _split3(a):
    hi = a.astype(BF16)
    r1 = a - hi.astype(F32)
    mid = r1.astype(BF16)
    lo = (r1 - mid.astype(F32)).astype(BF16)
    return hi, mid, lo


def _dot_f32(a, b):
    ah = a.astype(BF16)
    al = (a - ah.astype(F32)).astype(BF16)
    bh = b.astype(BF16)
    bl = (b - bh.astype(F32)).astype(BF16)
    return _dot(ah, bh) + (_dot(ah, bl) + _dot(al, bh))


def _silu(x):
    return x / (1.0 + jnp.exp(-x))


def _rmsnorm_body(x_ref, g_ref, h_ref):
    x = x_ref[...]
    ms = jnp.mean(x * x, axis=-1, keepdims=True)
    h_ref[...] = (x * lax.rsqrt(ms + NORM_EPS) * g_ref[...]).astype(h_ref.dtype)


def _rmsnorm(x, g, out_dtype):
    M, D = x.shape
    bm = _pick(M, 256, SUBLANES)
    return pl.pallas_call(
        _rmsnorm_body, grid=(M // bm,),
        in_specs=[pl.BlockSpec((bm, D), lambda i: (i, 0)), pl.BlockSpec((1, D), lambda i: (0, 0))],
        out_specs=pl.BlockSpec((bm, D), lambda i: (i, 0)),
        out_shape=jax.ShapeDtypeStruct((M, D), out_dtype),
        compiler_params=_cparams(("parallel",)), name="rmsnorm",
    )(x, g.reshape(1, D))


def _add_rmsnorm_body(x_ref, d_ref, g_ref, xo_ref, h_ref):
    x = x_ref[...] + d_ref[...]
    xo_ref[...] = x
    ms = jnp.mean(x * x, axis=-1, keepdims=True)
    h_ref[...] = (x * lax.rsqrt(ms + NORM_EPS) * g_ref[...]).astype(h_ref.dtype)


def _add_rmsnorm(x, delta, g, out_dtype):
    M, D = x.shape
    bm = _pick(M, 256, SUBLANES)
    row = pl.BlockSpec((bm, D), lambda i: (i, 0))
    return pl.pallas_call(
        _add_rmsnorm_body, grid=(M // bm,),
        in_specs=[row, row, pl.BlockSpec((1, D), lambda i: (0, 0))],
        out_specs=[row, row],
        out_shape=[jax.ShapeDtypeStruct((M, D), F32), jax.ShapeDtypeStruct((M, D), out_dtype)],
        compiler_params=_cparams(("parallel",)), name="add_rmsnorm",
    )(x, delta, g.reshape(1, D))


def _mm_body(a_ref, b_ref, o_ref):
    o_ref[...] = _dot(a_ref[...], b_ref[...]).astype(o_ref.dtype)


def _matmul(a, b_stack, layer, out_dtype=F32, bm_t=1024, bn_t=1024):
    M, K = a.shape
    N = b_stack.shape[2]
    bm = _pick(M, bm_t, SUBLANES)
    bn = _pick(N, bn_t, LANES)
    return pl.pallas_call(
        _mm_body, grid=(M // bm, N // bn),
        in_specs=[pl.BlockSpec((bm, K), lambda i, j: (i, 0)),
                  pl.BlockSpec((None, K, bn), lambda i, j: (layer, 0, j))],
        out_specs=pl.BlockSpec((bm, bn), lambda i, j: (i, j)),
        out_shape=jax.ShapeDtypeStruct((M, N), out_dtype),
        compiler_params=_cparams(("parallel", "parallel")), name="proj_matmul",
    )(a, b_stack)


def _out_mm_body(a1_ref, a2_ref, b1_ref, b2_ref, r_ref, o_ref):
    acc = _dot(a1_ref[...], b1_ref[...]) + _dot(a2_ref[...], b2_ref[...])
    o_ref[...] = r_ref[...] + acc


def _out_matmul(a1, a2, b_stack, layer, res):
    M, K1 = a1.shape
    assert a2.shape[1] == K1 and b_stack.shape[1] == 2 * K1
    N = b_stack.shape[2]
    bm = _pick(M, 1024, SUBLANES)
    bn = _pick(N, 1024, LANES)
    return pl.pallas_call(
        _out_mm_body, grid=(M // bm, N // bn),
        in_specs=[pl.BlockSpec((bm, K1), lambda i, j: (i, 0)), pl.BlockSpec((bm, K1), lambda i, j: (i, 0)),
                  pl.BlockSpec((None, K1, bn), lambda i, j: (layer, 0, j)),
                  pl.BlockSpec((None, K1, bn), lambda i, j: (layer, 1, j)),
                  pl.BlockSpec((bm, bn), lambda i, j: (i, j))],
        out_specs=pl.BlockSpec((bm, bn), lambda i, j: (i, j)),
        out_shape=jax.ShapeDtypeStruct((M, N), F32),
        compiler_params=_cparams(("parallel", "parallel")), name="out_matmul",
    )(a1, a2, b_stack, b_stack, res)


def _mlp_body(h_ref, wu_ref, wd_ref, o_ref, *, cw):
    j = pl.program_id(1)

    @pl.when(j == 0)
    def _():
        o_ref[...] = jnp.zeros_like(o_ref)

    hw = wu_ref.shape[1] // 2
    ts = []
    for q in range(2):
        t = _dot(h_ref[...], wu_ref[:, q * hw:(q + 1) * hw])
        ts.append(jnp.square(jnp.maximum(t, 0.0)).astype(BF16))
    for c in range(o_ref.shape[1] // cw):
        cols = slice(c * cw, (c + 1) * cw)
        o_ref[:, cols] += _dot(ts[0], wd_ref[0:hw, cols]) + _dot(ts[1], wd_ref[hw:2 * hw, cols])


def _mlp(h, w_up_stack, w_down_stack, layer):
    M, D = h.shape
    F = w_up_stack.shape[2]
    bm = _pick(M, 512, SUBLANES)
    bf = _pick(F, 512, LANES)
    cw = _pick(D, 512, LANES)
    return pl.pallas_call(
        functools.partial(_mlp_body, cw=cw), grid=(M // bm, F // bf),
        in_specs=[pl.BlockSpec((bm, D), lambda i, j: (i, 0)),
                  pl.BlockSpec((None, D, bf), lambda i, j: (layer, 0, j)),
                  pl.BlockSpec((None, bf, D), lambda i, j: (layer, j, 0))],
        out_specs=pl.BlockSpec((bm, D), lambda i, j: (i, 0)),
        out_shape=jax.ShapeDtypeStruct((M, D), F32),
        compiler_params=_cparams(("parallel", "arbitrary")), name="mlp",
    )(h, w_up_stack, w_down_stack)


def _dwconv_body(p_ref, pv_ref, nx_ref, w_ref, b_ref, o_ref, *, act):
    i = pl.program_id(0)
    last = pl.num_programs(0) - 1
    u = p_ref[...]
    bm = u.shape[0]
    prev_row = jnp.where(i > 0, pv_ref[SUBLANES - 1:SUBLANES, :], 0.0)
    next_row = jnp.where(i < last, nx_ref[0:1, :], 0.0)
    rows = lax.broadcasted_iota(jnp.int32, u.shape, 0)
    um = jnp.where(rows == 0, prev_row, pltpu.roll(u, 1, 0))
    up = jnp.where(rows == bm - 1, next_row, pltpu.roll(u, bm - 1, 0))
    y = w_ref[0:1, :] * um
    y = y + w_ref[1:2, :] * u
    y = y + w_ref[2:3, :] * up
    y = y + b_ref[...]
    if act == "silu":
        y = _silu(y)
    if len(o_ref.shape) == 3:
        for k in range(o_ref.shape[0]):
            o_ref[k] = y[:, k * LANES:(k + 1) * LANES]
    else:
        o_ref[...] = y


def _dwconv(p, col_off, C, w, b, act, blocked_out=False):
    L = p.shape[0]
    bm = _pick(L, 1024, SUBLANES)
    bc = _pick(math.gcd(C, col_off) if col_off else C, 1024, LANES)
    co = col_off // bc
    hb = bm // SUBLANES
    nrb = L // SUBLANES
    if blocked_out:
        spb = bc // LANES
        out_spec = pl.BlockSpec((spb, bm, LANES), lambda i, j: (j, i, 0))
        out_shape = jax.ShapeDtypeStruct((C // LANES, L, LANES), F32)
    else:
        out_spec = pl.BlockSpec((bm, bc), lambda i, j: (i, j))
        out_shape = jax.ShapeDtypeStruct((L, C), F32)
    return pl.pallas_call(
        functools.partial(_dwconv_body, act=act), grid=(L // bm, C // bc),
        in_specs=[pl.BlockSpec((bm, bc), lambda i, j: (i, co + j)),
                  pl.BlockSpec((SUBLANES, bc), lambda i, j: (jnp.maximum(i * hb - 1, 0), co + j)),
                  pl.BlockSpec((SUBLANES, bc), lambda i, j: (jnp.minimum((i + 1) * hb, nrb - 1), co + j)),
                  pl.BlockSpec((3, bc), lambda i, j: (0, j)),
                  pl.BlockSpec((1, bc), lambda i, j: (0, j))],
        out_specs=out_spec, out_shape=out_shape,
        compiler_params=_cparams(("parallel", "parallel")), name="dwconv_" + act,
    )(p, p, p, w, b.reshape(1, C))


def _expand_heads(v, ex_ref):
    hi = v.astype(BF16)
    lo = (v - hi.astype(F32)).astype(BF16)
    return _dot(hi, ex_ref[...]) + _dot(lo, ex_ref[...])


def _ssd_body(xbc_ref, dt_ref, bias_ref, alog_ref, ex_ref, y_ref, s_ref, *, reverse, heads, groups):
    Q = SSM_CHUNK
    P = SSM_HEAD_DIM
    N = SSM_STATE
    R = heads // groups
    W = heads * P
    step = pl.program_id(0)

    @pl.when(step == 0)
    def _():
        s_ref[...] = jnp.zeros_like(s_ref)

    x = dt_ref[...] + bias_ref[...]
    dt = jnp.maximum(x, 0.0) + jnp.log1p(jnp.exp(-jnp.abs(x)))
    a = dt * (-jnp.exp(alog_ref[...]))
    ri = lax.broadcasted_iota(jnp.int32, (Q, Q), 0)
    ci = lax.broadcasted_iota(jnp.int32, (Q, Q), 1)
    mask = (ci >= ri) if reverse else (ri >= ci)
    tri = jnp.where(mask, 1.0, 0.0).astype(BF16)
    ah, am, al = _split3(a)
    acs = _dot(tri, ah) + (_dot(tri, am) + _dot(tri, al))
    tot = acs[0:1, :] if reverse else acs[Q - 1:Q, :]
    e_all = _expand_heads(jnp.exp(acs), ex_ref)
    w_all = _expand_heads(jnp.exp(tot - acs) * dt, ex_ref)
    t_all = _expand_heads(jnp.broadcast_to(jnp.exp(tot), (SUBLANES, LANES)), ex_ref)[0:1, :]
    acs_row = acs.T
    dt_row = dt.T

    load_b = lambda g: xbc_ref[:, W + g * N:W + (g + 1) * N].astype(BF16)
    load_c = lambda g: xbc_ref[:, W + groups * N + g * N:W + groups * N + (g + 1) * N].astype(BF16)
    load_x = lambda g: xbc_ref[:, g * R * P:(g + 1) * R * P]
    lane = lax.broadcasted_iota(jnp.int32, (Q, R * P), 1)

    cbs, y_offs = [], []
    for g in range(groups):
        cg = load_c(g)
        cbs.append(lax.dot_general(cg, load_b(g), (((1,), (1,)), ((), ())), preferred_element_type=F32))
        y_offs.append(_dot(cg, s_ref[g].astype(BF16)))
    for g in range(groups):
        gl = slice(g * R * P, (g + 1) * R * P)
        s_new = lax.dot_general(load_b(g), (load_x(g) * w_all[:, gl]).astype(BF16), (((0,), (0,)), ((), ())),
                                preferred_element_type=F32)
        s_ref[g] = s_ref[g] * t_all[:, gl] + s_new
    for g in range(groups):
        xg = load_x(g)
        ws, xs = [], []
        for r in range(R):
            h = g * R + r
            diff = acs[:, h:h + 1] - acs_row[h:h + 1, :]
            lm = jnp.exp(jnp.where(mask, diff, -jnp.inf))
            ws.append((cbs[g] * lm * dt_row[h:h + 1, :]).astype(BF16))
            xs.append(jnp.where((lane >= r * P) & (lane < (r + 1) * P), xg, 0.0).astype(BF16))
        y_diag = _dot(jnp.concatenate(ws, axis=1), jnp.concatenate(xs, axis=0))
        gl = slice(g * R * P, (g + 1) * R * P)
        y_ref[:, gl] = y_diag + y_offs[g] * e_all[:, gl]


def _ssd(xbc, dt_raw, dt_bias, a_log, *, reverse, heads, groups):
    L = xbc.shape[0]
    Q = SSM_CHUNK
    assert L % Q == 0
    nc = L // Q
    W = heads * SSM_HEAD_DIM
    R = heads // groups
    cmap = (lambda i: (nc - 1 - i, 0)) if reverse else (lambda i: (i, 0))
    ex = np.zeros((LANES, W), np.float32)
    ex[np.arange(W) // SSM_HEAD_DIM, np.arange(W)] = 1.0
    return pl.pallas_call(
        functools.partial(_ssd_body, reverse=reverse, heads=heads, groups=groups), grid=(nc,),
        in_specs=[pl.BlockSpec((Q, xbc.shape[1]), cmap), pl.BlockSpec((Q, LANES), cmap),
                  pl.BlockSpec((1, LANES), lambda i: (0, 0)), pl.BlockSpec((1, LANES), lambda i: (0, 0)),
                  pl.BlockSpec((LANES, W), lambda i: (0, 0))],
        out_specs=pl.BlockSpec((Q, W), cmap),
        out_shape=jax.ShapeDtypeStruct((L, W), F32),
        scratch_shapes=[pltpu.VMEM((groups, SSM_STATE, R * SSM_HEAD_DIM), F32)],
        compiler_params=_cparams(("arbitrary",)), name="ssd_bwd" if reverse else "ssd_fwd",
    )(xbc, dt_raw, dt_bias, a_log, jnp.asarray(ex.astype(BF16)))


def _ssm_gate_body(yf_ref, yb_ref, xh_ref, z_ref, d_ref, g_ref, o_ref, *, groups):
    y = (yf_ref[...] + yb_ref[...] + d_ref[...] * xh_ref[...]) * _silu(z_ref[...])
    W = y.shape[1]
    gw = W // groups
    for g in range(groups):
        yg = y[:, g * gw:(g + 1) * gw]
        ms = jnp.mean(yg * yg, axis=-1, keepdims=True)
        o_ref[:, g * gw:(g + 1) * gw] = (yg * lax.rsqrt(ms + NORM_EPS) * g_ref[:, g * gw:(g + 1) * gw]).astype(o_ref.dtype)


def _ssm_gate(yf, yb, xbc, proj, z_off, d_rep, norm_g, groups):
    L, W = yf.shape
    bm = _pick(L, 256, SUBLANES)
    assert z_off % W == 0
    zo = z_off // W
    row = pl.BlockSpec((bm, W), lambda i: (i, 0))
    vec = pl.BlockSpec((1, W), lambda i: (0, 0))
    return pl.pallas_call(
        functools.partial(_ssm_gate_body, groups=groups), grid=(L // bm,),
        in_specs=[row, row, row, pl.BlockSpec((bm, W), lambda i: (i, zo)), vec, vec],
        out_specs=row, out_shape=jax.ShapeDtypeStruct((L, W), BF16),
        compiler_params=_cparams(("parallel",)), name="ssm_gate",
    )(yf, yb, xbc, proj, d_rep.reshape(1, W), norm_g.reshape(1, W))


def _dft_consts(L):
    N2 = DFT_N2
    N = 2 * L
    H1 = L // N2
    n2 = np.arange(N2)[:, None, None]
    k1 = np.arange(H1)[None, :, None]
    n1 = np.arange(H1)[None, None, :]
    m = ((2 * k1 + 1) * (N2 * n1 + n2)) % (2 * N)
    ph = np.pi * m.astype(np.float64) / N
    m_fwd = np.concatenate([np.cos(ph), -np.sin(ph)], axis=1)
    m_inv = (2.0 / N) * np.concatenate([np.cos(ph), -np.sin(ph)], axis=1).transpose(0, 2, 1)
    kk = (np.arange(N2)[:, None] * np.arange(N2)[None, :]) % N2
    c2 = np.cos(2 * np.pi * kk / N2)
    s2 = np.sin(2 * np.pi * kk / N2)
    f2_fwd = np.block([[c2, s2], [-s2, c2]])
    f2_inv = np.block([[c2, -s2], [s2, c2]])
    as_bf = lambda a: jnp.asarray(a.astype(BF16))
    return as_bf(m_fwd), as_bf(m_inv), as_bf(f2_fwd), as_bf(f2_inv)


def _pitch(rows):
    p = rows + SUBLANES
    assert (p // SUBLANES) % 2 == 1
    return p


def _stage3(a_s, f2_ref, k1, H1, pitch):
    N2 = DFT_N2
    ar = a_s[pl.ds(k1, N2, stride=pitch), :]
    ai = a_s[pl.ds(H1 + k1, N2, stride=pitch), :]
    av = jnp.concatenate([ar, ai], axis=0).astype(BF16)
    return _dot(f2_ref[...], av)


def _fwd_body(u_ref, m_ref, f2_ref, x_ref, a_s, *, H1, ks):
    N2 = DFT_N2
    pitch = _pitch(2 * H1)
    s = pl.program_id(1)

    @pl.when(s == 0)
    def _():
        def body(n2, carry):
            u = u_ref[pl.ds(n2, H1, stride=N2), :]
            a_s[pl.ds(pl.multiple_of(n2 * pitch, SUBLANES), 2 * H1), :] = _dot(m_ref[n2], u.astype(BF16))
            return carry
        lax.fori_loop(0, N2, body, 0, unroll=DFT_UNROLL)

    def kbody(kk, carry):
        x = _stage3(a_s, f2_ref, s * ks + kk, H1, pitch)
        x_ref[kk, 0] = x[:N2]
        x_ref[kk, 1] = x[N2:]
        return carry
    lax.fori_loop(0, ks, kbody, 0, unroll=DFT_UNROLL)


def _hy_fwd(u_blk, slab_off, nj, consts):
    L = u_blk.shape[1]
    N2, cb = DFT_N2, HY_CB
    H1 = L // N2
    ks = _pick(H1, 16, 1)
    m_fwd, _, f2_fwd, _ = consts
    return pl.pallas_call(
        functools.partial(_fwd_body, H1=H1, ks=ks), grid=(nj, H1 // ks),
        in_specs=[pl.BlockSpec((None, L, cb), lambda j, s: (slab_off + j, 0, 0)),
                  pl.BlockSpec((N2, 2 * H1, H1), lambda j, s: (0, 0, 0)),
                  pl.BlockSpec((2 * N2, 2 * N2), lambda j, s: (0, 0))],
        out_specs=pl.BlockSpec((None, ks, 2, N2, cb), lambda j, s: (j, s, 0, 0, 0)),
        out_shape=jax.ShapeDtypeStruct((nj, H1, 2, N2, cb), F32),
        scratch_shapes=[pltpu.VMEM((N2 * _pitch(2 * H1), cb), F32)],
        compiler_params=_cparams(("parallel", "arbitrary")), name="hy_fwd",
    )(u_blk, m_fwd, f2_fwd)


def _inv_body(x_ref, g_ref, m_ref, f2_ref, y_ref, b_s, *, H1, ks):
    N2 = DFT_N2
    pitch = _pitch(2 * N2)
    s = pl.program_id(1)

    def kbody(kk, carry):
        xr, xi = x_ref[kk, 0], x_ref[kk, 1]
        gr, gi = g_ref[kk, 0], g_ref[kk, 1]
        yr = xr * gr - xi * gi
        yi = xr * gi + xi * gr
        b = _dot(f2_ref[...], jnp.concatenate([yr, yi], axis=0).astype(BF16))
        b_s[pl.ds(pl.multiple_of((s * ks + kk) * pitch, SUBLANES), 2 * N2), :] = b
        return carry
    lax.fori_loop(0, ks, kbody, 0, unroll=DFT_UNROLL)

    @pl.when(s == pl.num_programs(1) - 1)
    def _():
        def body(n2, carry):
            br = b_s[pl.ds(n2, H1, stride=pitch), :]
            bi = b_s[pl.ds(N2 + n2, H1, stride=pitch), :]
            y = _dot(m_ref[n2], jnp.concatenate([br, bi], axis=0).astype(BF16))
            y_ref[pl.ds(n2, H1, stride=N2), :] = y
            return carry
        lax.fori_loop(0, N2, body, 0, unroll=DFT_UNROLL)


def _hy_inv(x, g_all, order, consts, L):
    nj, H1, _, N2, cb = x.shape
    ks = _pick(H1, 16, 1)
    _, m_inv, _, f2_inv = consts
    return pl.pallas_call(
        functools.partial(_inv_body, H1=H1, ks=ks), grid=(nj, H1 // ks),
        in_specs=[pl.BlockSpec((None, ks, 2, N2, cb), lambda j, s: (j, s, 0, 0, 0)),
                  pl.BlockSpec((None, None, ks, 2, N2, cb), lambda j, s: (order, j, s, 0, 0, 0)),
                  pl.BlockSpec((N2, H1, 2 * H1), lambda j, s: (0, 0, 0)),
                  pl.BlockSpec((2 * N2, 2 * N2), lambda j, s: (0, 0))],
        out_specs=pl.BlockSpec((None, L, cb), lambda j, s: (j, 0, 0)),
        out_shape=jax.ShapeDtypeStruct((nj, L, cb), F32),
        scratch_shapes=[pltpu.VMEM((H1 * _pitch(2 * N2), cb), F32)],
        compiler_params=_cparams(("parallel", "arbitrary")), name="hy_inv",
    )(x, g_all, m_inv, f2_inv)


def _pos_features(L, emb):
    N2 = DFT_N2
    H1 = L // N2
    bands = (emb - 1) // 2
    r = jnp.arange(L)
    lag = (r % H1) * N2 + (r // H1)
    f = jnp.linspace(1e-4, bands - 1, bands, dtype=F32)[None, :]

    def feats(lg):
        lgf = lg.astype(F32)[:, None]
        t = lgf / (L - 1)
        w = 2.0 * math.pi * lgf / L
        z = jnp.concatenate([t, jnp.cos(f * w), -jnp.sin(f * w)], axis=-1)
        return jnp.pad(z, ((0, 0), (0, LANES - emb)))
    return jnp.concatenate([feats(lag), feats(jnp.where(lag == 0, 0, L - lag))], axis=-1)


def _filter_mlp_body(z_ref, w1_ref, b1_ref, w2_ref, b2_ref, w3_ref, b3_ref, fr_ref, o_ref):
    h = jnp.sin(fr_ref[0:1, :] * (_dot_f32(z_ref[...], w1_ref[...]) + b1_ref[...]))
    h = jnp.sin(fr_ref[1:2, :] * (_dot_f32(h, w2_ref[...]) + b2_ref[...]))
    h = jnp.sin(fr_ref[2:3, :] * (_dot_f32(h, w3_ref[...]) + b3_ref[...]))
    o_ref[...] = h.astype(o_ref.dtype)


def _bdiag(w):
    a, b = w.shape
    z = jnp.zeros((a, b), w.dtype)
    return jnp.concatenate([jnp.concatenate([w, z], axis=1), jnp.concatenate([z, w], axis=1)], axis=0)


def _filter_mlp(z2, w1, b1, w2, b2, w3, b3, fr):
    L = z2.shape[0]
    emb, hd = w1.shape
    assert 2 * hd == LANES
    w1p = jnp.zeros((2 * LANES, LANES), F32).at[:emb, :hd].set(w1).at[LANES:LANES + emb, hd:].set(w1)
    tile2 = lambda v: jnp.concatenate([v, v], axis=-1)
    bm = _pick(L, 1024, SUBLANES)
    full = lambda a: pl.BlockSpec(a.shape, lambda i: (0,) * a.ndim)
    args = (w1p, tile2(b1).reshape(1, LANES), _bdiag(w2), tile2(b2).reshape(1, LANES),
            _bdiag(w3), tile2(b3).reshape(1, LANES), tile2(fr))
    return pl.pallas_call(
        _filter_mlp_body, grid=(L // bm,),
        in_specs=[pl.BlockSpec((bm, 2 * LANES), lambda i: (i, 0))] + [full(a) for a in args],
        out_specs=pl.BlockSpec((bm, LANES), lambda i: (i, 0)),
        out_shape=jax.ShapeDtypeStruct((L, LANES), BF16),
        compiler_params=_cparams(("parallel",)), name="hy_filter_mlp",
    )(z2, *args)


def _filt_body(hm_ref, w2_ref, ad_ref, m_ref, f2_ref, g_ref, a_s, g_s, inv_s, *, L, H1, ks):
    N2, cb = DFT_N2, HY_CB
    pitch = _pitch(2 * H1)
    s = pl.program_id(2)

    @pl.when(s == 0)
    def _():
        rate = ad_ref[...] * (1.0 / (L - 1))
        n1 = lax.broadcasted_iota(jnp.int32, (H1, cb), 0)
        sgn = jnp.where(n1 % 2 == 0, 1.0, -1.0)
        e1f = jnp.exp(-(n1 * N2).astype(F32) * rate)
        e1b = jnp.exp(-((H1 - 1 - n1) * N2).astype(F32) * rate)

        def group(gi, nrm):
            def gen(t, nrm):
                n2 = gi * FILT_GROUP + t
                hs = hm_ref[pl.ds(pl.multiple_of(n2 * H1, SUBLANES), H1), :]
                gp = _dot(hs, w2_ref[...])
                n2f = n2.astype(F32)
                gf = gp[:, :cb] * (e1f * jnp.exp(-n2f * rate))
                gb = gp[:, cb:] * (e1b * jnp.exp(-(N2 - n2f) * rate))
                gb = jnp.where(n1 + n2 == 0, 0.0, -gb)
                g_s[t, 0] = gf.astype(BF16)
                g_s[t, 1] = gb.astype(BF16)
                return nrm + (jnp.abs(gf) + jnp.abs(gb))
            nrm = lax.fori_loop(0, FILT_GROUP, gen, nrm, unroll=DFT_UNROLL)

            def xform(t, carry):
                n2 = gi * FILT_GROUP + t
                m = m_ref[n2]
                a1 = _dot(m, g_s[t, 0])
                a2 = _dot(m, g_s[t, 1])
                base = pl.multiple_of(n2 * pitch, SUBLANES)
                a_s[pl.ds(base, H1), :] = a1[:H1] + sgn * a2[H1:]
                a_s[pl.ds(base + H1, H1), :] = a1[H1:] - sgn * a2[:H1]
                return carry
            lax.fori_loop(0, FILT_GROUP, xform, 0, unroll=DFT_UNROLL)
            return nrm
        nrm = lax.fori_loop(0, N2 // FILT_GROUP, group, jnp.zeros((H1, cb), F32))
        inv_s[...] = jnp.broadcast_to(1.0 / jnp.sum(nrm, axis=0, keepdims=True), inv_s.shape)

    scale = inv_s[0:1, :]

    def kbody(kk, carry):
        x = _stage3(a_s, f2_ref, s * ks + kk, H1, pitch) * scale
        g_ref[kk, 0] = x[:N2]
        g_ref[kk, 1] = x[N2:]
        return carry
    lax.fori_loop(0, ks, kbody, 0, unroll=DFT_UNROLL)


def _hy_filters(hm, wout, C, L, consts):
    N2, cb = DFT_N2, HY_CB
    H1 = L // N2
    ks = _pick(H1, 16, 1)
    hd = wout.shape[0]
    nj = C // cb
    m_fwd, _, f2_fwd, _ = consts
    w4 = wout.reshape(hd, 2, 2, nj, cb)
    wf = jnp.transpose(w4[:, :, 0], (1, 2, 0, 3))
    wb = jnp.transpose(w4[:, :, 1], (1, 2, 0, 3))
    z = jnp.zeros_like(wf)
    w2 = jnp.concatenate([jnp.concatenate([wf, z], axis=-1), jnp.concatenate([z, wb], axis=-1)], axis=2)
    w2 = w2.astype(BF16)
    deltas = jnp.linspace(math.log(HY_FAST_DECAY) / HY_DECAY_TARGET,
                          math.log(HY_SLOW_DECAY) / HY_DECAY_TARGET, C, dtype=F32)
    absd = jnp.abs(deltas).reshape(1, C)
    return pl.pallas_call(
        functools.partial(_filt_body, L=L, H1=H1, ks=ks), grid=(2, nj, H1 // ks),
        in_specs=[pl.BlockSpec((L, LANES), lambda o, j, s: (0, 0)),
                  pl.BlockSpec((None, None, 2 * hd, 2 * cb), lambda o, j, s: (o, j, 0, 0)),
                  pl.BlockSpec((1, cb), lambda o, j, s: (0, j)),
                  pl.BlockSpec((N2, 2 * H1, H1), lambda o, j, s: (0, 0, 0)),
                  pl.BlockSpec((2 * N2, 2 * N2), lambda o, j, s: (0, 0))],
        out_specs=pl.BlockSpec((None, None, ks, 2, N2, cb), lambda o, j, s: (o, j, s, 0, 0, 0)),
        out_shape=jax.ShapeDtypeStruct((2, nj, H1, 2, N2, cb), F32),
        scratch_shapes=[pltpu.VMEM((N2 * _pitch(2 * H1), cb), F32), pltpu.VMEM((FILT_GROUP, 2, H1, cb), BF16),
                        pltpu.VMEM((SUBLANES, cb), F32)],
        compiler_params=_cparams(("parallel", "parallel", "arbitrary")), name="hy_filters",
    )(hm, w2, absd, m_fwd, f2_fwd)


def _hy_gate1_body(gate_ref, v_ref, y_ref, sk_ref, o_ref):
    o_ref[...] = gate_ref[...] * (y_ref[...] + sk_ref[...] * v_ref[...])


def _hy_gate1(uh_blk, gate_blk, v_blk, y_blk, sk):
    nj, L, cb = y_blk.shape
    bm = _pick(L, 256, SUBLANES)
    blk = lambda g: pl.BlockSpec((nj, bm, cb), lambda i: (g, i, 0))
    return pl.pallas_call(
        _hy_gate1_body, grid=(L // bm,),
        in_specs=[blk(gate_blk), blk(v_blk), blk(0), pl.BlockSpec((nj, 1, cb), lambda i: (0, 0, 0))],
        out_specs=blk(0), out_shape=jax.ShapeDtypeStruct((nj, L, cb), F32),
        compiler_params=_cparams(("parallel",)), name="hy_gate1",
    )(uh_blk, uh_blk, y_blk, sk.reshape(nj, 1, cb))


def _hy_gate2_body(gate_ref, v_ref, y_ref, sk_ref, g_ref, o_ref):
    z = gate_ref[...] * (y_ref[...] + sk_ref[...] * v_ref[...])
    nj, _, cb = z.shape
    ms = jnp.sum(jnp.sum(z * z, axis=0), axis=-1, keepdims=True) * (1.0 / (nj * cb))
    r = lax.rsqrt(ms + NORM_EPS)
    for k in range(nj):
        o_ref[:, k * cb:(k + 1) * cb] = (z[k] * r * g_ref[k]).astype(o_ref.dtype)


def _hy_gate2(uh_blk, gate_blk, v_blk, y_blk, sk, g):
    nj, L, cb = y_blk.shape
    bm = _pick(L, 256, SUBLANES)
    blk = lambda gidx: pl.BlockSpec((nj, bm, cb), lambda i: (gidx, i, 0))
    vec = pl.BlockSpec((nj, 1, cb), lambda i: (0, 0, 0))
    return pl.pallas_call(
        _hy_gate2_body, grid=(L // bm,),
        in_specs=[blk(gate_blk), blk(0), blk(0), vec, vec],
        out_specs=pl.BlockSpec((bm, nj * cb), lambda i: (i, 0)),
        out_shape=jax.ShapeDtypeStruct((L, nj * cb), BF16),
        compiler_params=_cparams(("parallel",)), name="hy_gate2",
    )(uh_blk, v_blk, y_blk, sk.reshape(nj, 1, cb), g.reshape(nj, 1, cb))


def kernel(x, norm_mix_g, w_in, hy_conv_w, hy_conv_b, hy_pos_w1, hy_pos_b1, hy_pos_w2, hy_pos_b2, hy_pos_w3, hy_pos_b3, hy_sin_freq, hy_pos_wout, hy_skip, hy_out_g, ssm_conv_w, ssm_conv_b, ssm_A_log, ssm_dt_bias, ssm_D, ssm_out_g, w_out, norm_mlp_g, w_up, w_down, final_norm_g):
    B, L, D = x.shape
    depth = w_in.shape[0]
    CH = hy_out_g.shape[1]
    WS = ssm_out_g.shape[1]
    heads = ssm_D.shape[1]
    xbc_w = ssm_conv_w.shape[2]
    groups = (xbc_w - WS) // (2 * SSM_STATE)
    o1 = 3 * CH
    o2 = o1 + WS
    o3 = o2 + xbc_w
    assert WS == heads * SSM_HEAD_DIM and heads <= LANES and L % DFT_N2 == 0 and CH % HY_CB == 0

    consts = _dft_consts(L)
    z2 = _pos_features(L, hy_pos_w1.shape[1])
    w_in_main = w_in[:, :, :o3].astype(BF16)
    w_in_dt = jnp.pad(w_in[:, :, o3:], ((0, 0), (0, 0), (0, LANES - heads))).astype(BF16)
    w_out_b = w_out.astype(BF16)
    w_up_b = w_up.astype(BF16)
    w_down_b = w_down.astype(BF16)
    pad_h = lambda v: jnp.pad(v, (0, LANES - heads)).reshape(1, LANES)
    d_rep = jnp.repeat(ssm_D, SSM_HEAD_DIM, axis=-1)

    outs = []
    for b in range(B):
        xb = x[b]
        delta = None
        for l in range(depth):
            if delta is None:
                h = _rmsnorm(xb, norm_mix_g[l], BF16)
            else:
                xb, h = _add_rmsnorm(xb, delta, norm_mix_g[l], BF16)
            proj = _matmul(h, w_in_main, l)
            dt_raw = _matmul(h, w_in_dt, l, bm_t=2048)

            nj = CH // HY_CB
            uh = _dwconv(proj, 0, o1, hy_conv_w[l], hy_conv_b[l], "none", blocked_out=True)
            hm = _filter_mlp(z2, hy_pos_w1[l], hy_pos_b1[l], hy_pos_w2[l], hy_pos_b2[l],
                             hy_pos_w3[l], hy_pos_b3[l], hy_sin_freq[l])
            g_all = _hy_filters(hm, hy_pos_wout[l], CH, L, consts)
            y0 = _hy_inv(_hy_fwd(uh, 2 * nj, nj, consts), g_all, 0, consts, L)
            z1 = _hy_gate1(uh, 0, 2, y0, hy_skip[l, 0])
            y1 = _hy_inv(_hy_fwd(z1, 0, nj, consts), g_all, 1, consts, L)
            y_hy = _hy_gate2(uh, 1, z1, y1, hy_skip[l, 1], hy_out_g[l])

            xbc = _dwconv(proj, o2, xbc_w, ssm_conv_w[l], ssm_conv_b[l], "silu")
            yf = _ssd(xbc, dt_raw, pad_h(ssm_dt_bias[l, 0]), pad_h(ssm_A_log[l, 0]),
                      reverse=False, heads=heads, groups=groups)
            yb = _ssd(xbc, dt_raw, pad_h(ssm_dt_bias[l, 1]), pad_h(ssm_A_log[l, 1]),
                      reverse=True, heads=heads, groups=groups)
            y_ssm = _ssm_gate(yf, yb, xbc, proj, o1, d_rep[l], ssm_out_g[l], groups)

            xb = _out_matmul(y_hy, y_ssm, w_out_b, l, xb)
            h2 = _rmsnorm(xb, norm_mlp_g[l], BF16)
            delta = _mlp(h2, w_up_b, w_down_b, l)
        _, y = _add_rmsnorm(xb, delta, final_norm_g, F32)
        outs.append(y)
    return jnp.stack(outs, axis=0)
```

```python
import functools
import math

import numpy as np
import jax
import jax.numpy as jnp
from jax import lax
from jax.experimental import pallas as pl
from jax.experimental.pallas import tpu as pltpu

F32 = jnp.float32
BF16 = jnp.bfloat16

NORM_EPS = 1e-5
SSM_HEAD_DIM = 64
SSM_STATE = 128
SSM_CHUNK = 256
HY_FAST_DECAY = 0.3
HY_SLOW_DECAY = 1.5
HY_DECAY_TARGET = 1e-2

LANES = 128
SUBLANES = 8
DFT_N2 = 128
HY_CB = 128
DFT_UNROLL = 16
DFT_GROUP = 32
FILT_GROUP = 16
SPEC_DTYPE = BF16
VMEM_LIMIT_MB = 56


def _cparams(sem, vmem_mb=VMEM_LIMIT_MB):
    return pltpu.CompilerParams(dimension_semantics=sem, vmem_limit_bytes=vmem_mb * 1024 * 1024)


def _pick(n, target, mult):
    best = None
    for d in range(mult, min(n, target) + 1, mult):
        if n % d == 0:
            best = d
    assert best is not None, (n, target, mult)
    return best


def _dot(a, b):
    return jnp.dot(a, b, preferred_element_type=F32)


def _split3(a):
    hi = a.astype(BF16)
    r1 = a - hi.astype(F32)
    mid = r1.astype(BF16)
    lo = (r1 - mid.astype(F32)).astype(BF16)
    return hi, mid, lo


def _dot_f32(a, b):
    ah = a.astype(BF16)
    al = (a - ah.astype(F32)).astype(BF16)
    bh = b.astype(BF16)
    bl = (b - bh.astype(F32)).astype(BF16)
    return _dot(ah, bh) + (_dot(ah, bl) + _dot(al, bh))


def _silu(x):
    return x / (1.0 + jnp.exp(-x))


def _rmsnorm_body(x_ref, g_ref, h_ref):
    x = x_ref[...]
    ms = jnp.mean(x * x, axis=-1, keepdims=True)
    h_ref[...] = (x * lax.rsqrt(ms + NORM_EPS) * g_ref[...]).astype(h_ref.dtype)


def _rmsnorm(x, g, out_dtype):
    M, D = x.shape
    bm = _pick(M, 256, SUBLANES)
    return pl.pallas_call(
        _rmsnorm_body, grid=(M // bm,),
        in_specs=[pl.BlockSpec((bm, D), lambda i: (i, 0)), pl.BlockSpec((1, D), lambda i: (0, 0))],
        out_specs=pl.BlockSpec((bm, D), lambda i: (i, 0)),
        out_shape=jax.ShapeDtypeStruct((M, D), out_dtype),
        compiler_params=_cparams(("parallel",)), name="rmsnorm",
    )(x, g.reshape(1, D))


def _add_rmsnorm_body(x_ref, d_ref, g_ref, *out_refs):
    x = x_ref[...] + d_ref[...]
    if len(out_refs) == 2:
        out_refs[0][...] = x
    h_ref = out_refs[-1]
    ms = jnp.mean(x * x, axis=-1, keepdims=True)
    h_ref[...] = (x * lax.rsqrt(ms + NORM_EPS) * g_ref[...]).astype(h_ref.dtype)


def _add_rmsnorm(x, delta, g, out_dtype, keep_sum=True):
    M, D = x.shape
    bm = _pick(M, 256, SUBLANES)
    row = pl.BlockSpec((bm, D), lambda i: (i, 0))
    h_shape = jax.ShapeDtypeStruct((M, D), out_dtype)
    return pl.pallas_call(
        _add_rmsnorm_body, grid=(M // bm,),
        in_specs=[row, row, pl.BlockSpec((1, D), lambda i: (0, 0))],
        out_specs=[row, row] if keep_sum else row,
        out_shape=[jax.ShapeDtypeStruct((M, D), F32), h_shape] if keep_sum else h_shape,
        compiler_params=_cparams(("parallel",)), name="add_rmsnorm",
    )(x, delta, g.reshape(1, D))


def _mm_body(a_ref, b_ref, o_ref):
    o_ref[...] = _dot(a_ref[...], b_ref[...]).astype(o_ref.dtype)


def _matmul(a, b_stack, layer, n_cols=None, out_dtype=F32, bm_t=1024, bn_t=1024):
    M, K = a.shape
    N = b_stack.shape[2] if n_cols is None else n_cols
    bm = _pick(M, bm_t, SUBLANES)
    bn = _pick(N, bn_t, LANES)
    return pl.pallas_call(
        _mm_body, grid=(M // bm, N // bn),
        in_specs=[pl.BlockSpec((bm, K), lambda i, j: (i, 0)),
                  pl.BlockSpec((None, K, bn), lambda i, j: (layer, 0, j))],
        out_specs=pl.BlockSpec((bm, bn), lambda i, j: (i, j)),
        out_shape=jax.ShapeDtypeStruct((M, N), out_dtype),
        compiler_params=_cparams(("parallel", "parallel")), name="proj_matmul",
    )(a, b_stack)


def _out_mm_body(a1_ref, a2_ref, b1_ref, b2_ref, r_ref, o_ref):
    acc = _dot(a1_ref[...], b1_ref[...]) + _dot(a2_ref[...], b2_ref[...])
    o_ref[...] = r_ref[...] + acc


def _out_matmul(a1, a2, b_stack, layer, res):
    M, K1 = a1.shape
    assert a2.shape[1] == K1 and b_stack.shape[1] == 2 * K1
    N = b_stack.shape[2]
    bm = _pick(M, 1024, SUBLANES)
    bn = _pick(N, 1024, LANES)
    return pl.pallas_call(
        _out_mm_body, grid=(M // bm, N // bn),
        in_specs=[pl.BlockSpec((bm, K1), lambda i, j: (i, 0)), pl.BlockSpec((bm, K1), lambda i, j: (i, 0)),
                  pl.BlockSpec((None, K1, bn), lambda i, j: (layer, 0, j)),
                  pl.BlockSpec((None, K1, bn), lambda i, j: (layer, 1, j)),
                  pl.BlockSpec((bm, bn), lambda i, j: (i, j))],
        out_specs=pl.BlockSpec((bm, bn), lambda i, j: (i, j)),
        out_shape=jax.ShapeDtypeStruct((M, N), F32),
        compiler_params=_cparams(("parallel", "parallel")), name="out_matmul",
    )(a1, a2, b_stack, b_stack, res)


def _mlp_body(h_ref, wu_ref, wd_ref, o_ref, *, cw):
    j = pl.program_id(1)

    @pl.when(j == 0)
    def _():
        o_ref[...] = jnp.zeros_like(o_ref)

    hw = wu_ref.shape[1] // 2
    ts = []
    for q in range(2):
        t = _dot(h_ref[...], wu_ref[:, q * hw:(q + 1) * hw])
        ts.append(jnp.square(jnp.maximum(t, 0.0)).astype(BF16))
    for c in range(o_ref.shape[1] // cw):
        cols = slice(c * cw, (c + 1) * cw)
        o_ref[:, cols] += _dot(ts[0], wd_ref[0:hw, cols]) + _dot(ts[1], wd_ref[hw:2 * hw, cols])


def _mlp(h, w_up_stack, w_down_stack, layer):
    M, D = h.shape
    F = w_up_stack.shape[2]
    bm = _pick(M, 512, SUBLANES)
    bf = _pick(F, 512, LANES)
    cw = _pick(D, 512, LANES)
    return pl.pallas_call(
        functools.partial(_mlp_body, cw=cw), grid=(M // bm, F // bf),
        in_specs=[pl.BlockSpec((bm, D), lambda i, j: (i, 0)),
                  pl.BlockSpec((None, D, bf), lambda i, j: (layer, 0, j)),
                  pl.BlockSpec((None, bf, D), lambda i, j: (layer, j, 0))],
        out_specs=pl.BlockSpec((bm, D), lambda i, j: (i, 0)),
        out_shape=jax.ShapeDtypeStruct((M, D), F32),
        compiler_params=_cparams(("parallel", "arbitrary")), name="mlp",
    )(h, w_up_stack, w_down_stack)


def _dwconv_body(p_ref, pv_ref, nx_ref, w_ref, b_ref, o_ref, *, act):
    i = pl.program_id(0)
    last = pl.num_programs(0) - 1
    u = p_ref[...]
    bm = u.shape[0]
    prev_row = jnp.where(i > 0, pv_ref[SUBLANES - 1:SUBLANES, :], 0.0)
    next_row = jnp.where(i < last, nx_ref[0:1, :], 0.0)
    rows = lax.broadcasted_iota(jnp.int32, u.shape, 0)
    um = jnp.where(rows == 0, prev_row, pltpu.roll(u, 1, 0))
    up = jnp.where(rows == bm - 1, next_row, pltpu.roll(u, bm - 1, 0))
    y = w_ref[0:1, :] * um
    y = y + w_ref[1:2, :] * u
    y = y + w_ref[2:3, :] * up
    y = y + b_ref[...]
    if act == "silu":
        y = _silu(y)
    if len(o_ref.shape) == 3:
        for k in range(o_ref.shape[0]):
            o_ref[k] = y[:, k * LANES:(k + 1) * LANES]
    else:
        o_ref[...] = y


def _dwconv(p, col_off, C, w, b, act, blocked_out=False):
    L = p.shape[0]
    bm = _pick(L, 1024, SUBLANES)
    bc = _pick(math.gcd(C, col_off) if col_off else C, 1024, LANES)
    co = col_off // bc
    hb = bm // SUBLANES
    nrb = L // SUBLANES
    if blocked_out:
        spb = bc // LANES
        out_spec = pl.BlockSpec((spb, bm, LANES), lambda i, j: (j, i, 0))
        out_shape = jax.ShapeDtypeStruct((C // LANES, L, LANES), F32)
    else:
        out_spec = pl.BlockSpec((bm, bc), lambda i, j: (i, j))
        out_shape = jax.ShapeDtypeStruct((L, C), F32)
    return pl.pallas_call(
        functools.partial(_dwconv_body, act=act), grid=(L // bm, C // bc),
        in_specs=[pl.BlockSpec((bm, bc), lambda i, j: (i, co + j)),
                  pl.BlockSpec((SUBLANES, bc), lambda i, j: (jnp.maximum(i * hb - 1, 0), co + j)),
                  pl.BlockSpec((SUBLANES, bc), lambda i, j: (jnp.minimum((i + 1) * hb, nrb - 1), co + j)),
                  pl.BlockSpec((3, bc), lambda i, j: (0, j)),
                  pl.BlockSpec((1, bc), lambda i, j: (0, j))],
        out_specs=out_spec, out_shape=out_shape,
        compiler_params=_cparams(("parallel", "parallel")), name="dwconv_" + act,
    )(p, p, p, w, b.reshape(1, C))


def _expand_heads(v, ex_ref):
    hi = v.astype(BF16)
    lo = (v - hi.astype(F32)).astype(BF16)
    return _dot(hi, ex_ref[...]) + _dot(lo, ex_ref[...])


def _ssd_body(xbc_ref, dt_ref, bias_ref, alog_ref, ex_ref, y_ref, s_ref, *, reverse, heads, groups):
    Q = SSM_CHUNK
    P = SSM_HEAD_DIM
    N = SSM_STATE
    R = heads // groups
    W = heads * P
    step = pl.program_id(0)

    @pl.when(step == 0)
    def _():
        s_ref[...] = jnp.zeros_like(s_ref)

    x = dt_ref[...] + bias_ref[...]
    dt = jnp.maximum(x, 0.0) + jnp.log1p(jnp.exp(-jnp.abs(x)))
    a = dt * (-jnp.exp(alog_ref[...]))
    ri = lax.broadcasted_iota(jnp.int32, (Q, Q), 0)
    ci = lax.broadcasted_iota(jnp.int32, (Q, Q), 1)
    mask = (ci >= ri) if reverse else (ri >= ci)
    tri = jnp.where(mask, 1.0, 0.0).astype(BF16)
    ah, am, al = _split3(a)
    acs = _dot(tri, ah) + (_dot(tri, am) + _dot(tri, al))
    tot = acs[0:1, :] if reverse else acs[Q - 1:Q, :]
    e_all = _expand_heads(jnp.exp(acs), ex_ref)
    w_all = _expand_heads(jnp.exp(tot - acs) * dt, ex_ref)
    t_all = _expand_heads(jnp.broadcast_to(jnp.exp(tot), (SUBLANES, LANES)), ex_ref)[0:1, :]
    acs_row = acs.T
    dt_row = dt.T

    load_b = lambda g: xbc_ref[:, W + g * N:W + (g + 1) * N].astype(BF16)
    load_c = lambda g: xbc_ref[:, W + groups * N + g * N:W + groups * N + (g + 1) * N].astype(BF16)
    load_x = lambda g: xbc_ref[:, g * R * P:(g + 1) * R * P]
    lane = lax.broadcasted_iota(jnp.int32, (Q, R * P), 1)

    cbs, y_offs = [], []
    for g in range(groups):
        cg = load_c(g)
        cbs.append(lax.dot_general(cg, load_b(g), (((1,), (1,)), ((), ())), preferred_element_type=F32))
        y_offs.append(_dot(cg, s_ref[g].astype(BF16)))
    for g in range(groups):
        gl = slice(g * R * P, (g + 1) * R * P)
        s_new = lax.dot_general(load_b(g), (load_x(g) * w_all[:, gl]).astype(BF16), (((0,), (0,)), ((), ())),
                                preferred_element_type=F32)
        s_ref[g] = s_ref[g] * t_all[:, gl] + s_new
    for g in range(groups):
        xg = load_x(g)
        ws, xs = [], []
        for r in range(R):
            h = g * R + r
            diff = acs[:, h:h + 1] - acs_row[h:h + 1, :]
            lm = jnp.exp(jnp.where(mask, diff, -jnp.inf))
            ws.append((cbs[g] * lm * dt_row[h:h + 1, :]).astype(BF16))
            xs.append(jnp.where((lane >= r * P) & (lane < (r + 1) * P), xg, 0.0).astype(BF16))
        y_diag = _dot(jnp.concatenate(ws, axis=1), jnp.concatenate(xs, axis=0))
        gl = slice(g * R * P, (g + 1) * R * P)
        y_ref[:, gl] = y_diag + y_offs[g] * e_all[:, gl]


def _ssd(xbc, dt_raw, dt_bias, a_log, *, reverse, heads, groups):
    L = xbc.shape[0]
    Q = SSM_CHUNK
    assert L % Q == 0
    nc = L // Q
    W = heads * SSM_HEAD_DIM
    R = heads // groups
    cmap = (lambda i: (nc - 1 - i, 0)) if reverse else (lambda i: (i, 0))
    ex = np.zeros((LANES, W), np.float32)
    ex[np.arange(W) // SSM_HEAD_DIM, np.arange(W)] = 1.0
    return pl.pallas_call(
        functools.partial(_ssd_body, reverse=reverse, heads=heads, groups=groups), grid=(nc,),
        in_specs=[pl.BlockSpec((Q, xbc.shape[1]), cmap), pl.BlockSpec((Q, LANES), cmap),
                  pl.BlockSpec((1, LANES), lambda i: (0, 0)), pl.BlockSpec((1, LANES), lambda i: (0, 0)),
                  pl.BlockSpec((LANES, W), lambda i: (0, 0))],
        out_specs=pl.BlockSpec((Q, W), cmap),
        out_shape=jax.ShapeDtypeStruct((L, W), F32),
        scratch_shapes=[pltpu.VMEM((groups, SSM_STATE, R * SSM_HEAD_DIM), F32)],
        compiler_params=_cparams(("arbitrary",)), name="ssd_bwd" if reverse else "ssd_fwd",
    )(xbc, dt_raw, dt_bias, a_log, jnp.asarray(ex.astype(BF16)))


def _ssm_gate_body(yf_ref, yb_ref, xh_ref, z_ref, d_ref, g_ref, o_ref, *, groups):
    y = (yf_ref[...] + yb_ref[...] + d_ref[...] * xh_ref[...]) * _silu(z_ref[...])
    W = y.shape[1]
    gw = W // groups
    for g in range(groups):
        yg = y[:, g * gw:(g + 1) * gw]
        ms = jnp.mean(yg * yg, axis=-1, keepdims=True)
        o_ref[:, g * gw:(g + 1) * gw] = (yg * lax.rsqrt(ms + NORM_EPS) * g_ref[:, g * gw:(g + 1) * gw]).astype(o_ref.dtype)


def _ssm_gate(yf, yb, xbc, proj, z_off, d_rep, norm_g, groups):
    L, W = yf.shape
    bm = _pick(L, 256, SUBLANES)
    assert z_off % W == 0
    zo = z_off // W
    row = pl.BlockSpec((bm, W), lambda i: (i, 0))
    vec = pl.BlockSpec((1, W), lambda i: (0, 0))
    return pl.pallas_call(
        functools.partial(_ssm_gate_body, groups=groups), grid=(L // bm,),
        in_specs=[row, row, row, pl.BlockSpec((bm, W), lambda i: (i, zo)), vec, vec],
        out_specs=row, out_shape=jax.ShapeDtypeStruct((L, W), BF16),
        compiler_params=_cparams(("parallel",)), name="ssm_gate",
    )(yf, yb, xbc, proj, d_rep.reshape(1, W), norm_g.reshape(1, W))


def _dft_consts(L):
    N2 = DFT_N2
    N = 2 * L
    H1 = L // N2
    n2 = np.arange(N2)[:, None, None]
    k1 = np.arange(H1)[None, :, None]
    n1 = np.arange(H1)[None, None, :]
    m = ((2 * k1 + 1) * (N2 * n1 + n2)) % (2 * N)
    ph = np.pi * m.astype(np.float64) / N
    m_fwd = np.concatenate([np.cos(ph), -np.sin(ph)], axis=1)
    m_inv = (2.0 / N) * np.concatenate([np.cos(ph), -np.sin(ph)], axis=1).transpose(0, 2, 1)
    kk = (np.arange(N2)[:, None] * np.arange(N2)[None, :]) % N2
    c2 = np.cos(2 * np.pi * kk / N2)
    s2 = np.sin(2 * np.pi * kk / N2)
    f2_fwd = np.block([[c2, s2], [-s2, c2]])
    f2_inv = np.block([[c2, -s2], [s2, c2]])
    as_bf = lambda a: jnp.asarray(a.astype(BF16))
    return as_bf(m_fwd), as_bf(m_inv), as_bf(f2_fwd), as_bf(f2_inv)


def _pitch(rows):
    p = rows + SUBLANES
    assert (p // SUBLANES) % 2 == 1
    return p


def _stage3(a_s, f2_ref, k1, H1, pitch):
    N2 = DFT_N2
    ar = a_s[pl.ds(k1, N2, stride=pitch), :]
    ai = a_s[pl.ds(H1 + k1, N2, stride=pitch), :]
    av = jnp.concatenate([ar, ai], axis=0).astype(BF16)
    return _dot(f2_ref[...], av)


def _fwd_body(u_ref, m_ref, f2_ref, x_ref, a_s, *, H1, ks, gw, ng):
    N2, cb = DFT_N2, HY_CB
    pitch = _pitch(2 * H1)
    s = pl.program_id(1)

    @pl.when(s < ng)
    def _():
        for t in range(gw):
            n2 = s * gw + t
            u = u_ref[:, t * cb:(t + 1) * cb].astype(BF16)
            a_s[pl.ds(pl.multiple_of(n2 * pitch, SUBLANES), 2 * H1), :] = _dot(m_ref[n2], u)

    @pl.when(s >= ng)
    def _():
        def kbody(kk, carry):
            x = _stage3(a_s, f2_ref, (s - ng) * ks + kk, H1, pitch)
            x_ref[kk, 0] = x[:N2].astype(x_ref.dtype)
            x_ref[kk, 1] = x[N2:].astype(x_ref.dtype)
            return carry
        lax.fori_loop(0, ks, kbody, 0, unroll=DFT_UNROLL)


def _hy_fwd(u_blk, slab_off, nj, consts):
    L = u_blk.shape[1]
    N2, cb = DFT_N2, HY_CB
    H1 = L // N2
    ks = _pick(H1, 16, 1)
    gw = _pick(N2, DFT_GROUP, 1)
    ng, nsl = N2 // gw, H1 // ks
    m_fwd, _, f2_fwd, _ = consts
    u2d = u_blk.reshape(u_blk.shape[0], H1, N2 * cb)
    return pl.pallas_call(
        functools.partial(_fwd_body, H1=H1, ks=ks, gw=gw, ng=ng), grid=(nj, ng + nsl),
        in_specs=[pl.BlockSpec((None, H1, gw * cb), lambda j, s: (slab_off + j, 0, jnp.minimum(s, ng - 1))),
                  pl.BlockSpec((N2, 2 * H1, H1), lambda j, s: (0, 0, 0)),
                  pl.BlockSpec((2 * N2, 2 * N2), lambda j, s: (0, 0))],
        out_specs=pl.BlockSpec((None, ks, 2, N2, cb), lambda j, s: (j, jnp.maximum(s - ng, 0), 0, 0, 0)),
        out_shape=jax.ShapeDtypeStruct((nj, H1, 2, N2, cb), SPEC_DTYPE),
        scratch_shapes=[pltpu.VMEM((N2 * _pitch(2 * H1), cb), F32)],
        compiler_params=_cparams(("parallel", "arbitrary")), name="hy_fwd",
    )(u2d, m_fwd, f2_fwd)


def _inv_body(x_ref, g_ref, m_ref, f2_ref, y_ref, b_s, *, H1, ks, gw, nsl):
    N2, cb = DFT_N2, HY_CB
    pitch = _pitch(2 * N2)
    s = pl.program_id(1)

    @pl.when(s < nsl)
    def _():
        def kbody(kk, carry):
            xr, xi = x_ref[kk, 0].astype(F32), x_ref[kk, 1].astype(F32)
            gr, gi = g_ref[kk, 0].astype(F32), g_ref[kk, 1].astype(F32)
            yr = xr * gr - xi * gi
            yi = xr * gi + xi * gr
            b = _dot(f2_ref[...], jnp.concatenate([yr, yi], axis=0).astype(BF16))
            b_s[pl.ds(pl.multiple_of((s * ks + kk) * pitch, SUBLANES), 2 * N2), :] = b
            return carry
        lax.fori_loop(0, ks, kbody, 0, unroll=DFT_UNROLL)

    @pl.when(s >= nsl)
    def _():
        for t in range(gw):
            n2 = (s - nsl) * gw + t
            br = b_s[pl.ds(n2, H1, stride=pitch), :]
            bi = b_s[pl.ds(N2 + n2, H1, stride=pitch), :]
            y_ref[:, t * cb:(t + 1) * cb] = _dot(m_ref[n2], jnp.concatenate([br, bi], axis=0).astype(BF16))


def _hy_inv(x, g_all, order, consts, L):
    nj, H1, _, N2, cb = x.shape
    ks = _pick(H1, 16, 1)
    gw = _pick(N2, DFT_GROUP, 1)
    ng, nsl = N2 // gw, H1 // ks
    _, m_inv, _, f2_inv = consts
    y2d = pl.pallas_call(
        functools.partial(_inv_body, H1=H1, ks=ks, gw=gw, nsl=nsl), grid=(nj, nsl + ng),
        in_specs=[pl.BlockSpec((None, ks, 2, N2, cb), lambda j, s: (j, jnp.minimum(s, nsl - 1), 0, 0, 0)),
                  pl.BlockSpec((None, None, ks, 2, N2, cb),
                               lambda j, s: (order, j, jnp.minimum(s, nsl - 1), 0, 0, 0)),
                  pl.BlockSpec((N2, H1, 2 * H1), lambda j, s: (0, 0, 0)),
                  pl.BlockSpec((2 * N2, 2 * N2), lambda j, s: (0, 0))],
        out_specs=pl.BlockSpec((None, H1, gw * cb), lambda j, s: (j, 0, jnp.maximum(s - nsl, 0))),
        out_shape=jax.ShapeDtypeStruct((nj, H1, N2 * cb), F32),
        scratch_shapes=[pltpu.VMEM((H1 * _pitch(2 * N2), cb), F32)],
        compiler_params=_cparams(("parallel", "arbitrary")), name="hy_inv",
    )(x, g_all, m_inv, f2_inv)
    return y2d.reshape(nj, L, cb)


def _pos_features(L, emb):
    N2 = DFT_N2
    H1 = L // N2
    bands = (emb - 1) // 2
    r = jnp.arange(L)
    lag = (r % H1) * N2 + (r // H1)
    f = jnp.linspace(1e-4, bands - 1, bands, dtype=F32)[None, :]

    def feats(lg):
        lgf = lg.astype(F32)[:, None]
        t = lgf / (L - 1)
        w = 2.0 * math.pi * lgf / L
        z = jnp.concatenate([t, jnp.cos(f * w), -jnp.sin(f * w)], axis=-1)
        return jnp.pad(z, ((0, 0), (0, LANES - emb)))
    return jnp.concatenate([feats(lag), feats(jnp.where(lag == 0, 0, L - lag))], axis=-1)


def _filter_mlp_body(z_ref, w1_ref, b1_ref, w2_ref, b2_ref, w3_ref, b3_ref, fr_ref, o_ref):
    h = jnp.sin(fr_ref[0:1, :] * (_dot_f32(z_ref[...], w1_ref[...]) + b1_ref[...]))
    h = jnp.sin(fr_ref[1:2, :] * (_dot_f32(h, w2_ref[...]) + b2_ref[...]))
    h = jnp.sin(fr_ref[2:3, :] * (_dot_f32(h, w3_ref[...]) + b3_ref[...]))
    o_ref[...] = h.astype(o_ref.dtype)


def _bdiag(w):
    a, b = w.shape
    z = jnp.zeros((a, b), w.dtype)
    return jnp.concatenate([jnp.concatenate([w, z], axis=1), jnp.concatenate([z, w], axis=1)], axis=0)


def _filter_mlp(z2, w1, b1, w2, b2, w3, b3, fr):
    L = z2.shape[0]
    emb, hd = w1.shape
    assert 2 * hd == LANES
    w1p = jnp.zeros((2 * LANES, LANES), F32).at[:emb, :hd].set(w1).at[LANES:LANES + emb, hd:].set(w1)
    tile2 = lambda v: jnp.concatenate([v, v], axis=-1)
    bm = _pick(L, 1024, SUBLANES)
    full = lambda a: pl.BlockSpec(a.shape, lambda i: (0,) * a.ndim)
    args = (w1p, tile2(b1).reshape(1, LANES), _bdiag(w2), tile2(b2).reshape(1, LANES),
            _bdiag(w3), tile2(b3).reshape(1, LANES), tile2(fr))
    return pl.pallas_call(
        _filter_mlp_body, grid=(L // bm,),
        in_specs=[pl.BlockSpec((bm, 2 * LANES), lambda i: (i, 0))] + [full(a) for a in args],
        out_specs=pl.BlockSpec((bm, LANES), lambda i: (i, 0)),
        out_shape=jax.ShapeDtypeStruct((L, LANES), BF16),
        compiler_params=_cparams(("parallel",)), name="hy_filter_mlp",
    )(z2, *args)


def _filt_body(hm_ref, w2_ref, ad_ref, m_ref, f2_ref, g_ref, a_s, g_s, inv_s, *, L, H1, ks):
    N2, cb = DFT_N2, HY_CB
    pitch = _pitch(2 * H1)
    s = pl.program_id(2)

    @pl.when(s == 0)
    def _():
        rate = ad_ref[...] * (1.0 / (L - 1))
        n1 = lax.broadcasted_iota(jnp.int32, (H1, cb), 0)
        sgn = jnp.where(n1 % 2 == 0, 1.0, -1.0)
        e1f = jnp.exp(-(n1 * N2).astype(F32) * rate)
        e1b = jnp.exp(-((H1 - 1 - n1) * N2).astype(F32) * rate)

        def group(gi, nrm):
            def gen(t, nrm):
                n2 = gi * FILT_GROUP + t
                hs = hm_ref[pl.ds(pl.multiple_of(n2 * H1, SUBLANES), H1), :]
                gp = _dot(hs, w2_ref[...])
                n2f = n2.astype(F32)
                gf = gp[:, :cb] * (e1f * jnp.exp(-n2f * rate))
                gb = gp[:, cb:] * (e1b * jnp.exp(-(N2 - n2f) * rate))
                gb = jnp.where(n1 + n2 == 0, 0.0, -gb)
                g_s[t, 0] = gf.astype(BF16)
                g_s[t, 1] = gb.astype(BF16)
                return nrm + (jnp.abs(gf) + jnp.abs(gb))
            nrm = lax.fori_loop(0, FILT_GROUP, gen, nrm, unroll=DFT_UNROLL)

            def xform(t, carry):
                n2 = gi * FILT_GROUP + t
                m = m_ref[n2]
                a1 = _dot(m, g_s[t, 0])
                a2 = _dot(m, g_s[t, 1])
                base = pl.multiple_of(n2 * pitch, SUBLANES)
                a_s[pl.ds(base, H1), :] = a1[:H1] + sgn * a2[H1:]
                a_s[pl.ds(base + H1, H1), :] = a1[H1:] - sgn * a2[:H1]
                return carry
            lax.fori_loop(0, FILT_GROUP, xform, 0, unroll=DFT_UNROLL)
            return nrm
        nrm = lax.fori_loop(0, N2 // FILT_GROUP, group, jnp.zeros((H1, cb), F32))
        inv_s[...] = jnp.broadcast_to(1.0 / jnp.sum(nrm, axis=0, keepdims=True), inv_s.shape)

    scale = inv_s[0:1, :]

    def kbody(kk, carry):
        x = _stage3(a_s, f2_ref, s * ks + kk, H1, pitch) * scale
        g_ref[kk, 0] = x[:N2].astype(g_ref.dtype)
        g_ref[kk, 1] = x[N2:].astype(g_ref.dtype)
        return carry
    lax.fori_loop(0, ks, kbody, 0, unroll=DFT_UNROLL)


def _hy_filters(hm, wout, C, L, consts):
    N2, cb = DFT_N2, HY_CB
    H1 = L // N2
    ks = _pick(H1, 16, 1)
    hd = wout.shape[0]
    nj = C // cb
    m_fwd, _, f2_fwd, _ = consts
    w4 = wout.reshape(hd, 2, 2, nj, cb)
    wf = jnp.transpose(w4[:, :, 0], (1, 2, 0, 3))
    wb = jnp.transpose(w4[:, :, 1], (1, 2, 0, 3))
    z = jnp.zeros_like(wf)
    w2 = jnp.concatenate([jnp.concatenate([wf, z], axis=-1), jnp.concatenate([z, wb], axis=-1)], axis=2)
    w2 = w2.astype(BF16)
    deltas = jnp.linspace(math.log(HY_FAST_DECAY) / HY_DECAY_TARGET,
                          math.log(HY_SLOW_DECAY) / HY_DECAY_TARGET, C, dtype=F32)
    absd = jnp.abs(deltas).reshape(1, C)
    return pl.pallas_call(
        functools.partial(_filt_body, L=L, H1=H1, ks=ks), grid=(2, nj, H1 // ks),
        in_specs=[pl.BlockSpec((L, LANES), lambda o, j, s: (0, 0)),
                  pl.BlockSpec((None, None, 2 * hd, 2 * cb), lambda o, j, s: (o, j, 0, 0)),
                  pl.BlockSpec((1, cb), lambda o, j, s: (0, j)),
                  pl.BlockSpec((N2, 2 * H1, H1), lambda o, j, s: (0, 0, 0)),
                  pl.BlockSpec((2 * N2, 2 * N2), lambda o, j, s: (0, 0))],
        out_specs=pl.BlockSpec((None, None, ks, 2, N2, cb), lambda o, j, s: (o, j, s, 0, 0, 0)),
        out_shape=jax.ShapeDtypeStruct((2, nj, H1, 2, N2, cb), SPEC_DTYPE),
        scratch_shapes=[pltpu.VMEM((N2 * _pitch(2 * H1), cb), F32), pltpu.VMEM((FILT_GROUP, 2, H1, cb), BF16),
                        pltpu.VMEM((SUBLANES, cb), F32)],
        compiler_params=_cparams(("parallel", "parallel", "arbitrary")), name="hy_filters",
    )(hm, w2, absd, m_fwd, f2_fwd)


def _hy_gate1_body(gate_ref, v_ref, y_ref, sk_ref, o_ref):
    o_ref[...] = gate_ref[...] * (y_ref[...] + sk_ref[...] * v_ref[...])


def _hy_gate1(uh_blk, gate_blk, v_blk, y_blk, sk):
    nj, L, cb = y_blk.shape
    bm = _pick(L, 256, SUBLANES)
    blk = lambda g: pl.BlockSpec((nj, bm, cb), lambda i: (g, i, 0))
    return pl.pallas_call(
        _hy_gate1_body, grid=(L // bm,),
        in_specs=[blk(gate_blk), blk(v_blk), blk(0), pl.BlockSpec((nj, 1, cb), lambda i: (0, 0, 0))],
        out_specs=blk(0), out_shape=jax.ShapeDtypeStruct((nj, L, cb), F32),
        compiler_params=_cparams(("parallel",)), name="hy_gate1",
    )(uh_blk, uh_blk, y_blk, sk.reshape(nj, 1, cb))


def _hy_gate2_body(gate_ref, v_ref, y_ref, sk_ref, g_ref, o_ref):
    z = gate_ref[...] * (y_ref[...] + sk_ref[...] * v_ref[...])
    nj, _, cb = z.shape
    ms = jnp.sum(jnp.sum(z * z, axis=0), axis=-1, keepdims=True) * (1.0 / (nj * cb))
    r = lax.rsqrt(ms + NORM_EPS)
    for k in range(nj):
        o_ref[:, k * cb:(k + 1) * cb] = (z[k] * r * g_ref[k]).astype(o_ref.dtype)


def _hy_gate2(uh_blk, gate_blk, v_blk, y_blk, sk, g):
    nj, L, cb = y_blk.shape
    bm = _pick(L, 256, SUBLANES)
    blk = lambda gidx: pl.BlockSpec((nj, bm, cb), lambda i: (gidx, i, 0))
    vec = pl.BlockSpec((nj, 1, cb), lambda i: (0, 0, 0))
    return pl.pallas_call(
        _hy_gate2_body, grid=(L // bm,),
        in_specs=[blk(gate_blk), blk(0), blk(0), vec, vec],
        out_specs=pl.BlockSpec((bm, nj * cb), lambda i: (i, 0)),
        out_shape=jax.ShapeDtypeStruct((L, nj * cb), BF16),
        compiler_params=_cparams(("parallel",)), name="hy_gate2",
    )(uh_blk, v_blk, y_blk, sk.reshape(nj, 1, cb), g.reshape(nj, 1, cb))


def kernel(x, norm_mix_g, w_in, hy_conv_w, hy_conv_b, hy_pos_w1, hy_pos_b1, hy_pos_w2, hy_pos_b2, hy_pos_w3, hy_pos_b3, hy_sin_freq, hy_pos_wout, hy_skip, hy_out_g, ssm_conv_w, ssm_conv_b, ssm_A_log, ssm_dt_bias, ssm_D, ssm_out_g, w_out, norm_mlp_g, w_up, w_down, final_norm_g):
    B, L, D = x.shape
    depth = w_in.shape[0]
    CH = hy_out_g.shape[1]
    WS = ssm_out_g.shape[1]
    heads = ssm_D.shape[1]
    xbc_w = ssm_conv_w.shape[2]
    groups = (xbc_w - WS) // (2 * SSM_STATE)
    o1 = 3 * CH
    o2 = o1 + WS
    o3 = o2 + xbc_w
    assert WS == heads * SSM_HEAD_DIM and heads <= LANES and L % DFT_N2 == 0 and CH % HY_CB == 0

    consts = _dft_consts(L)
    z2 = _pos_features(L, hy_pos_w1.shape[1])
    w_in_b = w_in.astype(BF16)
    w_in_dt = jnp.pad(w_in[:, :, o3:], ((0, 0), (0, 0), (0, LANES - heads))).astype(BF16)
    w_out_b = w_out.astype(BF16)
    w_up_b = w_up.astype(BF16)
    w_down_b = w_down.astype(BF16)
    pad_h = lambda v: jnp.pad(v, (0, LANES - heads)).reshape(1, LANES)
    d_rep = jnp.repeat(ssm_D, SSM_HEAD_DIM, axis=-1)

    outs = []
    for b in range(B):
        xb = x[b]
        delta = None
        for l in range(depth):
            if delta is None:
                h = _rmsnorm(xb, norm_mix_g[l], BF16)
            else:
                xb, h = _add_rmsnorm(xb, delta, norm_mix_g[l], BF16)
            proj = _matmul(h, w_in_b, l, n_cols=o3)
            dt_raw = _matmul(h, w_in_dt, l, bm_t=2048)

            nj = CH // HY_CB
            uh = _dwconv(proj, 0, o1, hy_conv_w[l], hy_conv_b[l], "none", blocked_out=True)
            hm = _filter_mlp(z2, hy_pos_w1[l], hy_pos_b1[l], hy_pos_w2[l], hy_pos_b2[l],
                             hy_pos_w3[l], hy_pos_b3[l], hy_sin_freq[l])
            g_all = _hy_filters(hm, hy_pos_wout[l], CH, L, consts)
            y0 = _hy_inv(_hy_fwd(uh, 2 * nj, nj, consts), g_all, 0, consts, L)
            z1 = _hy_gate1(uh, 0, 2, y0, hy_skip[l, 0])
            y1 = _hy_inv(_hy_fwd(z1, 0, nj, consts), g_all, 1, consts, L)
            y_hy = _hy_gate2(uh, 1, z1, y1, hy_skip[l, 1], hy_out_g[l])

            xbc = _dwconv(proj, o2, xbc_w, ssm_conv_w[l], ssm_conv_b[l], "silu")
            yf = _ssd(xbc, dt_raw, pad_h(ssm_dt_bias[l, 0]), pad_h(ssm_A_log[l, 0]),
                      reverse=False, heads=heads, groups=groups)
            yb = _ssd(xbc, dt_raw, pad_h(ssm_dt_bias[l, 1]), pad_h(ssm_A_log[l, 1]),
                      reverse=True, heads=heads, groups=groups)
            y_ssm = _ssm_gate(yf, yb, xbc, proj, o1, d_rep[l], ssm_out_g[l], groups)

            xb = _out_matmul(y_hy, y_ssm, w_out_b, l, xb)
            h2 = _rmsnorm(xb, norm_mlp_g[l], BF16)
            delta = _mlp(h2, w_up_b, w_down_b, l)
        outs.append(_add_rmsnorm(xb, delta, final_norm_g, F32, keep_sum=False))
    return jnp.stack(outs, axis=0)
```

```python
import functools
import math

import numpy as np
import jax
import jax.numpy as jnp
from jax import lax
from jax.experimental import pallas as pl
from jax.experimental.pallas import tpu as pltpu

F32 = jnp.float32
BF16 = jnp.bfloat16

NORM_EPS = 1e-5
SSM_HEAD_DIM = 64
SSM_STATE = 128
SSM_CHUNK = 256
HY_FAST_DECAY = 0.3
HY_SLOW_DECAY = 1.5
HY_DECAY_TARGET = 1e-2

LANES = 128
SUBLANES = 8
DFT_N2 = 128
HY_CB = 128
DFT_UNROLL = 16
DFT_GROUP = 32
TT_CHUNK = SUBLANES * DFT_N2
FILT_GROUP = 16
SPEC_DTYPE = BF16
VMEM_LIMIT_MB = 56


def _cparams(sem, vmem_mb=VMEM_LIMIT_MB):
    return pltpu.CompilerParams(dimension_semantics=sem, vmem_limit_bytes=vmem_mb * 1024 * 1024)


def _pick(n, target, mult):
    best = None
    for d in range(mult, min(n, target) + 1, mult):
        if n % d == 0:
            best = d
    assert best is not None, (n, target, mult)
    return best


def _dot(a, b):
    return jnp.dot(a, b, preferred_element_type=F32)


def _split3(a):
    hi = a.astype(BF16)
    r1 = a - hi.astype(F32)
    mid = r1.astype(BF16)
    lo = (r1 - mid.astype(F32)).astype(BF16)
    return hi, mid, lo


def _dot_f32(a, b):
    ah = a.astype(BF16)
    al = (a - ah.astype(F32)).astype(BF16)
    bh = b.astype(BF16)
    bl = (b - bh.astype(F32)).astype(BF16)
    return _dot(ah, bh) + (_dot(ah, bl) + _dot(al, bh))


def _silu(x):
    return x / (1.0 + jnp.exp(-x))


def _rmsnorm_body(x_ref, g_ref, h_ref):
    x = x_ref[...]
    ms = jnp.mean(x * x, axis=-1, keepdims=True)
    h_ref[...] = (x * lax.rsqrt(ms + NORM_EPS) * g_ref[...]).astype(h_ref.dtype)


def _rmsnorm(x, g, out_dtype):
    M, D = x.shape
    bm = _pick(M, 256, SUBLANES)
    return pl.pallas_call(
        _rmsnorm_body, grid=(M // bm,),
        in_specs=[pl.BlockSpec((bm, D), lambda i: (i, 0)), pl.BlockSpec((1, D), lambda i: (0, 0))],
        out_specs=pl.BlockSpec((bm, D), lambda i: (i, 0)),
        out_shape=jax.ShapeDtypeStruct((M, D), out_dtype),
        compiler_params=_cparams(("parallel",)), name="rmsnorm",
    )(x, g.reshape(1, D))


def _add_rmsnorm_body(x_ref, d_ref, g_ref, *out_refs):
    x = x_ref[...] + d_ref[...]
    if len(out_refs) == 2:
        out_refs[0][...] = x
    h_ref = out_refs[-1]
    ms = jnp.mean(x * x, axis=-1, keepdims=True)
    h_ref[...] = (x * lax.rsqrt(ms + NORM_EPS) * g_ref[...]).astype(h_ref.dtype)


def _add_rmsnorm(x, delta, g, out_dtype, keep_sum=True):
    M, D = x.shape
    bm = _pick(M, 256, SUBLANES)
    row = pl.BlockSpec((bm, D), lambda i: (i, 0))
    h_shape = jax.ShapeDtypeStruct((M, D), out_dtype)
    return pl.pallas_call(
        _add_rmsnorm_body, grid=(M // bm,),
        in_specs=[row, row, pl.BlockSpec((1, D), lambda i: (0, 0))],
        out_specs=[row, row] if keep_sum else row,
        out_shape=[jax.ShapeDtypeStruct((M, D), F32), h_shape] if keep_sum else h_shape,
        compiler_params=_cparams(("parallel",)), name="add_rmsnorm",
    )(x, delta, g.reshape(1, D))


def _mm_body(a_ref, b_ref, o_ref):
    o_ref[...] = _dot(a_ref[...], b_ref[...]).astype(o_ref.dtype)


def _matmul(a, b_stack, layer, n_cols=None, out_dtype=F32, bm_t=1024, bn_t=1024):
    M, K = a.shape
    N = b_stack.shape[2] if n_cols is None else n_cols
    bm = _pick(M, bm_t, SUBLANES)
    bn = _pick(N, bn_t, LANES)
    return pl.pallas_call(
        _mm_body, grid=(M // bm, N // bn),
        in_specs=[pl.BlockSpec((bm, K), lambda i, j: (i, 0)),
                  pl.BlockSpec((None, K, bn), lambda i, j: (layer, 0, j))],
        out_specs=pl.BlockSpec((bm, bn), lambda i, j: (i, j)),
        out_shape=jax.ShapeDtypeStruct((M, N), out_dtype),
        compiler_params=_cparams(("parallel", "parallel")), name="proj_matmul",
    )(a, b_stack)


def _out_mm_body(a1_ref, a2_ref, b1_ref, b2_ref, r_ref, o_ref):
    acc = _dot(a1_ref[...], b1_ref[...]) + _dot(a2_ref[...], b2_ref[...])
    o_ref[...] = r_ref[...] + acc


def _out_matmul(a1, a2, b_stack, layer, res):
    M, K1 = a1.shape
    assert a2.shape[1] == K1 and b_stack.shape[1] == 2 * K1
    N = b_stack.shape[2]
    bm = _pick(M, 1024, SUBLANES)
    bn = _pick(N, 1024, LANES)
    return pl.pallas_call(
        _out_mm_body, grid=(M // bm, N // bn),
        in_specs=[pl.BlockSpec((bm, K1), lambda i, j: (i, 0)), pl.BlockSpec((bm, K1), lambda i, j: (i, 0)),
                  pl.BlockSpec((None, K1, bn), lambda i, j: (layer, 0, j)),
                  pl.BlockSpec((None, K1, bn), lambda i, j: (layer, 1, j)),
                  pl.BlockSpec((bm, bn), lambda i, j: (i, j))],
        out_specs=pl.BlockSpec((bm, bn), lambda i, j: (i, j)),
        out_shape=jax.ShapeDtypeStruct((M, N), F32),
        compiler_params=_cparams(("parallel", "parallel")), name="out_matmul",
    )(a1, a2, b_stack, b_stack, res)


def _mlp_body(h_ref, wu_ref, wd_ref, o_ref, *, cw):
    j = pl.program_id(1)

    @pl.when(j == 0)
    def _():
        o_ref[...] = jnp.zeros_like(o_ref)

    hw = wu_ref.shape[1] // 2
    ts = []
    for q in range(2):
        t = _dot(h_ref[...], wu_ref[:, q * hw:(q + 1) * hw])
        ts.append(jnp.square(jnp.maximum(t, 0.0)).astype(BF16))
    for c in range(o_ref.shape[1] // cw):
        cols = slice(c * cw, (c + 1) * cw)
        o_ref[:, cols] += _dot(ts[0], wd_ref[0:hw, cols]) + _dot(ts[1], wd_ref[hw:2 * hw, cols])


def _mlp(h, w_up_stack, w_down_stack, layer):
    M, D = h.shape
    F = w_up_stack.shape[2]
    bm = _pick(M, 512, SUBLANES)
    bf = _pick(F, 512, LANES)
    cw = _pick(D, 512, LANES)
    return pl.pallas_call(
        functools.partial(_mlp_body, cw=cw), grid=(M // bm, F // bf),
        in_specs=[pl.BlockSpec((bm, D), lambda i, j: (i, 0)),
                  pl.BlockSpec((None, D, bf), lambda i, j: (layer, 0, j)),
                  pl.BlockSpec((None, bf, D), lambda i, j: (layer, j, 0))],
        out_specs=pl.BlockSpec((bm, D), lambda i, j: (i, 0)),
        out_shape=jax.ShapeDtypeStruct((M, D), F32),
        compiler_params=_cparams(("parallel", "arbitrary")), name="mlp",
    )(h, w_up_stack, w_down_stack)


def _dwconv_body(p_ref, pv_ref, nx_ref, w_ref, b_ref, o_ref, *, act):
    i = pl.program_id(0)
    last = pl.num_programs(0) - 1
    u = p_ref[...]
    bm = u.shape[0]
    prev_row = jnp.where(i > 0, pv_ref[SUBLANES - 1:SUBLANES, :], 0.0)
    next_row = jnp.where(i < last, nx_ref[0:1, :], 0.0)
    rows = lax.broadcasted_iota(jnp.int32, u.shape, 0)
    um = jnp.where(rows == 0, prev_row, pltpu.roll(u, 1, 0))
    up = jnp.where(rows == bm - 1, next_row, pltpu.roll(u, bm - 1, 0))
    y = w_ref[0:1, :] * um
    y = y + w_ref[1:2, :] * u
    y = y + w_ref[2:3, :] * up
    y = y + b_ref[...]
    if act == "silu":
        y = _silu(y)
    if len(o_ref.shape) == 3:
        assert bm == TT_CHUNK
        for k in range(o_ref.shape[0]):
            for r in range(SUBLANES):
                o_ref.at[k][pl.ds(r, DFT_N2, stride=SUBLANES), :] = (
                    y[r * DFT_N2:(r + 1) * DFT_N2, k * LANES:(k + 1) * LANES])
    else:
        o_ref[...] = y


def _dwconv(p, col_off, C, w, b, act, blocked_out=False):
    L = p.shape[0]
    bm = TT_CHUNK if blocked_out else _pick(L, 1024, SUBLANES)
    assert L % bm == 0
    bc = _pick(math.gcd(C, col_off) if col_off else C, 1024, LANES)
    co = col_off // bc
    hb = bm // SUBLANES
    nrb = L // SUBLANES
    if blocked_out:
        spb = bc // LANES
        out_spec = pl.BlockSpec((spb, bm, LANES), lambda i, j: (j, i, 0))
        out_shape = jax.ShapeDtypeStruct((C // LANES, L, LANES), F32)
    else:
        out_spec = pl.BlockSpec((bm, bc), lambda i, j: (i, j))
        out_shape = jax.ShapeDtypeStruct((L, C), F32)
    return pl.pallas_call(
        functools.partial(_dwconv_body, act=act), grid=(L // bm, C // bc),
        in_specs=[pl.BlockSpec((bm, bc), lambda i, j: (i, co + j)),
                  pl.BlockSpec((SUBLANES, bc), lambda i, j: (jnp.maximum(i * hb - 1, 0), co + j)),
                  pl.BlockSpec((SUBLANES, bc), lambda i, j: (jnp.minimum((i + 1) * hb, nrb - 1), co + j)),
                  pl.BlockSpec((3, bc), lambda i, j: (0, j)),
                  pl.BlockSpec((1, bc), lambda i, j: (0, j))],
        out_specs=out_spec, out_shape=out_shape,
        compiler_params=_cparams(("parallel", "parallel")), name="dwconv_" + act,
    )(p, p, p, w, b.reshape(1, C))


def _expand_heads(v, ex_ref):
    hi = v.astype(BF16)
    lo = (v - hi.astype(F32)).astype(BF16)
    return _dot(hi, ex_ref[...]) + _dot(lo, ex_ref[...])


def _ssd_body(xbc_ref, dt_ref, bias_ref, alog_ref, ex_ref, y_ref, s_ref, *, reverse, heads, groups):
    Q = SSM_CHUNK
    P = SSM_HEAD_DIM
    N = SSM_STATE
    R = heads // groups
    W = heads * P
    step = pl.program_id(0)

    @pl.when(step == 0)
    def _():
        s_ref[...] = jnp.zeros_like(s_ref)

    x = dt_ref[...] + bias_ref[...]
    dt = jnp.maximum(x, 0.0) + jnp.log1p(jnp.exp(-jnp.abs(x)))
    a = dt * (-jnp.exp(alog_ref[...]))
    ri = lax.broadcasted_iota(jnp.int32, (Q, Q), 0)
    ci = lax.broadcasted_iota(jnp.int32, (Q, Q), 1)
    mask = (ci >= ri) if reverse else (ri >= ci)
    tri = jnp.where(mask, 1.0, 0.0).astype(BF16)
    ah, am, al = _split3(a)
    acs = _dot(tri, ah) + (_dot(tri, am) + _dot(tri, al))
    tot = acs[0:1, :] if reverse else acs[Q - 1:Q, :]
    e_all = _expand_heads(jnp.exp(acs), ex_ref)
    w_all = _expand_heads(jnp.exp(tot - acs) * dt, ex_ref)
    t_all = _expand_heads(jnp.broadcast_to(jnp.exp(tot), (SUBLANES, LANES)), ex_ref)[0:1, :]
    acs_row = acs.T
    dt_row = dt.T

    load_b = lambda g: xbc_ref[:, W + g * N:W + (g + 1) * N].astype(BF16)
    load_c = lambda g: xbc_ref[:, W + groups * N + g * N:W + groups * N + (g + 1) * N].astype(BF16)
    load_x = lambda g: xbc_ref[:, g * R * P:(g + 1) * R * P]
    lane = lax.broadcasted_iota(jnp.int32, (Q, R * P), 1)

    cbs, y_offs = [], []
    for g in range(groups):
        cg = load_c(g)
        cbs.append(lax.dot_general(cg, load_b(g), (((1,), (1,)), ((), ())), preferred_element_type=F32))
        y_offs.append(_dot(cg, s_ref[g].astype(BF16)))
    for g in range(groups):
        gl = slice(g * R * P, (g + 1) * R * P)
        s_new = lax.dot_general(load_b(g), (load_x(g) * w_all[:, gl]).astype(BF16), (((0,), (0,)), ((), ())),
                                preferred_element_type=F32)
        s_ref[g] = s_ref[g] * t_all[:, gl] + s_new
    for g in range(groups):
        xg = load_x(g)
        ws, xs = [], []
        for r in range(R):
            h = g * R + r
            diff = acs[:, h:h + 1] - acs_row[h:h + 1, :]
            lm = jnp.exp(jnp.where(mask, diff, -jnp.inf))
            ws.append((cbs[g] * lm * dt_row[h:h + 1, :]).astype(BF16))
            xs.append(jnp.where((lane >= r * P) & (lane < (r + 1) * P), xg, 0.0).astype(BF16))
        y_diag = _dot(jnp.concatenate(ws, axis=1), jnp.concatenate(xs, axis=0))
        gl = slice(g * R * P, (g + 1) * R * P)
        y_ref[:, gl] = y_diag + y_offs[g] * e_all[:, gl]


def _ssd(xbc, dt_raw, dt_bias, a_log, *, reverse, heads, groups):
    L = xbc.shape[0]
    Q = SSM_CHUNK
    assert L % Q == 0
    nc = L // Q
    W = heads * SSM_HEAD_DIM
    R = heads // groups
    cmap = (lambda i: (nc - 1 - i, 0)) if reverse else (lambda i: (i, 0))
    ex = np.zeros((LANES, W), np.float32)
    ex[np.arange(W) // SSM_HEAD_DIM, np.arange(W)] = 1.0
    return pl.pallas_call(
        functools.partial(_ssd_body, reverse=reverse, heads=heads, groups=groups), grid=(nc,),
        in_specs=[pl.BlockSpec((Q, xbc.shape[1]), cmap), pl.BlockSpec((Q, LANES), cmap),
                  pl.BlockSpec((1, LANES), lambda i: (0, 0)), pl.BlockSpec((1, LANES), lambda i: (0, 0)),
                  pl.BlockSpec((LANES, W), lambda i: (0, 0))],
        out_specs=pl.BlockSpec((Q, W), cmap),
        out_shape=jax.ShapeDtypeStruct((L, W), F32),
        scratch_shapes=[pltpu.VMEM((groups, SSM_STATE, R * SSM_HEAD_DIM), F32)],
        compiler_params=_cparams(("arbitrary",)), name="ssd_bwd" if reverse else "ssd_fwd",
    )(xbc, dt_raw, dt_bias, a_log, jnp.asarray(ex.astype(BF16)))


def _ssm_gate_body(yf_ref, yb_ref, xh_ref, z_ref, d_ref, g_ref, o_ref, *, groups):
    y = (yf_ref[...] + yb_ref[...] + d_ref[...] * xh_ref[...]) * _silu(z_ref[...])
    W = y.shape[1]
    gw = W // groups
    for g in range(groups):
        yg = y[:, g * gw:(g + 1) * gw]
        ms = jnp.mean(yg * yg, axis=-1, keepdims=True)
        o_ref[:, g * gw:(g + 1) * gw] = (yg * lax.rsqrt(ms + NORM_EPS) * g_ref[:, g * gw:(g + 1) * gw]).astype(o_ref.dtype)


def _ssm_gate(yf, yb, xbc, proj, z_off, d_rep, norm_g, groups):
    L, W = yf.shape
    bm = _pick(L, 256, SUBLANES)
    assert z_off % W == 0
    zo = z_off // W
    row = pl.BlockSpec((bm, W), lambda i: (i, 0))
    vec = pl.BlockSpec((1, W), lambda i: (0, 0))
    return pl.pallas_call(
        functools.partial(_ssm_gate_body, groups=groups), grid=(L // bm,),
        in_specs=[row, row, row, pl.BlockSpec((bm, W), lambda i: (i, zo)), vec, vec],
        out_specs=row, out_shape=jax.ShapeDtypeStruct((L, W), BF16),
        compiler_params=_cparams(("parallel",)), name="ssm_gate",
    )(yf, yb, xbc, proj, d_rep.reshape(1, W), norm_g.reshape(1, W))


def _dft_consts(L):
    N2 = DFT_N2
    N = 2 * L
    H1 = L // N2
    n2 = np.arange(N2)[:, None, None]
    k1 = np.arange(H1)[None, :, None]
    n1 = np.arange(H1)[None, None, :]
    m = ((2 * k1 + 1) * (N2 * n1 + n2)) % (2 * N)
    ph = np.pi * m.astype(np.float64) / N
    m_fwd = np.concatenate([np.cos(ph), -np.sin(ph)], axis=1)
    m_inv = (2.0 / N) * np.concatenate([np.cos(ph), -np.sin(ph)], axis=1).transpose(0, 2, 1)
    kk = (np.arange(N2)[:, None] * np.arange(N2)[None, :]) % N2
    c2 = np.cos(2 * np.pi * kk / N2)
    s2 = np.sin(2 * np.pi * kk / N2)
    f2_fwd = np.block([[c2, s2], [-s2, c2]])
    f2_inv = np.block([[c2, -s2], [s2, c2]])
    as_bf = lambda a: jnp.asarray(a.astype(BF16))
    return as_bf(m_fwd), as_bf(m_inv), as_bf(f2_fwd), as_bf(f2_inv)


def _pitch(rows):
    p = rows + SUBLANES
    assert (p // SUBLANES) % 2 == 1
    return p


def _stage3(a_s, f2_ref, k1, H1, pitch):
    N2 = DFT_N2
    ar = a_s[pl.ds(k1, N2, stride=pitch), :]
    ai = a_s[pl.ds(H1 + k1, N2, stride=pitch), :]
    av = jnp.concatenate([ar, ai], axis=0).astype(BF16)
    return _dot(f2_ref[...], av)


def _fwd_body(u_ref, m_ref, f2_ref, x_ref, a_s, *, H1, ks, gw, ng):
    N2, cb = DFT_N2, HY_CB
    pitch = _pitch(2 * H1)
    s = pl.program_id(1)

    @pl.when(s < ng)
    def _():
        for t in range(gw):
            n2 = s * gw + t
            u = u_ref[:, t * SUBLANES:(t + 1) * SUBLANES, :].reshape(H1, cb).astype(BF16)
            a_s[pl.ds(pl.multiple_of(n2 * pitch, SUBLANES), 2 * H1), :] = _dot(m_ref[n2], u)

    @pl.when(s >= ng)
    def _():
        def kbody(kk, carry):
            x = _stage3(a_s, f2_ref, (s - ng) * ks + kk, H1, pitch)
            x_ref[kk, 0] = x[:N2].astype(x_ref.dtype)
            x_ref[kk, 1] = x[N2:].astype(x_ref.dtype)
            return carry
        lax.fori_loop(0, ks, kbody, 0, unroll=DFT_UNROLL)


def _hy_fwd(u_blk, slab_off, nj, consts):
    L = u_blk.shape[1]
    N2, cb = DFT_N2, HY_CB
    H1 = L // N2
    ks = _pick(H1, 16, 1)
    gw = _pick(N2, DFT_GROUP, 1)
    ng, nsl = N2 // gw, H1 // ks
    m_fwd, _, f2_fwd, _ = consts
    assert H1 % SUBLANES == 0
    u4 = u_blk.reshape(u_blk.shape[0], H1 // SUBLANES, TT_CHUNK, cb)
    return pl.pallas_call(
        functools.partial(_fwd_body, H1=H1, ks=ks, gw=gw, ng=ng), grid=(nj, ng + nsl),
        in_specs=[pl.BlockSpec((None, H1 // SUBLANES, gw * SUBLANES, cb),
                               lambda j, s: (slab_off + j, 0, jnp.minimum(s, ng - 1), 0)),
                  pl.BlockSpec((N2, 2 * H1, H1), lambda j, s: (0, 0, 0)),
                  pl.BlockSpec((2 * N2, 2 * N2), lambda j, s: (0, 0))],
        out_specs=pl.BlockSpec((None, ks, 2, N2, cb), lambda j, s: (j, jnp.maximum(s - ng, 0), 0, 0, 0)),
        out_shape=jax.ShapeDtypeStruct((nj, H1, 2, N2, cb), SPEC_DTYPE),
        scratch_shapes=[pltpu.VMEM((N2 * _pitch(2 * H1), cb), F32)],
        compiler_params=_cparams(("parallel", "arbitrary")), name="hy_fwd",
    )(u4, m_fwd, f2_fwd)


def _inv_body(x_ref, g_ref, m_ref, f2_ref, y_ref, b_s, *, H1, ks, gw, nsl):
    N2, cb = DFT_N2, HY_CB
    pitch = _pitch(2 * N2)
    s = pl.program_id(1)

    @pl.when(s < nsl)
    def _():
        def kbody(kk, carry):
            xr, xi = x_ref[kk, 0].astype(F32), x_ref[kk, 1].astype(F32)
            gr, gi = g_ref[kk, 0].astype(F32), g_ref[kk, 1].astype(F32)
            yr = xr * gr - xi * gi
            yi = xr * gi + xi * gr
            b = _dot(f2_ref[...], jnp.concatenate([yr, yi], axis=0).astype(BF16))
            b_s[pl.ds(pl.multiple_of((s * ks + kk) * pitch, SUBLANES), 2 * N2), :] = b
            return carry
        lax.fori_loop(0, ks, kbody, 0, unroll=DFT_UNROLL)

    @pl.when(s >= nsl)
    def _():
        for t in range(gw):
            n2 = (s - nsl) * gw + t
            br = b_s[pl.ds(n2, H1, stride=pitch), :]
            bi = b_s[pl.ds(N2 + n2, H1, stride=pitch), :]
            y = _dot(m_ref[n2], jnp.concatenate([br, bi], axis=0).astype(BF16))
            y_ref[:, t * SUBLANES:(t + 1) * SUBLANES, :] = y.reshape(H1 // SUBLANES, SUBLANES, cb)


def _hy_inv(x, g_all, order, consts, L):
    nj, H1, _, N2, cb = x.shape
    ks = _pick(H1, 16, 1)
    gw = _pick(N2, DFT_GROUP, 1)
    ng, nsl = N2 // gw, H1 // ks
    _, m_inv, _, f2_inv = consts
    y2d = pl.pallas_call(
        functools.partial(_inv_body, H1=H1, ks=ks, gw=gw, nsl=nsl), grid=(nj, nsl + ng),
        in_specs=[pl.BlockSpec((None, ks, 2, N2, cb), lambda j, s: (j, jnp.minimum(s, nsl - 1), 0, 0, 0)),
                  pl.BlockSpec((None, None, ks, 2, N2, cb),
                               lambda j, s: (order, j, jnp.minimum(s, nsl - 1), 0, 0, 0)),
                  pl.BlockSpec((N2, H1, 2 * H1), lambda j, s: (0, 0, 0)),
                  pl.BlockSpec((2 * N2, 2 * N2), lambda j, s: (0, 0))],
        out_specs=pl.BlockSpec((None, H1 // SUBLANES, gw * SUBLANES, cb),
                               lambda j, s: (j, 0, jnp.maximum(s - nsl, 0), 0)),
        out_shape=jax.ShapeDtypeStruct((nj, H1 // SUBLANES, TT_CHUNK, cb), F32),
        scratch_shapes=[pltpu.VMEM((H1 * _pitch(2 * N2), cb), F32)],
        compiler_params=_cparams(("parallel", "arbitrary")), name="hy_inv",
    )(x, g_all, m_inv, f2_inv)
    return y2d.reshape(nj, L, cb)


def _pos_features(L, emb):
    N2 = DFT_N2
    H1 = L // N2
    bands = (emb - 1) // 2
    r = jnp.arange(L)
    lag = (r % H1) * N2 + (r // H1)
    f = jnp.linspace(1e-4, bands - 1, bands, dtype=F32)[None, :]

    def feats(lg):
        lgf = lg.astype(F32)[:, None]
        t = lgf / (L - 1)
        w = 2.0 * math.pi * lgf / L
        z = jnp.concatenate([t, jnp.cos(f * w), -jnp.sin(f * w)], axis=-1)
        return jnp.pad(z, ((0, 0), (0, LANES - emb)))
    return jnp.concatenate([feats(lag), feats(jnp.where(lag == 0, 0, L - lag))], axis=-1)


def _filter_mlp_body(z_ref, w1_ref, b1_ref, w2_ref, b2_ref, w3_ref, b3_ref, fr_ref, o_ref):
    h = jnp.sin(fr_ref[0:1, :] * (_dot_f32(z_ref[...], w1_ref[...]) + b1_ref[...]))
    h = jnp.sin(fr_ref[1:2, :] * (_dot_f32(h, w2_ref[...]) + b2_ref[...]))
    h = jnp.sin(fr_ref[2:3, :] * (_dot_f32(h, w3_ref[...]) + b3_ref[...]))
    o_ref[...] = h.astype(o_ref.dtype)


def _bdiag(w):
    a, b = w.shape
    z = jnp.zeros((a, b), w.dtype)
    return jnp.concatenate([jnp.concatenate([w, z], axis=1), jnp.concatenate([z, w], axis=1)], axis=0)


def _filter_mlp(z2, w1, b1, w2, b2, w3, b3, fr):
    L = z2.shape[0]
    emb, hd = w1.shape
    assert 2 * hd == LANES
    w1p = jnp.zeros((2 * LANES, LANES), F32).at[:emb, :hd].set(w1).at[LANES:LANES + emb, hd:].set(w1)
    tile2 = lambda v: jnp.concatenate([v, v], axis=-1)
    bm = _pick(L, 1024, SUBLANES)
    full = lambda a: pl.BlockSpec(a.shape, lambda i: (0,) * a.ndim)
    args = (w1p, tile2(b1).reshape(1, LANES), _bdiag(w2), tile2(b2).reshape(1, LANES),
            _bdiag(w3), tile2(b3).reshape(1, LANES), tile2(fr))
    return pl.pallas_call(
        _filter_mlp_body, grid=(L // bm,),
        in_specs=[pl.BlockSpec((bm, 2 * LANES), lambda i: (i, 0))] + [full(a) for a in args],
        out_specs=pl.BlockSpec((bm, LANES), lambda i: (i, 0)),
        out_shape=jax.ShapeDtypeStruct((L, LANES), BF16),
        compiler_params=_cparams(("parallel",)), name="hy_filter_mlp",
    )(z2, *args)


def _filt_body(hm_ref, w2_ref, ad_ref, m_ref, f2_ref, g_ref, a_s, g_s, inv_s, *, L, H1, ks):
    N2, cb = DFT_N2, HY_CB
    pitch = _pitch(2 * H1)
    s = pl.program_id(2)

    @pl.when(s == 0)
    def _():
        rate = ad_ref[...] * (1.0 / (L - 1))
        n1 = lax.broadcasted_iota(jnp.int32, (H1, cb), 0)
        sgn = jnp.where(n1 % 2 == 0, 1.0, -1.0)
        e1f = jnp.exp(-(n1 * N2).astype(F32) * rate)
        e1b = jnp.exp(-((H1 - 1 - n1) * N2).astype(F32) * rate)

        def group(gi, nrm):
            def gen(t, nrm):
                n2 = gi * FILT_GROUP + t
                hs = hm_ref[pl.ds(pl.multiple_of(n2 * H1, SUBLANES), H1), :]
                gp = _dot(hs, w2_ref[...])
                n2f = n2.astype(F32)
                gf = gp[:, :cb] * (e1f * jnp.exp(-n2f * rate))
                gb = gp[:, cb:] * (e1b * jnp.exp(-(N2 - n2f) * rate))
                gb = jnp.where(n1 + n2 == 0, 0.0, -gb)
                g_s[t, 0] = gf.astype(BF16)
                g_s[t, 1] = gb.astype(BF16)
                return nrm + (jnp.abs(gf) + jnp.abs(gb))
            nrm = lax.fori_loop(0, FILT_GROUP, gen, nrm, unroll=DFT_UNROLL)

            def xform(t, carry):
                n2 = gi * FILT_GROUP + t
                m = m_ref[n2]
                a1 = _dot(m, g_s[t, 0])
                a2 = _dot(m, g_s[t, 1])
                base = pl.multiple_of(n2 * pitch, SUBLANES)
                a_s[pl.ds(base, H1), :] = a1[:H1] + sgn * a2[H1:]
                a_s[pl.ds(base + H1, H1), :] = a1[H1:] - sgn * a2[:H1]
                return carry
            lax.fori_loop(0, FILT_GROUP, xform, 0, unroll=DFT_UNROLL)
            return nrm
        nrm = lax.fori_loop(0, N2 // FILT_GROUP, group, jnp.zeros((H1, cb), F32))
        inv_s[...] = jnp.broadcast_to(1.0 / jnp.sum(nrm, axis=0, keepdims=True), inv_s.shape)

    scale = inv_s[0:1, :]

    def kbody(kk, carry):
        x = _stage3(a_s, f2_ref, s * ks + kk, H1, pitch) * scale
        g_ref[kk, 0] = x[:N2].astype(g_ref.dtype)
        g_ref[kk, 1] = x[N2:].astype(g_ref.dtype)
        return carry
    lax.fori_loop(0, ks, kbody, 0, unroll=DFT_UNROLL)


def _hy_filters(hm, wout, C, L, consts):
    N2, cb = DFT_N2, HY_CB
    H1 = L // N2
    ks = _pick(H1, 16, 1)
    hd = wout.shape[0]
    nj = C // cb
    m_fwd, _, f2_fwd, _ = consts
    w4 = wout.reshape(hd, 2, 2, nj, cb)
    wf = jnp.transpose(w4[:, :, 0], (1, 2, 0, 3))
    wb = jnp.transpose(w4[:, :, 1], (1, 2, 0, 3))
    z = jnp.zeros_like(wf)
    w2 = jnp.concatenate([jnp.concatenate([wf, z], axis=-1), jnp.concatenate([z, wb], axis=-1)], axis=2)
    w2 = w2.astype(BF16)
    deltas = jnp.linspace(math.log(HY_FAST_DECAY) / HY_DECAY_TARGET,
                          math.log(HY_SLOW_DECAY) / HY_DECAY_TARGET, C, dtype=F32)
    absd = jnp.abs(deltas).reshape(1, C)
    return pl.pallas_call(
        functools.partial(_filt_body, L=L, H1=H1, ks=ks), grid=(2, nj, H1 // ks),
        in_specs=[pl.BlockSpec((L, LANES), lambda o, j, s: (0, 0)),
                  pl.BlockSpec((None, None, 2 * hd, 2 * cb), lambda o, j, s: (o, j, 0, 0)),
                  pl.BlockSpec((1, cb), lambda o, j, s: (0, j)),
                  pl.BlockSpec((N2, 2 * H1, H1), lambda o, j, s: (0, 0, 0)),
                  pl.BlockSpec((2 * N2, 2 * N2), lambda o, j, s: (0, 0))],
        out_specs=pl.BlockSpec((None, None, ks, 2, N2, cb), lambda o, j, s: (o, j, s, 0, 0, 0)),
        out_shape=jax.ShapeDtypeStruct((2, nj, H1, 2, N2, cb), SPEC_DTYPE),
        scratch_shapes=[pltpu.VMEM((N2 * _pitch(2 * H1), cb), F32), pltpu.VMEM((FILT_GROUP, 2, H1, cb), BF16),
                        pltpu.VMEM((SUBLANES, cb), F32)],
        compiler_params=_cparams(("parallel", "parallel", "arbitrary")), name="hy_filters",
    )(hm, w2, absd, m_fwd, f2_fwd)


def _hy_gate1_body(gate_ref, v_ref, y_ref, sk_ref, o_ref):
    o_ref[...] = gate_ref[...] * (y_ref[...] + sk_ref[...] * v_ref[...])


def _hy_gate1(uh_blk, gate_blk, v_blk, y_blk, sk):
    nj, L, cb = y_blk.shape
    bm = _pick(L, 256, SUBLANES)
    blk = lambda g: pl.BlockSpec((nj, bm, cb), lambda i: (g, i, 0))
    return pl.pallas_call(
        _hy_gate1_body, grid=(L // bm,),
        in_specs=[blk(gate_blk), blk(v_blk), blk(0), pl.BlockSpec((nj, 1, cb), lambda i: (0, 0, 0))],
        out_specs=blk(0), out_shape=jax.ShapeDtypeStruct((nj, L, cb), F32),
        compiler_params=_cparams(("parallel",)), name="hy_gate1",
    )(uh_blk, uh_blk, y_blk, sk.reshape(nj, 1, cb))


def _hy_gate2_body(gate_ref, v_ref, y_ref, sk_ref, g_ref, o_ref, *, gw):
    nj, _, cb = gate_ref.shape
    for r in range(SUBLANES):
        rows = pl.ds(r, gw, stride=SUBLANES)
        zs = []
        ss = jnp.zeros((gw, cb), F32)
        for k in range(nj):
            z = gate_ref.at[k][rows, :] * (y_ref.at[k][rows, :] + sk_ref[k] * v_ref.at[k][rows, :])
            zs.append(z)
            ss = ss + z * z
        ms = jnp.sum(ss, axis=-1, keepdims=True) * (1.0 / (nj * cb))
        inv = lax.rsqrt(ms + NORM_EPS)
        for k in range(nj):
            o_ref[r, :, k * cb:(k + 1) * cb] = (zs[k] * inv * g_ref[k]).astype(o_ref.dtype)


def _hy_gate2(uh_blk, gate_blk, v_blk, y_blk, sk, g):
    nj, L, cb = y_blk.shape
    N2 = DFT_N2
    nch = L // TT_CHUNK
    gw = _pick(N2, DFT_GROUP, 16)
    ngr = N2 // gw
    blk = lambda gidx: pl.BlockSpec((nj, gw * SUBLANES, cb), lambda c, q: (gidx, c * ngr + q, 0))
    vec = pl.BlockSpec((nj, 1, cb), lambda c, q: (0, 0, 0))
    out = pl.pallas_call(
        functools.partial(_hy_gate2_body, gw=gw), grid=(nch, ngr),
        in_specs=[blk(gate_blk), blk(0), blk(0), vec, vec],
        out_specs=pl.BlockSpec((None, SUBLANES, gw, nj * cb), lambda c, q: (c, 0, q, 0)),
        out_shape=jax.ShapeDtypeStruct((nch, SUBLANES, N2, nj * cb), BF16),
        compiler_params=_cparams(("parallel", "parallel")), name="hy_gate2",
    )(uh_blk, v_blk, y_blk, sk.reshape(nj, 1, cb), g.reshape(nj, 1, cb))
    return out.reshape(L, nj * cb)


def kernel(x, norm_mix_g, w_in, hy_conv_w, hy_conv_b, hy_pos_w1, hy_pos_b1, hy_pos_w2, hy_pos_b2, hy_pos_w3, hy_pos_b3, hy_sin_freq, hy_pos_wout, hy_skip, hy_out_g, ssm_conv_w, ssm_conv_b, ssm_A_log, ssm_dt_bias, ssm_D, ssm_out_g, w_out, norm_mlp_g, w_up, w_down, final_norm_g):
    B, L, D = x.shape
    depth = w_in.shape[0]
    CH = hy_out_g.shape[1]
    WS = ssm_out_g.shape[1]
    heads = ssm_D.shape[1]
    xbc_w = ssm_conv_w.shape[2]
    groups = (xbc_w - WS) // (2 * SSM_STATE)
    o1 = 3 * CH
    o2 = o1 + WS
    o3 = o2 + xbc_w
    assert WS == heads * SSM_HEAD_DIM and heads <= LANES and L % DFT_N2 == 0 and CH % HY_CB == 0

    consts = _dft_consts(L)
    z2 = _pos_features(L, hy_pos_w1.shape[1])
    w_in_b = w_in.astype(BF16)
    w_in_dt = jnp.pad(w_in[:, :, o3:], ((0, 0), (0, 0), (0, LANES - heads))).astype(BF16)
    w_out_b = w_out.astype(BF16)
    w_up_b = w_up.astype(BF16)
    w_down_b = w_down.astype(BF16)
    pad_h = lambda v: jnp.pad(v, (0, LANES - heads)).reshape(1, LANES)
    d_rep = jnp.repeat(ssm_D, SSM_HEAD_DIM, axis=-1)

    outs = []
    for b in range(B):
        xb = x[b]
        delta = None
        for l in range(depth):
            if delta is None:
                h = _rmsnorm(xb, norm_mix_g[l], BF16)
            else:
                xb, h = _add_rmsnorm(xb, delta, norm_mix_g[l], BF16)
            proj = _matmul(h, w_in_b, l, n_cols=o3)
            dt_raw = _matmul(h, w_in_dt, l, bm_t=2048)

            nj = CH // HY_CB
            uh = _dwconv(proj, 0, o1, hy_conv_w[l], hy_conv_b[l], "none", blocked_out=True)
            hm = _filter_mlp(z2, hy_pos_w1[l], hy_pos_b1[l], hy_pos_w2[l], hy_pos_b2[l],
                             hy_pos_w3[l], hy_pos_b3[l], hy_sin_freq[l])
            g_all = _hy_filters(hm, hy_pos_wout[l], CH, L, consts)
            y0 = _hy_inv(_hy_fwd(uh, 2 * nj, nj, consts), g_all, 0, consts, L)
            z1 = _hy_gate1(uh, 0, 2, y0, hy_skip[l, 0])
            y1 = _hy_inv(_hy_fwd(z1, 0, nj, consts), g_all, 1, consts, L)
            y_hy = _hy_gate2(uh, 1, z1, y1, hy_skip[l, 1], hy_out_g[l])

            xbc = _dwconv(proj, o2, xbc_w, ssm_conv_w[l], ssm_conv_b[l], "silu")
            yf = _ssd(xbc, dt_raw, pad_h(ssm_dt_bias[l, 0]), pad_h(ssm_A_log[l, 0]),
                      reverse=False, heads=heads, groups=groups)
            yb = _ssd(xbc, dt_raw, pad_h(ssm_dt_bias[l, 1]), pad_h(ssm_A_log[l, 1]),
                      reverse=True, heads=heads, groups=groups)
            y_ssm = _ssm_gate(yf, yb, xbc, proj, o1, d_rep[l], ssm_out_g[l], groups)

            xb = _out_matmul(y_hy, y_ssm, w_out_b, l, xb)
            h2 = _rmsnorm(xb, norm_mlp_g[l], BF16)
            delta = _mlp(h2, w_up_b, w_down_b, l)
        outs.append(_add_rmsnorm(xb, delta, final_norm_g, F32, keep_sum=False))
    return jnp.stack(outs, axis=0)
```

```python
import functools
import math

import numpy as np
import jax
import jax.numpy as jnp
from jax import lax
from jax.experimental import pallas as pl
from jax.experimental.pallas import tpu as pltpu

F32 = jnp.float32
BF16 = jnp.bfloat16

NORM_EPS = 1e-5
SSM_HEAD_DIM = 64
SSM_STATE = 128
SSM_CHUNK = 256
HY_FAST_DECAY = 0.3
HY_SLOW_DECAY = 1.5
HY_DECAY_TARGET = 1e-2

LANES = 128
SUBLANES = 8
DFT_N2 = 128
HY_CB = 128
DFT_UNROLL = 16
DFT_GROUP = 32
TT_CHUNK = SUBLANES * DFT_N2
FILT_GROUP = 16
SPEC_DTYPE = BF16
VMEM_LIMIT_MB = 56


def _cparams(sem, vmem_mb=VMEM_LIMIT_MB):
    return pltpu.CompilerParams(dimension_semantics=sem, vmem_limit_bytes=vmem_mb * 1024 * 1024)


def _pick(n, target, mult):
    best = None
    for d in range(mult, min(n, target) + 1, mult):
        if n % d == 0:
            best = d
    assert best is not None, (n, target, mult)
    return best


def _dot(a, b):
    return jnp.dot(a, b, preferred_element_type=F32)


def _split3(a):
    hi = a.astype(BF16)
    r1 = a - hi.astype(F32)
    mid = r1.astype(BF16)
    lo = (r1 - mid.astype(F32)).astype(BF16)
    return hi, mid, lo


def _dot_f32(a, b):
    ah = a.astype(BF16)
    al = (a - ah.astype(F32)).astype(BF16)
    bh = b.astype(BF16)
    bl = (b - bh.astype(F32)).astype(BF16)
    return _dot(ah, bh) + (_dot(ah, bl) + _dot(al, bh))


def _silu(x):
    return x / (1.0 + jnp.exp(-x))


def _rmsnorm_body(x_ref, g_ref, h_ref):
    x = x_ref[...]
    ms = jnp.mean(x * x, axis=-1, keepdims=True)
    h_ref[...] = (x * lax.rsqrt(ms + NORM_EPS) * g_ref[...]).astype(h_ref.dtype)


def _rmsnorm(x, g, out_dtype):
    M, D = x.shape
    bm = _pick(M, 256, SUBLANES)
    return pl.pallas_call(
        _rmsnorm_body, grid=(M // bm,),
        in_specs=[pl.BlockSpec((bm, D), lambda i: (i, 0)), pl.BlockSpec((1, D), lambda i: (0, 0))],
        out_specs=pl.BlockSpec((bm, D), lambda i: (i, 0)),
        out_shape=jax.ShapeDtypeStruct((M, D), out_dtype),
        compiler_params=_cparams(("parallel",)), name="rmsnorm",
    )(x, g.reshape(1, D))


def _add_rmsnorm_body(x_ref, d_ref, g_ref, *out_refs):
    x = x_ref[...] + d_ref[...]
    if len(out_refs) == 2:
        out_refs[0][...] = x
    h_ref = out_refs[-1]
    ms = jnp.mean(x * x, axis=-1, keepdims=True)
    h_ref[...] = (x * lax.rsqrt(ms + NORM_EPS) * g_ref[...]).astype(h_ref.dtype)


def _add_rmsnorm(x, delta, g, out_dtype, keep_sum=True):
    M, D = x.shape
    bm = _pick(M, 256, SUBLANES)
    row = pl.BlockSpec((bm, D), lambda i: (i, 0))
    h_shape = jax.ShapeDtypeStruct((M, D), out_dtype)
    return pl.pallas_call(
        _add_rmsnorm_body, grid=(M // bm,),
        in_specs=[row, row, pl.BlockSpec((1, D), lambda i: (0, 0))],
        out_specs=[row, row] if keep_sum else row,
        out_shape=[jax.ShapeDtypeStruct((M, D), F32), h_shape] if keep_sum else h_shape,
        compiler_params=_cparams(("parallel",)), name="add_rmsnorm",
    )(x, delta, g.reshape(1, D))


def _mm_body(a_ref, b_ref, o_ref):
    o_ref[...] = _dot(a_ref[...], b_ref[...]).astype(o_ref.dtype)


def _matmul(a, b_stack, layer, n_cols=None, out_dtype=F32, bm_t=1024, bn_t=1024):
    M, K = a.shape
    N = b_stack.shape[2] if n_cols is None else n_cols
    bm = _pick(M, bm_t, SUBLANES)
    bn = _pick(N, bn_t, LANES)
    return pl.pallas_call(
        _mm_body, grid=(M // bm, N // bn),
        in_specs=[pl.BlockSpec((bm, K), lambda i, j: (i, 0)),
                  pl.BlockSpec((None, K, bn), lambda i, j: (layer, 0, j))],
        out_specs=pl.BlockSpec((bm, bn), lambda i, j: (i, j)),
        out_shape=jax.ShapeDtypeStruct((M, N), out_dtype),
        compiler_params=_cparams(("parallel", "parallel")), name="proj_matmul",
    )(a, b_stack)


def _out_mm_body(a1_ref, a2_ref, b1_ref, b2_ref, r_ref, o_ref):
    acc = _dot(a1_ref[...], b1_ref[...]) + _dot(a2_ref[...], b2_ref[...])
    o_ref[...] = r_ref[...] + acc


def _out_matmul(a1, a2, b_stack, layer, res):
    M, K1 = a1.shape
    assert a2.shape[1] == K1 and b_stack.shape[1] == 2 * K1
    N = b_stack.shape[2]
    bm = _pick(M, 1024, SUBLANES)
    bn = _pick(N, 1024, LANES)
    return pl.pallas_call(
        _out_mm_body, grid=(M // bm, N // bn),
        in_specs=[pl.BlockSpec((bm, K1), lambda i, j: (i, 0)), pl.BlockSpec((bm, K1), lambda i, j: (i, 0)),
                  pl.BlockSpec((None, K1, bn), lambda i, j: (layer, 0, j)),
                  pl.BlockSpec((None, K1, bn), lambda i, j: (layer, 1, j)),
                  pl.BlockSpec((bm, bn), lambda i, j: (i, j))],
        out_specs=pl.BlockSpec((bm, bn), lambda i, j: (i, j)),
        out_shape=jax.ShapeDtypeStruct((M, N), F32),
        compiler_params=_cparams(("parallel", "parallel")), name="out_matmul",
    )(a1, a2, b_stack, b_stack, res)


def _mlp_body(h_ref, wu_ref, wd_ref, o_ref, *, cw):
    j = pl.program_id(1)

    @pl.when(j == 0)
    def _():
        o_ref[...] = jnp.zeros_like(o_ref)

    hw = wu_ref.shape[1] // 2
    ts = []
    for q in range(2):
        t = _dot(h_ref[...], wu_ref[:, q * hw:(q + 1) * hw])
        ts.append(jnp.square(jnp.maximum(t, 0.0)).astype(BF16))
    for c in range(o_ref.shape[1] // cw):
        cols = slice(c * cw, (c + 1) * cw)
        o_ref[:, cols] += _dot(ts[0], wd_ref[0:hw, cols]) + _dot(ts[1], wd_ref[hw:2 * hw, cols])


def _mlp(h, w_up_stack, w_down_stack, layer):
    M, D = h.shape
    F = w_up_stack.shape[2]
    bm = _pick(M, 512, SUBLANES)
    bf = _pick(F, 512, LANES)
    cw = _pick(D, 512, LANES)
    return pl.pallas_call(
        functools.partial(_mlp_body, cw=cw), grid=(M // bm, F // bf),
        in_specs=[pl.BlockSpec((bm, D), lambda i, j: (i, 0)),
                  pl.BlockSpec((None, D, bf), lambda i, j: (layer, 0, j)),
                  pl.BlockSpec((None, bf, D), lambda i, j: (layer, j, 0))],
        out_specs=pl.BlockSpec((bm, D), lambda i, j: (i, 0)),
        out_shape=jax.ShapeDtypeStruct((M, D), F32),
        compiler_params=_cparams(("parallel", "arbitrary")), name="mlp",
    )(h, w_up_stack, w_down_stack)


def _dwconv_body(p_ref, pv_ref, nx_ref, w_ref, b_ref, o_ref, *, act):
    i = pl.program_id(0)
    last = pl.num_programs(0) - 1
    u = p_ref[...]
    bm = u.shape[0]
    prev_row = jnp.where(i > 0, pv_ref[SUBLANES - 1:SUBLANES, :], 0.0)
    next_row = jnp.where(i < last, nx_ref[0:1, :], 0.0)
    rows = lax.broadcasted_iota(jnp.int32, u.shape, 0)
    um = jnp.where(rows == 0, prev_row, pltpu.roll(u, 1, 0))
    up = jnp.where(rows == bm - 1, next_row, pltpu.roll(u, bm - 1, 0))
    y = w_ref[0:1, :] * um
    y = y + w_ref[1:2, :] * u
    y = y + w_ref[2:3, :] * up
    y = y + b_ref[...]
    if act == "silu":
        y = _silu(y)
    if len(o_ref.shape) == 3:
        assert bm == TT_CHUNK
        for k in range(o_ref.shape[0]):
            for r in range(SUBLANES):
                o_ref.at[k][pl.ds(r, DFT_N2, stride=SUBLANES), :] = (
                    y[r * DFT_N2:(r + 1) * DFT_N2, k * LANES:(k + 1) * LANES])
    else:
        o_ref[...] = y


def _dwconv(p, col_off, C, w, b, act, blocked_out=False):
    L = p.shape[0]
    bm = TT_CHUNK if blocked_out else _pick(L, 1024, SUBLANES)
    assert L % bm == 0
    bc = _pick(math.gcd(C, col_off) if col_off else C, 1024, LANES)
    co = col_off // bc
    hb = bm // SUBLANES
    nrb = L // SUBLANES
    if blocked_out:
        spb = bc // LANES
        out_spec = pl.BlockSpec((spb, bm, LANES), lambda i, j: (j, i, 0))
        out_shape = jax.ShapeDtypeStruct((C // LANES, L, LANES), F32)
    else:
        out_spec = pl.BlockSpec((bm, bc), lambda i, j: (i, j))
        out_shape = jax.ShapeDtypeStruct((L, C), F32)
    return pl.pallas_call(
        functools.partial(_dwconv_body, act=act), grid=(L // bm, C // bc),
        in_specs=[pl.BlockSpec((bm, bc), lambda i, j: (i, co + j)),
                  pl.BlockSpec((SUBLANES, bc), lambda i, j: (jnp.maximum(i * hb - 1, 0), co + j)),
                  pl.BlockSpec((SUBLANES, bc), lambda i, j: (jnp.minimum((i + 1) * hb, nrb - 1), co + j)),
                  pl.BlockSpec((3, bc), lambda i, j: (0, j)),
                  pl.BlockSpec((1, bc), lambda i, j: (0, j))],
        out_specs=out_spec, out_shape=out_shape,
        compiler_params=_cparams(("parallel", "parallel")), name="dwconv_" + act,
    )(p, p, p, w, b.reshape(1, C))


def _expand_heads(v, ex_ref):
    hi = v.astype(BF16)
    lo = (v - hi.astype(F32)).astype(BF16)
    return _dot(hi, ex_ref[...]) + _dot(lo, ex_ref[...])


def _ssd_body(xbc_ref, dt_ref, bias_ref, alog_ref, ex_ref, y_ref, s_ref, *, reverse, heads, groups):
    Q = SSM_CHUNK
    P = SSM_HEAD_DIM
    N = SSM_STATE
    R = heads // groups
    W = heads * P
    step = pl.program_id(0)

    @pl.when(step == 0)
    def _():
        s_ref[...] = jnp.zeros_like(s_ref)

    x = dt_ref[...] + bias_ref[...]
    dt = jnp.maximum(x, 0.0) + jnp.log1p(jnp.exp(-jnp.abs(x)))
    a = dt * (-jnp.exp(alog_ref[...]))
    ri = lax.broadcasted_iota(jnp.int32, (Q, Q), 0)
    ci = lax.broadcasted_iota(jnp.int32, (Q, Q), 1)
    mask = (ci >= ri) if reverse else (ri >= ci)
    tri = jnp.where(mask, 1.0, 0.0).astype(BF16)
    ah, am, al = _split3(a)
    acs = _dot(tri, ah) + (_dot(tri, am) + _dot(tri, al))
    tot = acs[0:1, :] if reverse else acs[Q - 1:Q, :]
    e_all = _expand_heads(jnp.exp(acs), ex_ref)
    w_all = _expand_heads(jnp.exp(tot - acs) * dt, ex_ref)
    t_all = _expand_heads(jnp.broadcast_to(jnp.exp(tot), (SUBLANES, LANES)), ex_ref)[0:1, :]
    acs_row = acs.T
    dt_row = dt.T

    load_b = lambda g: xbc_ref[:, W + g * N:W + (g + 1) * N].astype(BF16)
    load_c = lambda g: xbc_ref[:, W + groups * N + g * N:W + groups * N + (g + 1) * N].astype(BF16)
    load_x = lambda g: xbc_ref[:, g * R * P:(g + 1) * R * P]
    lane = lax.broadcasted_iota(jnp.int32, (Q, R * P), 1)

    cbs, y_offs = [], []
    for g in range(groups):
        cg = load_c(g)
        cbs.append(lax.dot_general(cg, load_b(g), (((1,), (1,)), ((), ())), preferred_element_type=F32))
        y_offs.append(_dot(cg, s_ref[g].astype(BF16)))
    for g in range(groups):
        gl = slice(g * R * P, (g + 1) * R * P)
        s_new = lax.dot_general(load_b(g), (load_x(g) * w_all[:, gl]).astype(BF16), (((0,), (0,)), ((), ())),
                                preferred_element_type=F32)
        s_ref[g] = s_ref[g] * t_all[:, gl] + s_new
    for g in range(groups):
        xg = load_x(g)
        ws, xs = [], []
        for r in range(R):
            h = g * R + r
            diff = acs[:, h:h + 1] - acs_row[h:h + 1, :]
            lm = jnp.exp(jnp.where(mask, diff, -jnp.inf))
            ws.append((cbs[g] * lm * dt_row[h:h + 1, :]).astype(BF16))
            xs.append(jnp.where((lane >= r * P) & (lane < (r + 1) * P), xg, 0.0).astype(BF16))
        y_diag = _dot(jnp.concatenate(ws, axis=1), jnp.concatenate(xs, axis=0))
        gl = slice(g * R * P, (g + 1) * R * P)
        y_ref[:, gl] = y_diag + y_offs[g] * e_all[:, gl]


def _ssd(xbc, dt_raw, dt_bias, a_log, *, reverse, heads, groups):
    L = xbc.shape[0]
    Q = SSM_CHUNK
    assert L % Q == 0
    nc = L // Q
    W = heads * SSM_HEAD_DIM
    R = heads // groups
    cmap = (lambda i: (nc - 1 - i, 0)) if reverse else (lambda i: (i, 0))
    ex = np.zeros((LANES, W), np.float32)
    ex[np.arange(W) // SSM_HEAD_DIM, np.arange(W)] = 1.0
    return pl.pallas_call(
        functools.partial(_ssd_body, reverse=reverse, heads=heads, groups=groups), grid=(nc,),
        in_specs=[pl.BlockSpec((Q, xbc.shape[1]), cmap), pl.BlockSpec((Q, LANES), cmap),
                  pl.BlockSpec((1, LANES), lambda i: (0, 0)), pl.BlockSpec((1, LANES), lambda i: (0, 0)),
                  pl.BlockSpec((LANES, W), lambda i: (0, 0))],
        out_specs=pl.BlockSpec((Q, W), cmap),
        out_shape=jax.ShapeDtypeStruct((L, W), F32),
        scratch_shapes=[pltpu.VMEM((groups, SSM_STATE, R * SSM_HEAD_DIM), F32)],
        compiler_params=_cparams(("arbitrary",)), name="ssd_bwd" if reverse else "ssd_fwd",
    )(xbc, dt_raw, dt_bias, a_log, jnp.asarray(ex.astype(BF16)))


def _ssm_gate_body(yf_ref, yb_ref, xh_ref, z_ref, d_ref, g_ref, o_ref, *, groups):
    y = (yf_ref[...] + yb_ref[...] + d_ref[...] * xh_ref[...]) * _silu(z_ref[...])
    W = y.shape[1]
    gw = W // groups
    for g in range(groups):
        yg = y[:, g * gw:(g + 1) * gw]
        ms = jnp.mean(yg * yg, axis=-1, keepdims=True)
        o_ref[:, g * gw:(g + 1) * gw] = (yg * lax.rsqrt(ms + NORM_EPS) * g_ref[:, g * gw:(g + 1) * gw]).astype(o_ref.dtype)


def _ssm_gate(yf, yb, xbc, proj, z_off, d_rep, norm_g, groups):
    L, W = yf.shape
    bm = _pick(L, 256, SUBLANES)
    assert z_off % W == 0
    zo = z_off // W
    row = pl.BlockSpec((bm, W), lambda i: (i, 0))
    vec = pl.BlockSpec((1, W), lambda i: (0, 0))
    return pl.pallas_call(
        functools.partial(_ssm_gate_body, groups=groups), grid=(L // bm,),
        in_specs=[row, row, row, pl.BlockSpec((bm, W), lambda i: (i, zo)), vec, vec],
        out_specs=row, out_shape=jax.ShapeDtypeStruct((L, W), BF16),
        compiler_params=_cparams(("parallel",)), name="ssm_gate",
    )(yf, yb, xbc, proj, d_rep.reshape(1, W), norm_g.reshape(1, W))


def _dft_consts(L):
    N2 = DFT_N2
    N = 2 * L
    H1 = L // N2
    n2 = np.arange(N2)[:, None, None]
    k1 = np.arange(H1)[None, :, None]
    n1 = np.arange(H1)[None, None, :]
    m = ((2 * k1 + 1) * (N2 * n1 + n2)) % (2 * N)
    ph = np.pi * m.astype(np.float64) / N
    m_fwd = np.concatenate([np.cos(ph), -np.sin(ph)], axis=1)
    m_inv = (2.0 / N) * np.concatenate([np.cos(ph), -np.sin(ph)], axis=1).transpose(0, 2, 1)
    sg = np.where(np.arange(H1) % 2 == 0, 1.0, -1.0)[None, :, None]
    m_filt = np.concatenate([np.concatenate([np.cos(ph), -sg * np.sin(ph)], axis=2),
                             np.concatenate([-np.sin(ph), -sg * np.cos(ph)], axis=2)], axis=1)
    kk = (np.arange(N2)[:, None] * np.arange(N2)[None, :]) % N2
    c2 = np.cos(2 * np.pi * kk / N2)
    s2 = np.sin(2 * np.pi * kk / N2)
    f2_fwd = np.block([[c2, s2], [-s2, c2]])
    f2_inv = np.block([[c2, -s2], [s2, c2]])
    as_bf = lambda a: jnp.asarray(a.astype(BF16))
    return as_bf(m_fwd), as_bf(m_inv), as_bf(f2_fwd), as_bf(f2_inv), as_bf(m_filt)


def _pitch(rows):
    p = rows + SUBLANES
    assert (p // SUBLANES) % 2 == 1
    return p


def _pack2(a, b):
    ha = lax.bitcast_convert_type(a.astype(BF16).astype(F32), jnp.uint32)
    hb = lax.bitcast_convert_type(b.astype(BF16).astype(F32), jnp.uint32)
    return ha | (hb >> 16)


def _unpack2(w):
    a = lax.bitcast_convert_type(w & jnp.uint32(0xFFFF0000), F32)
    b = lax.bitcast_convert_type(w << 16, F32)
    return jnp.concatenate([a, b], axis=1).astype(BF16)


def _split_lanes(x):
    c = x.shape[1] // 2
    return x[:, :c], x[:, c:]


def _stage3(a_s, f2_ref, k1, H1, pitch):
    N2 = DFT_N2
    ar = _unpack2(a_s[pl.ds(k1, N2, stride=pitch), :])
    ai = _unpack2(a_s[pl.ds(H1 + k1, N2, stride=pitch), :])
    return _dot(f2_ref[...], jnp.concatenate([ar, ai], axis=0))


def _fwd_body(u_ref, m_ref, f2_ref, x_ref, a_s, *, H1, ks, gw, ng):
    N2, cb = DFT_N2, HY_CB
    pitch = _pitch(2 * H1)
    s = pl.program_id(1)

    @pl.when(s < ng)
    def _():
        for t in range(gw):
            n2 = s * gw + t
            rows = slice(t * SUBLANES, (t + 1) * SUBLANES)
            u = jnp.concatenate([u_ref[0, :, rows, :].reshape(H1, cb), u_ref[1, :, rows, :].reshape(H1, cb)],
                                axis=1).astype(BF16)
            a = _dot(m_ref[n2], u)
            a_s[pl.ds(pl.multiple_of(n2 * pitch, SUBLANES), 2 * H1), :] = _pack2(*_split_lanes(a))

    @pl.when(s >= ng)
    def _():
        def kbody(kk, carry):
            x = _stage3(a_s, f2_ref, (s - ng) * ks + kk, H1, pitch).astype(x_ref.dtype)
            for p, xp in enumerate(_split_lanes(x)):
                x_ref[p, kk, 0] = xp[:N2]
                x_ref[p, kk, 1] = xp[N2:]
            return carry
        lax.fori_loop(0, ks, kbody, 0, unroll=DFT_UNROLL)


def _hy_fwd(u_blk, slab_off, nj, consts):
    L = u_blk.shape[1]
    N2, cb = DFT_N2, HY_CB
    H1 = L // N2
    ks = _pick(H1, 16, 1)
    gw = _pick(N2, DFT_GROUP, 1)
    ng, nsl = N2 // gw, H1 // ks
    m_fwd, _, f2_fwd, _, _ = consts
    assert H1 % SUBLANES == 0 and nj % 2 == 0 and slab_off % 2 == 0
    u4 = u_blk.reshape(u_blk.shape[0], H1 // SUBLANES, TT_CHUNK, cb)
    return pl.pallas_call(
        functools.partial(_fwd_body, H1=H1, ks=ks, gw=gw, ng=ng), grid=(nj // 2, ng + nsl),
        in_specs=[pl.BlockSpec((2, H1 // SUBLANES, gw * SUBLANES, cb),
                               lambda j, s: (slab_off // 2 + j, 0, jnp.minimum(s, ng - 1), 0)),
                  pl.BlockSpec((N2, 2 * H1, H1), lambda j, s: (0, 0, 0)),
                  pl.BlockSpec((2 * N2, 2 * N2), lambda j, s: (0, 0))],
        out_specs=pl.BlockSpec((2, ks, 2, N2, cb), lambda j, s: (j, jnp.maximum(s - ng, 0), 0, 0, 0)),
        out_shape=jax.ShapeDtypeStruct((nj, H1, 2, N2, cb), SPEC_DTYPE),
        scratch_shapes=[pltpu.VMEM((N2 * _pitch(2 * H1), cb), jnp.uint32)],
        compiler_params=_cparams(("parallel", "arbitrary")), name="hy_fwd",
    )(u4, m_fwd, f2_fwd)


def _inv_body(x_ref, g_ref, m_ref, f2_ref, y_ref, b_s, *, H1, ks, gw, nsl):
    N2, cb = DFT_N2, HY_CB
    pitch = _pitch(2 * N2)
    s = pl.program_id(1)

    @pl.when(s < nsl)
    def _():
        def kbody(kk, carry):
            pair = lambda ref, c: jnp.concatenate([ref[0, kk, c], ref[1, kk, c]], axis=1)
            xr, xi = pair(x_ref, 0), pair(x_ref, 1)
            gr, gi = pair(g_ref, 0), pair(g_ref, 1)
            yr = xr * gr - xi * gi
            yi = xr * gi + xi * gr
            b = _dot(f2_ref[...], jnp.concatenate([yr, yi], axis=0).astype(BF16))
            b_s[pl.ds(pl.multiple_of((s * ks + kk) * pitch, SUBLANES), 2 * N2), :] = _pack2(*_split_lanes(b))
            return carry
        lax.fori_loop(0, ks, kbody, 0, unroll=DFT_UNROLL)

    @pl.when(s >= nsl)
    def _():
        for t in range(gw):
            n2 = (s - nsl) * gw + t
            br = _unpack2(b_s[pl.ds(n2, H1, stride=pitch), :])
            bi = _unpack2(b_s[pl.ds(N2 + n2, H1, stride=pitch), :])
            y = _dot(m_ref[n2], jnp.concatenate([br, bi], axis=0))
            for p, yp in enumerate(_split_lanes(y)):
                y_ref[p, :, t * SUBLANES:(t + 1) * SUBLANES, :] = yp.reshape(H1 // SUBLANES, SUBLANES, cb)


def _hy_inv(x, g_all, order, consts, L):
    nj, H1, _, N2, cb = x.shape
    ks = _pick(H1, 16, 1)
    gw = _pick(N2, DFT_GROUP, 1)
    ng, nsl = N2 // gw, H1 // ks
    _, m_inv, _, f2_inv, _ = consts
    y4 = pl.pallas_call(
        functools.partial(_inv_body, H1=H1, ks=ks, gw=gw, nsl=nsl), grid=(nj // 2, nsl + ng),
        in_specs=[pl.BlockSpec((2, ks, 2, N2, cb), lambda j, s: (j, jnp.minimum(s, nsl - 1), 0, 0, 0)),
                  pl.BlockSpec((None, 2, ks, 2, N2, cb),
                               lambda j, s: (order, j, jnp.minimum(s, nsl - 1), 0, 0, 0)),
                  pl.BlockSpec((N2, H1, 2 * H1), lambda j, s: (0, 0, 0)),
                  pl.BlockSpec((2 * N2, 2 * N2), lambda j, s: (0, 0))],
        out_specs=pl.BlockSpec((2, H1 // SUBLANES, gw * SUBLANES, cb),
                               lambda j, s: (j, 0, jnp.maximum(s - nsl, 0), 0)),
        out_shape=jax.ShapeDtypeStruct((nj, H1 // SUBLANES, TT_CHUNK, cb), F32),
        scratch_shapes=[pltpu.VMEM((H1 * _pitch(2 * N2), cb), jnp.uint32)],
        compiler_params=_cparams(("parallel", "arbitrary")), name="hy_inv",
    )(x, g_all, m_inv, f2_inv)
    return y4.reshape(nj, L, cb)


def _pos_features(L, emb):
    N2 = DFT_N2
    H1 = L // N2
    bands = (emb - 1) // 2
    r = jnp.arange(L)
    lag = (r % H1) * N2 + (r // H1)
    f = jnp.linspace(1e-4, bands - 1, bands, dtype=F32)[None, :]

    def feats(lg):
        lgf = lg.astype(F32)[:, None]
        t = lgf / (L - 1)
        w = 2.0 * math.pi * lgf / L
        z = jnp.concatenate([t, jnp.cos(f * w), -jnp.sin(f * w)], axis=-1)
        return jnp.pad(z, ((0, 0), (0, LANES - emb)))
    return jnp.concatenate([feats(lag), feats(jnp.where(lag == 0, 0, L - lag))], axis=-1)


def _filter_mlp_body(z_ref, w1_ref, b1_ref, w2_ref, b2_ref, w3_ref, b3_ref, fr_ref, o_ref):
    h = jnp.sin(fr_ref[0:1, :] * (_dot_f32(z_ref[...], w1_ref[...]) + b1_ref[...]))
    h = jnp.sin(fr_ref[1:2, :] * (_dot_f32(h, w2_ref[...]) + b2_ref[...]))
    h = jnp.sin(fr_ref[2:3, :] * (_dot_f32(h, w3_ref[...]) + b3_ref[...]))
    o_ref[...] = h.astype(o_ref.dtype)


def _bdiag(w):
    a, b = w.shape
    z = jnp.zeros((a, b), w.dtype)
    return jnp.concatenate([jnp.concatenate([w, z], axis=1), jnp.concatenate([z, w], axis=1)], axis=0)


def _filter_mlp(z2, w1, b1, w2, b2, w3, b3, fr):
    L = z2.shape[0]
    emb, hd = w1.shape
    assert 2 * hd == LANES
    w1p = jnp.zeros((2 * LANES, LANES), F32).at[:emb, :hd].set(w1).at[LANES:LANES + emb, hd:].set(w1)
    tile2 = lambda v: jnp.concatenate([v, v], axis=-1)
    bm = _pick(L, 1024, SUBLANES)
    full = lambda a: pl.BlockSpec(a.shape, lambda i: (0,) * a.ndim)
    args = (w1p, tile2(b1).reshape(1, LANES), _bdiag(w2), tile2(b2).reshape(1, LANES),
            _bdiag(w3), tile2(b3).reshape(1, LANES), tile2(fr))
    return pl.pallas_call(
        _filter_mlp_body, grid=(L // bm,),
        in_specs=[pl.BlockSpec((bm, 2 * LANES), lambda i: (i, 0))] + [full(a) for a in args],
        out_specs=pl.BlockSpec((bm, LANES), lambda i: (i, 0)),
        out_shape=jax.ShapeDtypeStruct((L, LANES), BF16),
        compiler_params=_cparams(("parallel",)), name="hy_filter_mlp",
    )(z2, *args)


def _filt_body(hm_ref, w2_ref, ad_ref, m_ref, f2_ref, g_ref, a_s, g_s, inv_s, *, L, H1, ks):
    N2 = DFT_N2
    cb = 2 * HY_CB
    pitch = _pitch(2 * H1)
    s = pl.program_id(2)

    @pl.when(s == 0)
    def _():
        rate = ad_ref[...] * (1.0 / (L - 1))
        n1 = lax.broadcasted_iota(jnp.int32, (H1, cb), 0)
        e1f = jnp.exp(-(n1 * N2).astype(F32) * rate)
        e1b = jnp.exp(-((H1 - 1 - n1) * N2).astype(F32) * rate)

        def group(gi, nrm):
            def gen(t, nrm):
                n2 = gi * FILT_GROUP + t
                hs = hm_ref[pl.ds(pl.multiple_of(n2 * H1, SUBLANES), H1), :]
                gp = _dot(hs, w2_ref[...])
                n2f = n2.astype(F32)
                gf = gp[:, :cb] * (e1f * jnp.exp(-n2f * rate))
                gb = gp[:, cb:] * (e1b * jnp.exp(-(N2 - n2f) * rate))
                gb = jnp.where(n1 + n2 == 0, 0.0, -gb)
                g_s[t, 0:H1, :] = gf.astype(BF16)
                g_s[t, H1:2 * H1, :] = gb.astype(BF16)
                return nrm + (jnp.abs(gf) + jnp.abs(gb))
            nrm = lax.fori_loop(0, FILT_GROUP, gen, nrm, unroll=DFT_UNROLL)

            def xform(t, carry):
                n2 = gi * FILT_GROUP + t
                a = _dot(m_ref[n2], g_s[t])
                a_s[pl.ds(pl.multiple_of(n2 * pitch, SUBLANES), 2 * H1), :] = _pack2(*_split_lanes(a))
                return carry
            lax.fori_loop(0, FILT_GROUP, xform, 0, unroll=DFT_UNROLL)
            return nrm
        nrm = lax.fori_loop(0, N2 // FILT_GROUP, group, jnp.zeros((H1, cb), F32))
        inv_s[...] = jnp.broadcast_to(1.0 / jnp.sum(nrm, axis=0, keepdims=True), inv_s.shape)

    scale = inv_s[0:1, :]

    def kbody(kk, carry):
        x = (_stage3(a_s, f2_ref, s * ks + kk, H1, pitch) * scale).astype(g_ref.dtype)
        for p, xp in enumerate(_split_lanes(x)):
            g_ref[p, kk, 0] = xp[:N2]
            g_ref[p, kk, 1] = xp[N2:]
        return carry
    lax.fori_loop(0, ks, kbody, 0, unroll=DFT_UNROLL)


def _hy_filters(hm, wout, C, L, consts):
    N2, cb = DFT_N2, HY_CB
    H1 = L // N2
    ks = _pick(H1, 16, 1)
    hd = wout.shape[0]
    nj = C // cb
    _, _, f2_fwd, _, m_filt = consts
    assert nj % 2 == 0
    pw = 2 * cb
    w4 = wout.reshape(hd, 2, 2, nj // 2, pw)
    wf = jnp.transpose(w4[:, :, 0], (1, 2, 0, 3))
    wb = jnp.transpose(w4[:, :, 1], (1, 2, 0, 3))
    z = jnp.zeros_like(wf)
    w2 = jnp.concatenate([jnp.concatenate([wf, z], axis=-1), jnp.concatenate([z, wb], axis=-1)], axis=2)
    w2 = w2.astype(BF16)
    deltas = jnp.linspace(math.log(HY_FAST_DECAY) / HY_DECAY_TARGET,
                          math.log(HY_SLOW_DECAY) / HY_DECAY_TARGET, C, dtype=F32)
    absd = jnp.abs(deltas).reshape(1, C)
    return pl.pallas_call(
        functools.partial(_filt_body, L=L, H1=H1, ks=ks), grid=(2, nj // 2, H1 // ks),
        in_specs=[pl.BlockSpec((L, LANES), lambda o, j, s: (0, 0)),
                  pl.BlockSpec((None, None, 2 * hd, 2 * pw), lambda o, j, s: (o, j, 0, 0)),
                  pl.BlockSpec((1, pw), lambda o, j, s: (0, j)),
                  pl.BlockSpec((N2, 2 * H1, 2 * H1), lambda o, j, s: (0, 0, 0), pipeline_mode=pl.Buffered(1)),
                  pl.BlockSpec((2 * N2, 2 * N2), lambda o, j, s: (0, 0))],
        out_specs=pl.BlockSpec((None, 2, ks, 2, N2, cb), lambda o, j, s: (o, j, s, 0, 0, 0)),
        out_shape=jax.ShapeDtypeStruct((2, nj, H1, 2, N2, cb), SPEC_DTYPE),
        scratch_shapes=[pltpu.VMEM((N2 * _pitch(2 * H1), cb), jnp.uint32),
                        pltpu.VMEM((FILT_GROUP, 2 * H1, pw), BF16), pltpu.VMEM((SUBLANES, pw), F32)],
        compiler_params=_cparams(("parallel", "parallel", "arbitrary")), name="hy_filters",
    )(hm, w2, absd, m_filt, f2_fwd)


def _hy_gate1_body(gate_ref, v_ref, y_ref, sk_ref, o_ref):
    o_ref[...] = gate_ref[...] * (y_ref[...] + sk_ref[...] * v_ref[...])


def _hy_gate1(uh_blk, gate_blk, v_blk, y_blk, sk):
    nj, L, cb = y_blk.shape
    bm = _pick(L, 256, SUBLANES)
    blk = lambda g: pl.BlockSpec((nj, bm, cb), lambda i: (g, i, 0))
    return pl.pallas_call(
        _hy_gate1_body, grid=(L // bm,),
        in_specs=[blk(gate_blk), blk(v_blk), blk(0), pl.BlockSpec((nj, 1, cb), lambda i: (0, 0, 0))],
        out_specs=blk(0), out_shape=jax.ShapeDtypeStruct((nj, L, cb), F32),
        compiler_params=_cparams(("parallel",)), name="hy_gate1",
    )(uh_blk, uh_blk, y_blk, sk.reshape(nj, 1, cb))


def _hy_gate2_body(gate_ref, v_ref, y_ref, sk_ref, g_ref, o_ref, *, gw):
    nj, _, cb = gate_ref.shape
    for r in range(SUBLANES):
        rows = pl.ds(r, gw, stride=SUBLANES)
        zs = []
        ss = jnp.zeros((gw, cb), F32)
        for k in range(nj):
            z = gate_ref.at[k][rows, :] * (y_ref.at[k][rows, :] + sk_ref[k] * v_ref.at[k][rows, :])
            zs.append(z)
            ss = ss + z * z
        ms = jnp.sum(ss, axis=-1, keepdims=True) * (1.0 / (nj * cb))
        inv = lax.rsqrt(ms + NORM_EPS)
        for k in range(nj):
            o_ref[r, :, k * cb:(k + 1) * cb] = (zs[k] * inv * g_ref[k]).astype(o_ref.dtype)


def _hy_gate2(uh_blk, gate_blk, v_blk, y_blk, sk, g):
    nj, L, cb = y_blk.shape
    N2 = DFT_N2
    nch = L // TT_CHUNK
    gw = _pick(N2, DFT_GROUP, 16)
    ngr = N2 // gw
    blk = lambda gidx: pl.BlockSpec((nj, gw * SUBLANES, cb), lambda c, q: (gidx, c * ngr + q, 0))
    vec = pl.BlockSpec((nj, 1, cb), lambda c, q: (0, 0, 0))
    out = pl.pallas_call(
        functools.partial(_hy_gate2_body, gw=gw), grid=(nch, ngr),
        in_specs=[blk(gate_blk), blk(0), blk(0), vec, vec],
        out_specs=pl.BlockSpec((None, SUBLANES, gw, nj * cb), lambda c, q: (c, 0, q, 0)),
        out_shape=jax.ShapeDtypeStruct((nch, SUBLANES, N2, nj * cb), BF16),
        compiler_params=_cparams(("parallel", "parallel")), name="hy_gate2",
    )(uh_blk, v_blk, y_blk, sk.reshape(nj, 1, cb), g.reshape(nj, 1, cb))
    return out.reshape(L, nj * cb)


def kernel(x, norm_mix_g, w_in, hy_conv_w, hy_conv_b, hy_pos_w1, hy_pos_b1, hy_pos_w2, hy_pos_b2, hy_pos_w3, hy_pos_b3, hy_sin_freq, hy_pos_wout, hy_skip, hy_out_g, ssm_conv_w, ssm_conv_b, ssm_A_log, ssm_dt_bias, ssm_D, ssm_out_g, w_out, norm_mlp_g, w_up, w_down, final_norm_g):
    B, L, D = x.shape
    depth = w_in.shape[0]
    CH = hy_out_g.shape[1]
    WS = ssm_out_g.shape[1]
    heads = ssm_D.shape[1]
    xbc_w = ssm_conv_w.shape[2]
    groups = (xbc_w - WS) // (2 * SSM_STATE)
    o1 = 3 * CH
    o2 = o1 + WS
    o3 = o2 + xbc_w
    assert WS == heads * SSM_HEAD_DIM and heads <= LANES and L % DFT_N2 == 0 and CH % HY_CB == 0

    consts = _dft_consts(L)
    z2 = _pos_features(L, hy_pos_w1.shape[1])
    w_in_b = w_in.astype(BF16)
    w_in_dt = jnp.pad(w_in[:, :, o3:], ((0, 0), (0, 0), (0, LANES - heads))).astype(BF16)
    w_out_b = w_out.astype(BF16)
    w_up_b = w_up.astype(BF16)
    w_down_b = w_down.astype(BF16)
    pad_h = lambda v: jnp.pad(v, (0, LANES - heads)).reshape(1, LANES)
    d_rep = jnp.repeat(ssm_D, SSM_HEAD_DIM, axis=-1)

    outs = []
    for b in range(B):
        xb = x[b]
        delta = None
        for l in range(depth):
            if delta is None:
                h = _rmsnorm(xb, norm_mix_g[l], BF16)
            else:
                xb, h = _add_rmsnorm(xb, delta, norm_mix_g[l], BF16)
            proj = _matmul(h, w_in_b, l, n_cols=o3)
            dt_raw = _matmul(h, w_in_dt, l, bm_t=2048)

            nj = CH // HY_CB
            uh = _dwconv(proj, 0, o1, hy_conv_w[l], hy_conv_b[l], "none", blocked_out=True)
            hm = _filter_mlp(z2, hy_pos_w1[l], hy_pos_b1[l], hy_pos_w2[l], hy_pos_b2[l],
                             hy_pos_w3[l], hy_pos_b3[l], hy_sin_freq[l])
            g_all = _hy_filters(hm, hy_pos_wout[l], CH, L, consts)
            y0 = _hy_inv(_hy_fwd(uh, 2 * nj, nj, consts), g_all, 0, consts, L)
            z1 = _hy_gate1(uh, 0, 2, y0, hy_skip[l, 0])
            y1 = _hy_inv(_hy_fwd(z1, 0, nj, consts), g_all, 1, consts, L)
            y_hy = _hy_gate2(uh, 1, z1, y1, hy_skip[l, 1], hy_out_g[l])

            xbc = _dwconv(proj, o2, xbc_w, ssm_conv_w[l], ssm_conv_b[l], "silu")
            yf = _ssd(xbc, dt_raw, pad_h(ssm_dt_bias[l, 0]), pad_h(ssm_A_log[l, 0]),
                      reverse=False, heads=heads, groups=groups)
            yb = _ssd(xbc, dt_raw, pad_h(ssm_dt_bias[l, 1]), pad_h(ssm_A_log[l, 1]),
                      reverse=True, heads=heads, groups=groups)
            y_ssm = _ssm_gate(yf, yb, xbc, proj, o1, d_rep[l], ssm_out_g[l], groups)

            xb = _out_matmul(y_hy, y_ssm, w_out_b, l, xb)
            h2 = _rmsnorm(xb, norm_mlp_g[l], BF16)
            delta = _mlp(h2, w_up_b, w_down_b, l)
        outs.append(_add_rmsnorm(xb, delta, final_norm_g, F32, keep_sum=False))
    return jnp.stack(outs, axis=0)
```

```python
import functools
import math

import numpy as np
import jax
import jax.numpy as jnp
from jax import lax
from jax.experimental import pallas as pl
from jax.experimental.pallas import tpu as pltpu

F32 = jnp.float32
BF16 = jnp.bfloat16

NORM_EPS = 1e-5
SSM_HEAD_DIM = 64
SSM_STATE = 128
SSM_CHUNK = 256
HY_FAST_DECAY = 0.3
HY_SLOW_DECAY = 1.5
HY_DECAY_TARGET = 1e-2

LANES = 128
SUBLANES = 8
DFT_N2 = 128
HY_CB = 128
DFT_UNROLL = 16
DFT_GROUP = 32
TT_CHUNK = SUBLANES * DFT_N2
FILT_GROUP = 16
SPEC_DTYPE = BF16
VMEM_LIMIT_MB = 56
CONV_VMEM_MB = 60


def _cparams(sem, vmem_mb=VMEM_LIMIT_MB):
    return pltpu.CompilerParams(dimension_semantics=sem, vmem_limit_bytes=vmem_mb * 1024 * 1024)


def _pick(n, target, mult):
    best = None
    for d in range(mult, min(n, target) + 1, mult):
        if n % d == 0:
            best = d
    assert best is not None, (n, target, mult)
    return best


def _dot(a, b):
    return jnp.dot(a, b, preferred_element_type=F32)


def _split3(a):
    hi = a.astype(BF16)
    r1 = a - hi.astype(F32)
    mid = r1.astype(BF16)
    lo = (r1 - mid.astype(F32)).astype(BF16)
    return hi, mid, lo


def _dot_f32(a, b):
    ah = a.astype(BF16)
    al = (a - ah.astype(F32)).astype(BF16)
    bh = b.astype(BF16)
    bl = (b - bh.astype(F32)).astype(BF16)
    return _dot(ah, bh) + (_dot(ah, bl) + _dot(al, bh))


def _silu(x):
    return x / (1.0 + jnp.exp(-x))


def _rmsnorm_body(x_ref, g_ref, h_ref):
    x = x_ref[...]
    ms = jnp.mean(x * x, axis=-1, keepdims=True)
    h_ref[...] = (x * lax.rsqrt(ms + NORM_EPS) * g_ref[...]).astype(h_ref.dtype)


def _rmsnorm(x, g, out_dtype):
    M, D = x.shape
    bm = _pick(M, 256, SUBLANES)
    return pl.pallas_call(
        _rmsnorm_body, grid=(M // bm,),
        in_specs=[pl.BlockSpec((bm, D), lambda i: (i, 0)), pl.BlockSpec((1, D), lambda i: (0, 0))],
        out_specs=pl.BlockSpec((bm, D), lambda i: (i, 0)),
        out_shape=jax.ShapeDtypeStruct((M, D), out_dtype),
        compiler_params=_cparams(("parallel",)), name="rmsnorm",
    )(x, g.reshape(1, D))


def _add_rmsnorm_body(x_ref, d_ref, g_ref, *out_refs):
    x = x_ref[...] + d_ref[...]
    if len(out_refs) == 2:
        out_refs[0][...] = x
    h_ref = out_refs[-1]
    ms = jnp.mean(x * x, axis=-1, keepdims=True)
    h_ref[...] = (x * lax.rsqrt(ms + NORM_EPS) * g_ref[...]).astype(h_ref.dtype)


def _add_rmsnorm(x, delta, g, out_dtype, keep_sum=True):
    M, D = x.shape
    bm = _pick(M, 256, SUBLANES)
    row = pl.BlockSpec((bm, D), lambda i: (i, 0))
    h_shape = jax.ShapeDtypeStruct((M, D), out_dtype)
    return pl.pallas_call(
        _add_rmsnorm_body, grid=(M // bm,),
        in_specs=[row, row, pl.BlockSpec((1, D), lambda i: (0, 0))],
        out_specs=[row, row] if keep_sum else row,
        out_shape=[jax.ShapeDtypeStruct((M, D), F32), h_shape] if keep_sum else h_shape,
        compiler_params=_cparams(("parallel",)), name="add_rmsnorm",
    )(x, delta, g.reshape(1, D))


def _mm_body(a_ref, b_ref, o_ref):
    o_ref[...] = _dot(a_ref[...], b_ref[...]).astype(o_ref.dtype)


def _matmul(a, b_stack, layer, n_cols=None, out_dtype=F32, bm_t=1024, bn_t=1024):
    M, K = a.shape
    N = b_stack.shape[2] if n_cols is None else n_cols
    bm = _pick(M, bm_t, SUBLANES)
    bn = _pick(N, bn_t, LANES)
    return pl.pallas_call(
        _mm_body, grid=(M // bm, N // bn),
        in_specs=[pl.BlockSpec((bm, K), lambda i, j: (i, 0)),
                  pl.BlockSpec((None, K, bn), lambda i, j: (layer, 0, j))],
        out_specs=pl.BlockSpec((bm, bn), lambda i, j: (i, j)),
        out_shape=jax.ShapeDtypeStruct((M, N), out_dtype),
        compiler_params=_cparams(("parallel", "parallel")), name="proj_matmul",
    )(a, b_stack)


def _out_mm_body(a1_ref, a2_ref, b1_ref, b2_ref, r_ref, o_ref):
    acc = _dot(a1_ref[...], b1_ref[...]) + _dot(a2_ref[...], b2_ref[...])
    o_ref[...] = r_ref[...] + acc


def _out_matmul(a1, a2, b_stack, layer, res):
    M, K1 = a1.shape
    assert a2.shape[1] == K1 and b_stack.shape[1] == 2 * K1
    N = b_stack.shape[2]
    bm = _pick(M, 1024, SUBLANES)
    bn = _pick(N, 1024, LANES)
    return pl.pallas_call(
        _out_mm_body, grid=(M // bm, N // bn),
        in_specs=[pl.BlockSpec((bm, K1), lambda i, j: (i, 0)), pl.BlockSpec((bm, K1), lambda i, j: (i, 0)),
                  pl.BlockSpec((None, K1, bn), lambda i, j: (layer, 0, j)),
                  pl.BlockSpec((None, K1, bn), lambda i, j: (layer, 1, j)),
                  pl.BlockSpec((bm, bn), lambda i, j: (i, j))],
        out_specs=pl.BlockSpec((bm, bn), lambda i, j: (i, j)),
        out_shape=jax.ShapeDtypeStruct((M, N), F32),
        compiler_params=_cparams(("parallel", "parallel")), name="out_matmul",
    )(a1, a2, b_stack, b_stack, res)


def _mlp_body(h_ref, wu_ref, wd_ref, o_ref, *, cw):
    j = pl.program_id(1)

    @pl.when(j == 0)
    def _():
        o_ref[...] = jnp.zeros_like(o_ref)

    hw = wu_ref.shape[1] // 2
    ts = []
    for q in range(2):
        t = _dot(h_ref[...], wu_ref[:, q * hw:(q + 1) * hw])
        ts.append(jnp.square(jnp.maximum(t, 0.0)).astype(BF16))
    for c in range(o_ref.shape[1] // cw):
        cols = slice(c * cw, (c + 1) * cw)
        o_ref[:, cols] += _dot(ts[0], wd_ref[0:hw, cols]) + _dot(ts[1], wd_ref[hw:2 * hw, cols])


def _mlp(h, w_up_stack, w_down_stack, layer):
    M, D = h.shape
    F = w_up_stack.shape[2]
    bm = _pick(M, 512, SUBLANES)
    bf = _pick(F, 512, LANES)
    cw = _pick(D, 512, LANES)
    return pl.pallas_call(
        functools.partial(_mlp_body, cw=cw), grid=(M // bm, F // bf),
        in_specs=[pl.BlockSpec((bm, D), lambda i, j: (i, 0)),
                  pl.BlockSpec((None, D, bf), lambda i, j: (layer, 0, j)),
                  pl.BlockSpec((None, bf, D), lambda i, j: (layer, j, 0))],
        out_specs=pl.BlockSpec((bm, D), lambda i, j: (i, 0)),
        out_shape=jax.ShapeDtypeStruct((M, D), F32),
        compiler_params=_cparams(("parallel", "arbitrary")), name="mlp",
    )(h, w_up_stack, w_down_stack)


def _dwconv_body(p_ref, pv_ref, nx_ref, w_ref, b_ref, o_ref, *, act):
    i = pl.program_id(0)
    last = pl.num_programs(0) - 1
    u = p_ref[...]
    bm = u.shape[0]
    prev_row = jnp.where(i > 0, pv_ref[SUBLANES - 1:SUBLANES, :], 0.0)
    next_row = jnp.where(i < last, nx_ref[0:1, :], 0.0)
    rows = lax.broadcasted_iota(jnp.int32, u.shape, 0)
    um = jnp.where(rows == 0, prev_row, pltpu.roll(u, 1, 0))
    up = jnp.where(rows == bm - 1, next_row, pltpu.roll(u, bm - 1, 0))
    y = w_ref[0:1, :] * um
    y = y + w_ref[1:2, :] * u
    y = y + w_ref[2:3, :] * up
    y = y + b_ref[...]
    if act == "silu":
        y = _silu(y)
    if len(o_ref.shape) == 3:
        assert bm == TT_CHUNK
        for k in range(o_ref.shape[0]):
            for r in range(SUBLANES):
                o_ref.at[k][pl.ds(r, DFT_N2, stride=SUBLANES), :] = (
                    y[r * DFT_N2:(r + 1) * DFT_N2, k * LANES:(k + 1) * LANES])
    else:
        o_ref[...] = y


def _dwconv(p, col_off, C, w, b, act, blocked_out=False):
    L = p.shape[0]
    bm = TT_CHUNK if blocked_out else _pick(L, 1024, SUBLANES)
    assert L % bm == 0
    bc = _pick(math.gcd(C, col_off) if col_off else C, 1024, LANES)
    co = col_off // bc
    hb = bm // SUBLANES
    nrb = L // SUBLANES
    if blocked_out:
        spb = bc // LANES
        out_spec = pl.BlockSpec((spb, bm, LANES), lambda i, j: (j, i, 0))
        out_shape = jax.ShapeDtypeStruct((C // LANES, L, LANES), F32)
    else:
        out_spec = pl.BlockSpec((bm, bc), lambda i, j: (i, j))
        out_shape = jax.ShapeDtypeStruct((L, C), F32)
    return pl.pallas_call(
        functools.partial(_dwconv_body, act=act), grid=(L // bm, C // bc),
        in_specs=[pl.BlockSpec((bm, bc), lambda i, j: (i, co + j)),
                  pl.BlockSpec((SUBLANES, bc), lambda i, j: (jnp.maximum(i * hb - 1, 0), co + j)),
                  pl.BlockSpec((SUBLANES, bc), lambda i, j: (jnp.minimum((i + 1) * hb, nrb - 1), co + j)),
                  pl.BlockSpec((3, bc), lambda i, j: (0, j)),
                  pl.BlockSpec((1, bc), lambda i, j: (0, j))],
        out_specs=out_spec, out_shape=out_shape,
        compiler_params=_cparams(("parallel", "parallel")), name="dwconv_" + act,
    )(p, p, p, w, b.reshape(1, C))


def _expand_heads(v, ex_ref):
    hi = v.astype(BF16)
    lo = (v - hi.astype(F32)).astype(BF16)
    return _dot(hi, ex_ref[...]) + _dot(lo, ex_ref[...])


def _ssd_body(xbc_ref, dt_ref, bias_ref, alog_ref, ex_ref, y_ref, s_ref, *, reverse, heads, groups):
    Q = SSM_CHUNK
    P = SSM_HEAD_DIM
    N = SSM_STATE
    R = heads // groups
    W = heads * P
    step = pl.program_id(0)

    @pl.when(step == 0)
    def _():
        s_ref[...] = jnp.zeros_like(s_ref)

    x = dt_ref[...] + bias_ref[...]
    dt = jnp.maximum(x, 0.0) + jnp.log1p(jnp.exp(-jnp.abs(x)))
    a = dt * (-jnp.exp(alog_ref[...]))
    ri = lax.broadcasted_iota(jnp.int32, (Q, Q), 0)
    ci = lax.broadcasted_iota(jnp.int32, (Q, Q), 1)
    mask = (ci >= ri) if reverse else (ri >= ci)
    tri = jnp.where(mask, 1.0, 0.0).astype(BF16)
    ah, am, al = _split3(a)
    acs = _dot(tri, ah) + (_dot(tri, am) + _dot(tri, al))
    tot = acs[0:1, :] if reverse else acs[Q - 1:Q, :]
    e_all = _expand_heads(jnp.exp(acs), ex_ref)
    w_all = _expand_heads(jnp.exp(tot - acs) * dt, ex_ref)
    t_all = _expand_heads(jnp.broadcast_to(jnp.exp(tot), (SUBLANES, LANES)), ex_ref)[0:1, :]
    acs_row = acs.T
    dt_row = dt.T

    load_b = lambda g: xbc_ref[:, W + g * N:W + (g + 1) * N].astype(BF16)
    load_c = lambda g: xbc_ref[:, W + groups * N + g * N:W + groups * N + (g + 1) * N].astype(BF16)
    load_x = lambda g: xbc_ref[:, g * R * P:(g + 1) * R * P]
    lane = lax.broadcasted_iota(jnp.int32, (Q, R * P), 1)

    cbs, y_offs = [], []
    for g in range(groups):
        cg = load_c(g)
        cbs.append(lax.dot_general(cg, load_b(g), (((1,), (1,)), ((), ())), preferred_element_type=F32))
        y_offs.append(_dot(cg, s_ref[g].astype(BF16)))
    for g in range(groups):
        gl = slice(g * R * P, (g + 1) * R * P)
        s_new = lax.dot_general(load_b(g), (load_x(g) * w_all[:, gl]).astype(BF16), (((0,), (0,)), ((), ())),
                                preferred_element_type=F32)
        s_ref[g] = s_ref[g] * t_all[:, gl] + s_new
    for g in range(groups):
        xg = load_x(g)
        ws, xs = [], []
        for r in range(R):
            h = g * R + r
            diff = acs[:, h:h + 1] - acs_row[h:h + 1, :]
            lm = jnp.exp(jnp.where(mask, diff, -jnp.inf))
            ws.append((cbs[g] * lm * dt_row[h:h + 1, :]).astype(BF16))
            xs.append(jnp.where((lane >= r * P) & (lane < (r + 1) * P), xg, 0.0).astype(BF16))
        y_diag = _dot(jnp.concatenate(ws, axis=1), jnp.concatenate(xs, axis=0))
        gl = slice(g * R * P, (g + 1) * R * P)
        y_ref[:, gl] = y_diag + y_offs[g] * e_all[:, gl]


def _ssd(xbc, dt_raw, dt_bias, a_log, *, reverse, heads, groups):
    L = xbc.shape[0]
    Q = SSM_CHUNK
    assert L % Q == 0
    nc = L // Q
    W = heads * SSM_HEAD_DIM
    R = heads // groups
    cmap = (lambda i: (nc - 1 - i, 0)) if reverse else (lambda i: (i, 0))
    ex = np.zeros((LANES, W), np.float32)
    ex[np.arange(W) // SSM_HEAD_DIM, np.arange(W)] = 1.0
    return pl.pallas_call(
        functools.partial(_ssd_body, reverse=reverse, heads=heads, groups=groups), grid=(nc,),
        in_specs=[pl.BlockSpec((Q, xbc.shape[1]), cmap), pl.BlockSpec((Q, LANES), cmap),
                  pl.BlockSpec((1, LANES), lambda i: (0, 0)), pl.BlockSpec((1, LANES), lambda i: (0, 0)),
                  pl.BlockSpec((LANES, W), lambda i: (0, 0))],
        out_specs=pl.BlockSpec((Q, W), cmap),
        out_shape=jax.ShapeDtypeStruct((L, W), F32),
        scratch_shapes=[pltpu.VMEM((groups, SSM_STATE, R * SSM_HEAD_DIM), F32)],
        compiler_params=_cparams(("arbitrary",)), name="ssd_bwd" if reverse else "ssd_fwd",
    )(xbc, dt_raw, dt_bias, a_log, jnp.asarray(ex.astype(BF16)))


def _ssm_gate_body(yf_ref, yb_ref, xh_ref, z_ref, d_ref, g_ref, o_ref, *, groups):
    y = (yf_ref[...] + yb_ref[...] + d_ref[...] * xh_ref[...]) * _silu(z_ref[...])
    W = y.shape[1]
    gw = W // groups
    for g in range(groups):
        yg = y[:, g * gw:(g + 1) * gw]
        ms = jnp.mean(yg * yg, axis=-1, keepdims=True)
        o_ref[:, g * gw:(g + 1) * gw] = (yg * lax.rsqrt(ms + NORM_EPS) * g_ref[:, g * gw:(g + 1) * gw]).astype(o_ref.dtype)


def _ssm_gate(yf, yb, xbc, proj, z_off, d_rep, norm_g, groups):
    L, W = yf.shape
    bm = _pick(L, 256, SUBLANES)
    assert z_off % W == 0
    zo = z_off // W
    row = pl.BlockSpec((bm, W), lambda i: (i, 0))
    vec = pl.BlockSpec((1, W), lambda i: (0, 0))
    return pl.pallas_call(
        functools.partial(_ssm_gate_body, groups=groups), grid=(L // bm,),
        in_specs=[row, row, row, pl.BlockSpec((bm, W), lambda i: (i, zo)), vec, vec],
        out_specs=row, out_shape=jax.ShapeDtypeStruct((L, W), BF16),
        compiler_params=_cparams(("parallel",)), name="ssm_gate",
    )(yf, yb, xbc, proj, d_rep.reshape(1, W), norm_g.reshape(1, W))


def _dft_consts(L):
    N2 = DFT_N2
    N = 2 * L
    H1 = L // N2
    n2 = np.arange(N2)[:, None, None]
    k1 = np.arange(H1)[None, :, None]
    n1 = np.arange(H1)[None, None, :]
    m = ((2 * k1 + 1) * (N2 * n1 + n2)) % (2 * N)
    ph = np.pi * m.astype(np.float64) / N
    m_fwd = np.concatenate([np.cos(ph), -np.sin(ph)], axis=1)
    m_inv = (2.0 / N) * np.concatenate([np.cos(ph), -np.sin(ph)], axis=1).transpose(0, 2, 1)
    sg = np.where(np.arange(H1) % 2 == 0, 1.0, -1.0)[None, :, None]
    m_filt = np.concatenate([np.concatenate([np.cos(ph), -sg * np.sin(ph)], axis=2),
                             np.concatenate([-np.sin(ph), -sg * np.cos(ph)], axis=2)], axis=1)
    kk = (np.arange(N2)[:, None] * np.arange(N2)[None, :]) % N2
    c2 = np.cos(2 * np.pi * kk / N2)
    s2 = np.sin(2 * np.pi * kk / N2)
    f2_fwd = np.block([[c2, s2], [-s2, c2]])
    f2_inv = np.block([[c2, -s2], [s2, c2]])
    as_bf = lambda a: jnp.asarray(a.astype(BF16))
    return as_bf(m_fwd), as_bf(m_inv), as_bf(f2_fwd), as_bf(f2_inv), as_bf(m_filt)


def _pitch(rows):
    p = rows + SUBLANES
    assert (p // SUBLANES) % 2 == 1
    return p


def _pack2(a, b):
    ha = lax.bitcast_convert_type(a.astype(BF16).astype(F32), jnp.uint32)
    hb = lax.bitcast_convert_type(b.astype(BF16).astype(F32), jnp.uint32)
    return ha | (hb >> 16)


def _unpack2(w):
    a = lax.bitcast_convert_type(w & jnp.uint32(0xFFFF0000), F32)
    b = lax.bitcast_convert_type(w << 16, F32)
    return jnp.concatenate([a, b], axis=1).astype(BF16)


def _split_lanes(x):
    c = x.shape[1] // 2
    return x[:, :c], x[:, c:]


def _stage3(a_s, f2_ref, k1, H1, pitch):
    N2 = DFT_N2
    ar = _unpack2(a_s[pl.ds(k1, N2, stride=pitch), :])
    ai = _unpack2(a_s[pl.ds(H1 + k1, N2, stride=pitch), :])
    return _dot(f2_ref[...], jnp.concatenate([ar, ai], axis=0))


def _fwd_body(u_ref, m_ref, f2_ref, x_ref, a_s, *, H1, ks, gw, ng):
    N2, cb = DFT_N2, HY_CB
    pitch = _pitch(2 * H1)
    s = pl.program_id(1)

    @pl.when(s < ng)
    def _():
        for t in range(gw):
            n2 = s * gw + t
            rows = slice(t * SUBLANES, (t + 1) * SUBLANES)
            u = jnp.concatenate([u_ref[0, :, rows, :].reshape(H1, cb), u_ref[1, :, rows, :].reshape(H1, cb)],
                                axis=1).astype(BF16)
            a = _dot(m_ref[n2], u)
            a_s[pl.ds(pl.multiple_of(n2 * pitch, SUBLANES), 2 * H1), :] = _pack2(*_split_lanes(a))

    @pl.when(s >= ng)
    def _():
        def kbody(kk, carry):
            x = _stage3(a_s, f2_ref, (s - ng) * ks + kk, H1, pitch).astype(x_ref.dtype)
            for p, xp in enumerate(_split_lanes(x)):
                x_ref[p, kk, 0] = xp[:N2]
                x_ref[p, kk, 1] = xp[N2:]
            return carry
        lax.fori_loop(0, ks, kbody, 0, unroll=DFT_UNROLL)


def _hy_fwd(u_blk, slab_off, nj, consts):
    L = u_blk.shape[1]
    N2, cb = DFT_N2, HY_CB
    H1 = L // N2
    ks = _pick(H1, 16, 1)
    gw = _pick(N2, DFT_GROUP, 1)
    ng, nsl = N2 // gw, H1 // ks
    m_fwd, _, f2_fwd, _, _ = consts
    assert H1 % SUBLANES == 0 and nj % 2 == 0 and slab_off % 2 == 0
    u4 = u_blk.reshape(u_blk.shape[0], H1 // SUBLANES, TT_CHUNK, cb)
    return pl.pallas_call(
        functools.partial(_fwd_body, H1=H1, ks=ks, gw=gw, ng=ng), grid=(nj // 2, ng + nsl),
        in_specs=[pl.BlockSpec((2, H1 // SUBLANES, gw * SUBLANES, cb),
                               lambda j, s: (slab_off // 2 + j, 0, jnp.minimum(s, ng - 1), 0)),
                  pl.BlockSpec((N2, 2 * H1, H1), lambda j, s: (0, 0, 0)),
                  pl.BlockSpec((2 * N2, 2 * N2), lambda j, s: (0, 0))],
        out_specs=pl.BlockSpec((2, ks, 2, N2, cb), lambda j, s: (j, jnp.maximum(s - ng, 0), 0, 0, 0)),
        out_shape=jax.ShapeDtypeStruct((nj, H1, 2, N2, cb), SPEC_DTYPE),
        scratch_shapes=[pltpu.VMEM((N2 * _pitch(2 * H1), cb), jnp.uint32)],
        compiler_params=_cparams(("parallel", "arbitrary")), name="hy_fwd",
    )(u4, m_fwd, f2_fwd)


def _inv_body(x_ref, g_ref, m_ref, f2_ref, y_ref, b_s, *, H1, ks, gw, nsl):
    N2, cb = DFT_N2, HY_CB
    pitch = _pitch(2 * N2)
    s = pl.program_id(1)

    @pl.when(s < nsl)
    def _():
        def kbody(kk, carry):
            pair = lambda ref, c: jnp.concatenate([ref[0, kk, c], ref[1, kk, c]], axis=1)
            xr, xi = pair(x_ref, 0), pair(x_ref, 1)
            gr, gi = pair(g_ref, 0), pair(g_ref, 1)
            yr = xr * gr - xi * gi
            yi = xr * gi + xi * gr
            b = _dot(f2_ref[...], jnp.concatenate([yr, yi], axis=0).astype(BF16))
            b_s[pl.ds(pl.multiple_of((s * ks + kk) * pitch, SUBLANES), 2 * N2), :] = _pack2(*_split_lanes(b))
            return carry
        lax.fori_loop(0, ks, kbody, 0, unroll=DFT_UNROLL)

    @pl.when(s >= nsl)
    def _():
        for t in range(gw):
            n2 = (s - nsl) * gw + t
            br = _unpack2(b_s[pl.ds(n2, H1, stride=pitch), :])
            bi = _unpack2(b_s[pl.ds(N2 + n2, H1, stride=pitch), :])
            y = _dot(m_ref[n2], jnp.concatenate([br, bi], axis=0))
            for p, yp in enumerate(_split_lanes(y)):
                y_ref[p, :, t * SUBLANES:(t + 1) * SUBLANES, :] = yp.reshape(H1 // SUBLANES, SUBLANES, cb)


def _hy_inv(x, g_all, order, consts, L):
    nj, H1, _, N2, cb = x.shape
    ks = _pick(H1, 16, 1)
    gw = _pick(N2, DFT_GROUP, 1)
    ng, nsl = N2 // gw, H1 // ks
    _, m_inv, _, f2_inv, _ = consts
    y4 = pl.pallas_call(
        functools.partial(_inv_body, H1=H1, ks=ks, gw=gw, nsl=nsl), grid=(nj // 2, nsl + ng),
        in_specs=[pl.BlockSpec((2, ks, 2, N2, cb), lambda j, s: (j, jnp.minimum(s, nsl - 1), 0, 0, 0)),
                  pl.BlockSpec((None, 2, ks, 2, N2, cb),
                               lambda j, s: (order, j, jnp.minimum(s, nsl - 1), 0, 0, 0)),
                  pl.BlockSpec((N2, H1, 2 * H1), lambda j, s: (0, 0, 0)),
                  pl.BlockSpec((2 * N2, 2 * N2), lambda j, s: (0, 0))],
        out_specs=pl.BlockSpec((2, H1 // SUBLANES, gw * SUBLANES, cb),
                               lambda j, s: (j, 0, jnp.maximum(s - nsl, 0), 0)),
        out_shape=jax.ShapeDtypeStruct((nj, H1 // SUBLANES, TT_CHUNK, cb), F32),
        scratch_shapes=[pltpu.VMEM((H1 * _pitch(2 * N2), cb), jnp.uint32)],
        compiler_params=_cparams(("parallel", "arbitrary")), name="hy_inv",
    )(x, g_all, m_inv, f2_inv)
    return y4.reshape(nj, L, cb)


def _conv_body(u_ref, g_ref, mf_ref, mi_ref, f2f_ref, f2i_ref, y_ref, a_s, *, H1, ks, gw, ng, nsl):
    N2, cb = DFT_N2, HY_CB
    pitch = _pitch(2 * H1)
    s = pl.program_id(1)

    @pl.when(s < ng)
    def _():
        for t in range(gw):
            n2 = s * gw + t
            rows = slice(t * SUBLANES, (t + 1) * SUBLANES)
            u = jnp.concatenate([u_ref[0, :, rows, :].reshape(H1, cb), u_ref[1, :, rows, :].reshape(H1, cb)],
                                axis=1).astype(BF16)
            a = _dot(mf_ref[n2], u)
            a_s[pl.ds(pl.multiple_of(n2 * pitch, SUBLANES), 2 * H1), :] = _pack2(*_split_lanes(a))

    @pl.when((s >= ng) & (s < ng + nsl))
    def _():
        k0 = (s - ng) * ks
        xs = [_stage3(a_s, f2f_ref, k0 + kk, H1, pitch).astype(BF16) for kk in range(ks)]
        bs = []
        for kk in range(ks):
            gr = jnp.concatenate([g_ref[0, kk, 0], g_ref[1, kk, 0]], axis=1)
            gi = jnp.concatenate([g_ref[0, kk, 1], g_ref[1, kk, 1]], axis=1)
            xr, xi = xs[kk][:N2], xs[kk][N2:]
            y = jnp.concatenate([xr * gr - xi * gi, xr * gi + xi * gr], axis=0)
            bs.append(_pack2(*_split_lanes(_dot(f2i_ref[...], y))))
        for kk in range(ks):
            a_s[pl.ds(k0 + kk, N2, stride=pitch), :] = bs[kk][:N2]
            a_s[pl.ds(H1 + k0 + kk, N2, stride=pitch), :] = bs[kk][N2:]

    @pl.when(s >= ng + nsl)
    def _():
        for t in range(gw):
            n2 = (s - ng - nsl) * gw + t
            b = _unpack2(a_s[pl.ds(pl.multiple_of(n2 * pitch, SUBLANES), 2 * H1), :])
            y = _dot(mi_ref[n2], b)
            for p, yp in enumerate(_split_lanes(y)):
                y_ref[p, :, t * SUBLANES:(t + 1) * SUBLANES, :] = yp.reshape(H1 // SUBLANES, SUBLANES, cb)


def _hy_conv(u_blk, slab_off, nj, g_all, order, consts):
    L = u_blk.shape[1]
    N2, cb = DFT_N2, HY_CB
    H1 = L // N2
    ks = _pick(H1, 16, 1)
    gw = _pick(N2, DFT_GROUP, 1)
    ng, nsl = N2 // gw, H1 // ks
    m_fwd, m_inv, f2_fwd, f2_inv, _ = consts
    assert H1 % SUBLANES == 0 and nj % 2 == 0 and slab_off % 2 == 0
    u4 = u_blk.reshape(u_blk.shape[0], H1 // SUBLANES, TT_CHUNK, cb)
    const3 = lambda shape: pl.BlockSpec(shape, lambda j, s: (0, 0, 0), pipeline_mode=pl.Buffered(1))
    y4 = pl.pallas_call(
        functools.partial(_conv_body, H1=H1, ks=ks, gw=gw, ng=ng, nsl=nsl), grid=(nj // 2, 2 * ng + nsl),
        in_specs=[pl.BlockSpec((2, H1 // SUBLANES, gw * SUBLANES, cb),
                               lambda j, s: (slab_off // 2 + j, 0, jnp.minimum(s, ng - 1), 0)),
                  pl.BlockSpec((None, 2, ks, 2, N2, cb),
                               lambda j, s: (order, j, jnp.clip(s - ng, 0, nsl - 1), 0, 0, 0)),
                  const3((N2, 2 * H1, H1)), const3((N2, H1, 2 * H1)),
                  pl.BlockSpec((2 * N2, 2 * N2), lambda j, s: (0, 0)),
                  pl.BlockSpec((2 * N2, 2 * N2), lambda j, s: (0, 0))],
        out_specs=pl.BlockSpec((2, H1 // SUBLANES, gw * SUBLANES, cb),
                               lambda j, s: (j, 0, jnp.clip(s - ng - nsl, 0, ng - 1), 0)),
        out_shape=jax.ShapeDtypeStruct((nj, H1 // SUBLANES, TT_CHUNK, cb), F32),
        scratch_shapes=[pltpu.VMEM((N2 * _pitch(2 * H1), cb), jnp.uint32)],
        compiler_params=_cparams(("parallel", "arbitrary"), vmem_mb=CONV_VMEM_MB), name="hy_conv",
    )(u4, g_all, m_fwd, m_inv, f2_fwd, f2_inv)
    return y4.reshape(nj, L, cb)


def _pos_features(L, emb):
    N2 = DFT_N2
    H1 = L // N2
    bands = (emb - 1) // 2
    r = jnp.arange(L)
    lag = (r % H1) * N2 + (r // H1)
    f = jnp.linspace(1e-4, bands - 1, bands, dtype=F32)[None, :]

    def feats(lg):
        lgf = lg.astype(F32)[:, None]
        t = lgf / (L - 1)
        w = 2.0 * math.pi * lgf / L
        z = jnp.concatenate([t, jnp.cos(f * w), -jnp.sin(f * w)], axis=-1)
        return jnp.pad(z, ((0, 0), (0, LANES - emb)))
    return jnp.concatenate([feats(lag), feats(jnp.where(lag == 0, 0, L - lag))], axis=-1)


def _filter_mlp_body(z_ref, w1_ref, b1_ref, w2_ref, b2_ref, w3_ref, b3_ref, fr_ref, o_ref):
    h = jnp.sin(fr_ref[0:1, :] * (_dot_f32(z_ref[...], w1_ref[...]) + b1_ref[...]))
    h = jnp.sin(fr_ref[1:2, :] * (_dot_f32(h, w2_ref[...]) + b2_ref[...]))
    h = jnp.sin(fr_ref[2:3, :] * (_dot_f32(h, w3_ref[...]) + b3_ref[...]))
    o_ref[...] = h.astype(o_ref.dtype)


def _bdiag(w):
    a, b = w.shape
    z = jnp.zeros((a, b), w.dtype)
    return jnp.concatenate([jnp.concatenate([w, z], axis=1), jnp.concatenate([z, w], axis=1)], axis=0)


def _filter_mlp(z2, w1, b1, w2, b2, w3, b3, fr):
    L = z2.shape[0]
    emb, hd = w1.shape
    assert 2 * hd == LANES
    w1p = jnp.zeros((2 * LANES, LANES), F32).at[:emb, :hd].set(w1).at[LANES:LANES + emb, hd:].set(w1)
    tile2 = lambda v: jnp.concatenate([v, v], axis=-1)
    bm = _pick(L, 1024, SUBLANES)
    full = lambda a: pl.BlockSpec(a.shape, lambda i: (0,) * a.ndim)
    args = (w1p, tile2(b1).reshape(1, LANES), _bdiag(w2), tile2(b2).reshape(1, LANES),
            _bdiag(w3), tile2(b3).reshape(1, LANES), tile2(fr))
    return pl.pallas_call(
        _filter_mlp_body, grid=(L // bm,),
        in_specs=[pl.BlockSpec((bm, 2 * LANES), lambda i: (i, 0))] + [full(a) for a in args],
        out_specs=pl.BlockSpec((bm, LANES), lambda i: (i, 0)),
        out_shape=jax.ShapeDtypeStruct((L, LANES), BF16),
        compiler_params=_cparams(("parallel",)), name="hy_filter_mlp",
    )(z2, *args)


def _filt_body(hm_ref, w2_ref, ad_ref, m_ref, f2_ref, g_ref, a_s, g_s, inv_s, *, L, H1, ks):
    N2 = DFT_N2
    cb = 2 * HY_CB
    pitch = _pitch(2 * H1)
    s = pl.program_id(2)

    @pl.when(s == 0)
    def _():
        rate = ad_ref[...] * (1.0 / (L - 1))
        n1 = lax.broadcasted_iota(jnp.int32, (H1, cb), 0)
        e1f = jnp.exp(-(n1 * N2).astype(F32) * rate)
        e1b = jnp.exp(-((H1 - 1 - n1) * N2).astype(F32) * rate)

        def group(gi, nrm):
            def gen(t, nrm):
                n2 = gi * FILT_GROUP + t
                hs = hm_ref[pl.ds(pl.multiple_of(n2 * H1, SUBLANES), H1), :]
                gp = _dot(hs, w2_ref[...])
                n2f = n2.astype(F32)
                gf = gp[:, :cb] * (e1f * jnp.exp(-n2f * rate))
                gb = gp[:, cb:] * (e1b * jnp.exp(-(N2 - n2f) * rate))
                gb = jnp.where(n1 + n2 == 0, 0.0, -gb)
                g_s[t, 0:H1, :] = gf.astype(BF16)
                g_s[t, H1:2 * H1, :] = gb.astype(BF16)
                return nrm + (jnp.abs(gf) + jnp.abs(gb))
            nrm = lax.fori_loop(0, FILT_GROUP, gen, nrm, unroll=DFT_UNROLL)

            def xform(t, carry):
                n2 = gi * FILT_GROUP + t
                a = _dot(m_ref[n2], g_s[t])
                a_s[pl.ds(pl.multiple_of(n2 * pitch, SUBLANES), 2 * H1), :] = _pack2(*_split_lanes(a))
                return carry
            lax.fori_loop(0, FILT_GROUP, xform, 0, unroll=DFT_UNROLL)
            return nrm
        nrm = lax.fori_loop(0, N2 // FILT_GROUP, group, jnp.zeros((H1, cb), F32))
        inv_s[...] = jnp.broadcast_to(1.0 / jnp.sum(nrm, axis=0, keepdims=True), inv_s.shape)

    scale = inv_s[0:1, :]

    def kbody(kk, carry):
        x = (_stage3(a_s, f2_ref, s * ks + kk, H1, pitch) * scale).astype(g_ref.dtype)
        for p, xp in enumerate(_split_lanes(x)):
            g_ref[p, kk, 0] = xp[:N2]
            g_ref[p, kk, 1] = xp[N2:]
        return carry
    lax.fori_loop(0, ks, kbody, 0, unroll=DFT_UNROLL)


def _hy_filters(hm, wout, C, L, consts):
    N2, cb = DFT_N2, HY_CB
    H1 = L // N2
    ks = _pick(H1, 16, 1)
    hd = wout.shape[0]
    nj = C // cb
    _, _, f2_fwd, _, m_filt = consts
    assert nj % 2 == 0
    pw = 2 * cb
    w4 = wout.reshape(hd, 2, 2, nj // 2, pw)
    wf = jnp.transpose(w4[:, :, 0], (1, 2, 0, 3))
    wb = jnp.transpose(w4[:, :, 1], (1, 2, 0, 3))
    z = jnp.zeros_like(wf)
    w2 = jnp.concatenate([jnp.concatenate([wf, z], axis=-1), jnp.concatenate([z, wb], axis=-1)], axis=2)
    w2 = w2.astype(BF16)
    deltas = jnp.linspace(math.log(HY_FAST_DECAY) / HY_DECAY_TARGET,
                          math.log(HY_SLOW_DECAY) / HY_DECAY_TARGET, C, dtype=F32)
    absd = jnp.abs(deltas).reshape(1, C)
    return pl.pallas_call(
        functools.partial(_filt_body, L=L, H1=H1, ks=ks), grid=(2, nj // 2, H1 // ks),
        in_specs=[pl.BlockSpec((L, LANES), lambda o, j, s: (0, 0)),
                  pl.BlockSpec((None, None, 2 * hd, 2 * pw), lambda o, j, s: (o, j, 0, 0)),
                  pl.BlockSpec((1, pw), lambda o, j, s: (0, j)),
                  pl.BlockSpec((N2, 2 * H1, 2 * H1), lambda o, j, s: (0, 0, 0), pipeline_mode=pl.Buffered(1)),
                  pl.BlockSpec((2 * N2, 2 * N2), lambda o, j, s: (0, 0))],
        out_specs=pl.BlockSpec((None, 2, ks, 2, N2, cb), lambda o, j, s: (o, j, s, 0, 0, 0)),
        out_shape=jax.ShapeDtypeStruct((2, nj, H1, 2, N2, cb), SPEC_DTYPE),
        scratch_shapes=[pltpu.VMEM((N2 * _pitch(2 * H1), cb), jnp.uint32),
                        pltpu.VMEM((FILT_GROUP, 2 * H1, pw), BF16), pltpu.VMEM((SUBLANES, pw), F32)],
        compiler_params=_cparams(("parallel", "parallel", "arbitrary")), name="hy_filters",
    )(hm, w2, absd, m_filt, f2_fwd)


def _hy_gate1_body(gate_ref, v_ref, y_ref, sk_ref, o_ref):
    o_ref[...] = gate_ref[...] * (y_ref[...] + sk_ref[...] * v_ref[...])


def _hy_gate1(uh_blk, gate_blk, v_blk, y_blk, sk):
    nj, L, cb = y_blk.shape
    bm = _pick(L, 256, SUBLANES)
    blk = lambda g: pl.BlockSpec((nj, bm, cb), lambda i: (g, i, 0))
    return pl.pallas_call(
        _hy_gate1_body, grid=(L // bm,),
        in_specs=[blk(gate_blk), blk(v_blk), blk(0), pl.BlockSpec((nj, 1, cb), lambda i: (0, 0, 0))],
        out_specs=blk(0), out_shape=jax.ShapeDtypeStruct((nj, L, cb), F32),
        compiler_params=_cparams(("parallel",)), name="hy_gate1",
    )(uh_blk, uh_blk, y_blk, sk.reshape(nj, 1, cb))


def _hy_gate2_body(gate_ref, v_ref, y_ref, sk_ref, g_ref, o_ref, *, gw):
    nj, _, cb = gate_ref.shape
    for r in range(SUBLANES):
        rows = pl.ds(r, gw, stride=SUBLANES)
        zs = []
        ss = jnp.zeros((gw, cb), F32)
        for k in range(nj):
            z = gate_ref.at[k][rows, :] * (y_ref.at[k][rows, :] + sk_ref[k] * v_ref.at[k][rows, :])
            zs.append(z)
            ss = ss + z * z
        ms = jnp.sum(ss, axis=-1, keepdims=True) * (1.0 / (nj * cb))
        inv = lax.rsqrt(ms + NORM_EPS)
        for k in range(nj):
            o_ref[r, :, k * cb:(k + 1) * cb] = (zs[k] * inv * g_ref[k]).astype(o_ref.dtype)


def _hy_gate2(uh_blk, gate_blk, v_blk, y_blk, sk, g):
    nj, L, cb = y_blk.shape
    N2 = DFT_N2
    nch = L // TT_CHUNK
    gw = _pick(N2, DFT_GROUP, 16)
    ngr = N2 // gw
    blk = lambda gidx: pl.BlockSpec((nj, gw * SUBLANES, cb), lambda c, q: (gidx, c * ngr + q, 0))
    vec = pl.BlockSpec((nj, 1, cb), lambda c, q: (0, 0, 0))
    out = pl.pallas_call(
        functools.partial(_hy_gate2_body, gw=gw), grid=(nch, ngr),
        in_specs=[blk(gate_blk), blk(0), blk(0), vec, vec],
        out_specs=pl.BlockSpec((None, SUBLANES, gw, nj * cb), lambda c, q: (c, 0, q, 0)),
        out_shape=jax.ShapeDtypeStruct((nch, SUBLANES, N2, nj * cb), BF16),
        compiler_params=_cparams(("parallel", "parallel")), name="hy_gate2",
    )(uh_blk, v_blk, y_blk, sk.reshape(nj, 1, cb), g.reshape(nj, 1, cb))
    return out.reshape(L, nj * cb)


def kernel(x, norm_mix_g, w_in, hy_conv_w, hy_conv_b, hy_pos_w1, hy_pos_b1, hy_pos_w2, hy_pos_b2, hy_pos_w3, hy_pos_b3, hy_sin_freq, hy_pos_wout, hy_skip, hy_out_g, ssm_conv_w, ssm_conv_b, ssm_A_log, ssm_dt_bias, ssm_D, ssm_out_g, w_out, norm_mlp_g, w_up, w_down, final_norm_g):
    B, L, D = x.shape
    depth = w_in.shape[0]
    CH = hy_out_g.shape[1]
    WS = ssm_out_g.shape[1]
    heads = ssm_D.shape[1]
    xbc_w = ssm_conv_w.shape[2]
    groups = (xbc_w - WS) // (2 * SSM_STATE)
    o1 = 3 * CH
    o2 = o1 + WS
    o3 = o2 + xbc_w
    assert WS == heads * SSM_HEAD_DIM and heads <= LANES and L % DFT_N2 == 0 and CH % HY_CB == 0

    consts = _dft_consts(L)
    z2 = _pos_features(L, hy_pos_w1.shape[1])
    w_in_b = w_in.astype(BF16)
    w_in_dt = jnp.pad(w_in[:, :, o3:], ((0, 0), (0, 0), (0, LANES - heads))).astype(BF16)
    w_out_b = w_out.astype(BF16)
    w_up_b = w_up.astype(BF16)
    w_down_b = w_down.astype(BF16)
    pad_h = lambda v: jnp.pad(v, (0, LANES - heads)).reshape(1, LANES)
    d_rep = jnp.repeat(ssm_D, SSM_HEAD_DIM, axis=-1)

    outs = []
    for b in range(B):
        xb = x[b]
        delta = None
        for l in range(depth):
            if delta is None:
                h = _rmsnorm(xb, norm_mix_g[l], BF16)
            else:
                xb, h = _add_rmsnorm(xb, delta, norm_mix_g[l], BF16)
            proj = _matmul(h, w_in_b, l, n_cols=o3)
            dt_raw = _matmul(h, w_in_dt, l, bm_t=2048)

            nj = CH // HY_CB
            uh = _dwconv(proj, 0, o1, hy_conv_w[l], hy_conv_b[l], "none", blocked_out=True)
            hm = _filter_mlp(z2, hy_pos_w1[l], hy_pos_b1[l], hy_pos_w2[l], hy_pos_b2[l],
                             hy_pos_w3[l], hy_pos_b3[l], hy_sin_freq[l])
            g_all = _hy_filters(hm, hy_pos_wout[l], CH, L, consts)
            y0 = _hy_conv(uh, 2 * nj, nj, g_all, 0, consts)
            z1 = _hy_gate1(uh, 0, 2, y0, hy_skip[l, 0])
            y1 = _hy_conv(z1, 0, nj, g_all, 1, consts)
            y_hy = _hy_gate2(uh, 1, z1, y1, hy_skip[l, 1], hy_out_g[l])

            xbc = _dwconv(proj, o2, xbc_w, ssm_conv_w[l], ssm_conv_b[l], "silu")
            yf = _ssd(xbc, dt_raw, pad_h(ssm_dt_bias[l, 0]), pad_h(ssm_A_log[l, 0]),
                      reverse=False, heads=heads, groups=groups)
            yb = _ssd(xbc, dt_raw, pad_h(ssm_dt_bias[l, 1]), pad_h(ssm_A_log[l, 1]),
                      reverse=True, heads=heads, groups=groups)
            y_ssm = _ssm_gate(yf, yb, xbc, proj, o1, d_rep[l], ssm_out_g[l], groups)

            xb = _out_matmul(y_hy, y_ssm, w_out_b, l, xb)
            h2 = _rmsnorm(xb, norm_mlp_g[l], BF16)
            delta = _mlp(h2, w_up_b, w_down_b, l)
        outs.append(_add_rmsnorm(xb, delta, final_norm_g, F32, keep_sum=False))
    return jnp.stack(outs, axis=0)
```

```python
import functools
import math

import numpy as np
import jax
import jax.numpy as jnp
from jax import lax
from jax.experimental import pallas as pl
from jax.experimental.pallas import tpu as pltpu

F32 = jnp.float32
BF16 = jnp.bfloat16

NORM_EPS = 1e-5
SSM_HEAD_DIM = 64
SSM_STATE = 128
SSM_CHUNK = 256
HY_FAST_DECAY = 0.3
HY_SLOW_DECAY = 1.5
HY_DECAY_TARGET = 1e-2

LANES = 128
SUBLANES = 8
DFT_N2 = 128
HY_CB = 128
DFT_UNROLL = 16
DFT_GROUP = 32
TT_CHUNK = SUBLANES * DFT_N2
FILT_GROUP = 16
SPEC_DTYPE = BF16
VMEM_LIMIT_MB = 56
CONV_VMEM_MB = 60


def _cparams(sem, vmem_mb=VMEM_LIMIT_MB):
    return pltpu.CompilerParams(dimension_semantics=sem, vmem_limit_bytes=vmem_mb * 1024 * 1024)


def _pick(n, target, mult):
    best = None
    for d in range(mult, min(n, target) + 1, mult):
        if n % d == 0:
            best = d
    assert best is not None, (n, target, mult)
    return best


def _dot(a, b):
    return jnp.dot(a, b, preferred_element_type=F32)


def _split3(a):
    hi = a.astype(BF16)
    r1 = a - hi.astype(F32)
    mid = r1.astype(BF16)
    lo = (r1 - mid.astype(F32)).astype(BF16)
    return hi, mid, lo


def _dot_f32(a, b):
    ah = a.astype(BF16)
    al = (a - ah.astype(F32)).astype(BF16)
    bh = b.astype(BF16)
    bl = (b - bh.astype(F32)).astype(BF16)
    return _dot(ah, bh) + (_dot(ah, bl) + _dot(al, bh))


def _silu(x):
    return x / (1.0 + jnp.exp(-x))


def _rmsnorm_body(x_ref, g_ref, h_ref):
    x = x_ref[...]
    ms = jnp.mean(x * x, axis=-1, keepdims=True)
    h_ref[...] = (x * lax.rsqrt(ms + NORM_EPS) * g_ref[...]).astype(h_ref.dtype)


def _rmsnorm(x, g, out_dtype):
    M, D = x.shape
    bm = _pick(M, 256, SUBLANES)
    return pl.pallas_call(
        _rmsnorm_body, grid=(M // bm,),
        in_specs=[pl.BlockSpec((bm, D), lambda i: (i, 0)), pl.BlockSpec((1, D), lambda i: (0, 0))],
        out_specs=pl.BlockSpec((bm, D), lambda i: (i, 0)),
        out_shape=jax.ShapeDtypeStruct((M, D), out_dtype),
        compiler_params=_cparams(("parallel",)), name="rmsnorm",
    )(x, g.reshape(1, D))


def _add_rmsnorm_body(x_ref, d_ref, g_ref, *out_refs):
    x = x_ref[...] + d_ref[...]
    if len(out_refs) == 2:
        out_refs[0][...] = x
    h_ref = out_refs[-1]
    ms = jnp.mean(x * x, axis=-1, keepdims=True)
    h_ref[...] = (x * lax.rsqrt(ms + NORM_EPS) * g_ref[...]).astype(h_ref.dtype)


def _add_rmsnorm(x, delta, g, out_dtype, keep_sum=True):
    M, D = x.shape
    bm = _pick(M, 256, SUBLANES)
    row = pl.BlockSpec((bm, D), lambda i: (i, 0))
    h_shape = jax.ShapeDtypeStruct((M, D), out_dtype)
    return pl.pallas_call(
        _add_rmsnorm_body, grid=(M // bm,),
        in_specs=[row, row, pl.BlockSpec((1, D), lambda i: (0, 0))],
        out_specs=[row, row] if keep_sum else row,
        out_shape=[jax.ShapeDtypeStruct((M, D), F32), h_shape] if keep_sum else h_shape,
        compiler_params=_cparams(("parallel",)), name="add_rmsnorm",
    )(x, delta, g.reshape(1, D))


def _mm_body(a_ref, b_ref, o_ref):
    o_ref[...] = _dot(a_ref[...], b_ref[...]).astype(o_ref.dtype)


def _matmul(a, b_stack, layer, n_cols=None, out_dtype=F32, bm_t=1024, bn_t=1024):
    M, K = a.shape
    N = b_stack.shape[2] if n_cols is None else n_cols
    bm = _pick(M, bm_t, SUBLANES)
    bn = _pick(N, bn_t, LANES)
    return pl.pallas_call(
        _mm_body, grid=(M // bm, N // bn),
        in_specs=[pl.BlockSpec((bm, K), lambda i, j: (i, 0)),
                  pl.BlockSpec((None, K, bn), lambda i, j: (layer, 0, j))],
        out_specs=pl.BlockSpec((bm, bn), lambda i, j: (i, j)),
        out_shape=jax.ShapeDtypeStruct((M, N), out_dtype),
        compiler_params=_cparams(("parallel", "parallel")), name="proj_matmul",
    )(a, b_stack)


def _out_mm_body(a1_ref, a2_ref, b1_ref, b2_ref, r_ref, o_ref):
    acc = _dot(a1_ref[...], b1_ref[...]) + _dot(a2_ref[...], b2_ref[...])
    o_ref[...] = r_ref[...] + acc


def _out_matmul(a1, a2, b_stack, layer, res):
    M, K1 = a1.shape
    assert a2.shape[1] == K1 and b_stack.shape[1] == 2 * K1
    N = b_stack.shape[2]
    bm = _pick(M, 1024, SUBLANES)
    bn = _pick(N, 1024, LANES)
    return pl.pallas_call(
        _out_mm_body, grid=(M // bm, N // bn),
        in_specs=[pl.BlockSpec((bm, K1), lambda i, j: (i, 0)), pl.BlockSpec((bm, K1), lambda i, j: (i, 0)),
                  pl.BlockSpec((None, K1, bn), lambda i, j: (layer, 0, j)),
                  pl.BlockSpec((None, K1, bn), lambda i, j: (layer, 1, j)),
                  pl.BlockSpec((bm, bn), lambda i, j: (i, j))],
        out_specs=pl.BlockSpec((bm, bn), lambda i, j: (i, j)),
        out_shape=jax.ShapeDtypeStruct((M, N), F32),
        compiler_params=_cparams(("parallel", "parallel")), name="out_matmul",
    )(a1, a2, b_stack, b_stack, res)


def _mlp_body(h_ref, wu_ref, wd_ref, o_ref, *, cw):
    j = pl.program_id(1)

    @pl.when(j == 0)
    def _():
        o_ref[...] = jnp.zeros_like(o_ref)

    hw = wu_ref.shape[1] // 2
    ts = []
    for q in range(2):
        t = _dot(h_ref[...], wu_ref[:, q * hw:(q + 1) * hw])
        ts.append(jnp.square(jnp.maximum(t, 0.0)).astype(BF16))
    for c in range(o_ref.shape[1] // cw):
        cols = slice(c * cw, (c + 1) * cw)
        o_ref[:, cols] += _dot(ts[0], wd_ref[0:hw, cols]) + _dot(ts[1], wd_ref[hw:2 * hw, cols])


def _mlp(h, w_up_stack, w_down_stack, layer):
    M, D = h.shape
    F = w_up_stack.shape[2]
    bm = _pick(M, 512, SUBLANES)
    bf = _pick(F, 1024, LANES)
    cw = _pick(D, 512, LANES)
    return pl.pallas_call(
        functools.partial(_mlp_body, cw=cw), grid=(M // bm, F // bf),
        in_specs=[pl.BlockSpec((bm, D), lambda i, j: (i, 0), pipeline_mode=pl.Buffered(1)),
                  pl.BlockSpec((None, D, bf), lambda i, j: (layer, 0, j)),
                  pl.BlockSpec((None, bf, D), lambda i, j: (layer, j, 0))],
        out_specs=pl.BlockSpec((bm, D), lambda i, j: (i, 0)),
        out_shape=jax.ShapeDtypeStruct((M, D), F32),
        compiler_params=_cparams(("parallel", "arbitrary"), vmem_mb=CONV_VMEM_MB), name="mlp",
    )(h, w_up_stack, w_down_stack)


def _dwconv_body(p_ref, pv_ref, nx_ref, w_ref, b_ref, o_ref, *, act):
    i = pl.program_id(0)
    last = pl.num_programs(0) - 1
    u = p_ref[...]
    bm = u.shape[0]
    prev_row = jnp.where(i > 0, pv_ref[SUBLANES - 1:SUBLANES, :], 0.0)
    next_row = jnp.where(i < last, nx_ref[0:1, :], 0.0)
    rows = lax.broadcasted_iota(jnp.int32, u.shape, 0)
    um = jnp.where(rows == 0, prev_row, pltpu.roll(u, 1, 0))
    up = jnp.where(rows == bm - 1, next_row, pltpu.roll(u, bm - 1, 0))
    y = w_ref[0:1, :] * um
    y = y + w_ref[1:2, :] * u
    y = y + w_ref[2:3, :] * up
    y = y + b_ref[...]
    if act == "silu":
        y = _silu(y)
    if len(o_ref.shape) == 3:
        assert bm == TT_CHUNK
        for k in range(o_ref.shape[0]):
            for r in range(SUBLANES):
                o_ref.at[k][pl.ds(r, DFT_N2, stride=SUBLANES), :] = (
                    y[r * DFT_N2:(r + 1) * DFT_N2, k * LANES:(k + 1) * LANES])
    else:
        o_ref[...] = y


def _dwconv(p, col_off, C, w, b, act, blocked_out=False):
    L = p.shape[0]
    bm = TT_CHUNK if blocked_out else _pick(L, 1024, SUBLANES)
    assert L % bm == 0
    bc = _pick(math.gcd(C, col_off) if col_off else C, 1024, LANES)
    co = col_off // bc
    hb = bm // SUBLANES
    nrb = L // SUBLANES
    if blocked_out:
        spb = bc // LANES
        out_spec = pl.BlockSpec((spb, bm, LANES), lambda i, j: (j, i, 0))
        out_shape = jax.ShapeDtypeStruct((C // LANES, L, LANES), F32)
    else:
        out_spec = pl.BlockSpec((bm, bc), lambda i, j: (i, j))
        out_shape = jax.ShapeDtypeStruct((L, C), F32)
    return pl.pallas_call(
        functools.partial(_dwconv_body, act=act), grid=(L // bm, C // bc),
        in_specs=[pl.BlockSpec((bm, bc), lambda i, j: (i, co + j)),
                  pl.BlockSpec((SUBLANES, bc), lambda i, j: (jnp.maximum(i * hb - 1, 0), co + j)),
                  pl.BlockSpec((SUBLANES, bc), lambda i, j: (jnp.minimum((i + 1) * hb, nrb - 1), co + j)),
                  pl.BlockSpec((3, bc), lambda i, j: (0, j)),
                  pl.BlockSpec((1, bc), lambda i, j: (0, j))],
        out_specs=out_spec, out_shape=out_shape,
        compiler_params=_cparams(("parallel", "parallel")), name="dwconv_" + act,
    )(p, p, p, w, b.reshape(1, C))


def _expand_heads(v, ex_ref):
    hi = v.astype(BF16)
    lo = (v - hi.astype(F32)).astype(BF16)
    return _dot(hi, ex_ref[...]) + _dot(lo, ex_ref[...])


def _ssd_body(xbc_ref, dt_ref, bias_ref, alog_ref, ex_ref, *rest, reverse, heads, groups, gated):
    if gated:
        yo_ref, z_ref, d_ref, gn_ref, y_ref, s_ref = rest
    else:
        y_ref, s_ref = rest
    Q = SSM_CHUNK
    P = SSM_HEAD_DIM
    N = SSM_STATE
    R = heads // groups
    W = heads * P
    step = pl.program_id(0)

    @pl.when(step == 0)
    def _():
        s_ref[...] = jnp.zeros_like(s_ref)

    x = dt_ref[...] + bias_ref[...]
    dt = jnp.maximum(x, 0.0) + jnp.log1p(jnp.exp(-jnp.abs(x)))
    a = dt * (-jnp.exp(alog_ref[...]))
    ri = lax.broadcasted_iota(jnp.int32, (Q, Q), 0)
    ci = lax.broadcasted_iota(jnp.int32, (Q, Q), 1)
    mask = (ci >= ri) if reverse else (ri >= ci)
    tri = jnp.where(mask, 1.0, 0.0).astype(BF16)
    ah, am, al = _split3(a)
    acs = _dot(tri, ah) + (_dot(tri, am) + _dot(tri, al))
    tot = acs[0:1, :] if reverse else acs[Q - 1:Q, :]
    e_all = _expand_heads(jnp.exp(acs), ex_ref)
    w_all = _expand_heads(jnp.exp(tot - acs) * dt, ex_ref)
    t_all = _expand_heads(jnp.broadcast_to(jnp.exp(tot), (SUBLANES, LANES)), ex_ref)[0:1, :]
    acs_row = acs.T
    dt_row = dt.T

    load_b = lambda g: xbc_ref[:, W + g * N:W + (g + 1) * N].astype(BF16)
    load_c = lambda g: xbc_ref[:, W + groups * N + g * N:W + groups * N + (g + 1) * N].astype(BF16)
    load_x = lambda g: xbc_ref[:, g * R * P:(g + 1) * R * P]
    lane = lax.broadcasted_iota(jnp.int32, (Q, R * P), 1)

    cbs, y_offs = [], []
    for g in range(groups):
        cg = load_c(g)
        cbs.append(lax.dot_general(cg, load_b(g), (((1,), (1,)), ((), ())), preferred_element_type=F32))
        y_offs.append(_dot(cg, s_ref[g].astype(BF16)))
    for g in range(groups):
        gl = slice(g * R * P, (g + 1) * R * P)
        s_new = lax.dot_general(load_b(g), (load_x(g) * w_all[:, gl]).astype(BF16), (((0,), (0,)), ((), ())),
                                preferred_element_type=F32)
        s_ref[g] = s_ref[g] * t_all[:, gl] + s_new
    for g in range(groups):
        xg = load_x(g)
        ws, xs = [], []
        for r in range(R):
            h = g * R + r
            diff = acs[:, h:h + 1] - acs_row[h:h + 1, :]
            lm = jnp.exp(jnp.where(mask, diff, -jnp.inf))
            ws.append((cbs[g] * lm * dt_row[h:h + 1, :]).astype(BF16))
            xs.append(jnp.where((lane >= r * P) & (lane < (r + 1) * P), xg, 0.0).astype(BF16))
        y_diag = _dot(jnp.concatenate(ws, axis=1), jnp.concatenate(xs, axis=0))
        gl = slice(g * R * P, (g + 1) * R * P)
        y = y_diag + y_offs[g] * e_all[:, gl]
        if gated:
            y = (y + yo_ref[:, gl] + d_ref[:, gl] * xg) * _silu(z_ref[:, gl])
            ms = jnp.mean(y * y, axis=-1, keepdims=True)
            y = y * lax.rsqrt(ms + NORM_EPS) * gn_ref[:, gl]
        y_ref[:, gl] = y.astype(y_ref.dtype)


def _ssd(xbc, dt_raw, dt_bias, a_log, *, reverse, heads, groups, gate_with=None):
    L = xbc.shape[0]
    Q = SSM_CHUNK
    assert L % Q == 0
    nc = L // Q
    W = heads * SSM_HEAD_DIM
    R = heads // groups
    cmap = (lambda i: (nc - 1 - i, 0)) if reverse else (lambda i: (i, 0))
    ex = np.zeros((LANES, W), np.float32)
    ex[np.arange(W) // SSM_HEAD_DIM, np.arange(W)] = 1.0
    in_specs = [pl.BlockSpec((Q, xbc.shape[1]), cmap), pl.BlockSpec((Q, LANES), cmap),
                pl.BlockSpec((1, LANES), lambda i: (0, 0)), pl.BlockSpec((1, LANES), lambda i: (0, 0)),
                pl.BlockSpec((LANES, W), lambda i: (0, 0))]
    args = [xbc, dt_raw, dt_bias, a_log, jnp.asarray(ex.astype(BF16))]
    if gate_with is not None:
        y_other, proj, z_off, d_rep, norm_g = gate_with
        assert z_off % W == 0 and W // groups == R * SSM_HEAD_DIM
        zo = z_off // W
        vec = pl.BlockSpec((1, W), lambda i: (0, 0))
        in_specs += [pl.BlockSpec((Q, W), cmap), pl.BlockSpec((Q, W), lambda i: (cmap(i)[0], zo)), vec, vec]
        args += [y_other, proj, d_rep.reshape(1, W), norm_g.reshape(1, W)]
    return pl.pallas_call(
        functools.partial(_ssd_body, reverse=reverse, heads=heads, groups=groups, gated=gate_with is not None),
        grid=(nc,), in_specs=in_specs,
        out_specs=pl.BlockSpec((Q, W), cmap),
        out_shape=jax.ShapeDtypeStruct((L, W), F32 if gate_with is None else BF16),
        scratch_shapes=[pltpu.VMEM((groups, SSM_STATE, R * SSM_HEAD_DIM), F32)],
        compiler_params=_cparams(("arbitrary",)), name="ssd_bwd" if reverse else "ssd_fwd",
    )(*args)


def _dft_consts(L):
    N2 = DFT_N2
    N = 2 * L
    H1 = L // N2
    n2 = np.arange(N2)[:, None, None]
    k1 = np.arange(H1)[None, :, None]
    n1 = np.arange(H1)[None, None, :]
    m = ((2 * k1 + 1) * (N2 * n1 + n2)) % (2 * N)
    ph = np.pi * m.astype(np.float64) / N
    m_fwd = np.concatenate([np.cos(ph), -np.sin(ph)], axis=1)
    m_inv = (2.0 / N) * np.concatenate([np.cos(ph), -np.sin(ph)], axis=1).transpose(0, 2, 1)
    sg = np.where(np.arange(H1) % 2 == 0, 1.0, -1.0)[None, :, None]
    m_filt = np.concatenate([np.concatenate([np.cos(ph), -sg * np.sin(ph)], axis=2),
                             np.concatenate([-np.sin(ph), -sg * np.cos(ph)], axis=2)], axis=1)
    kk = (np.arange(N2)[:, None] * np.arange(N2)[None, :]) % N2
    c2 = np.cos(2 * np.pi * kk / N2)
    s2 = np.sin(2 * np.pi * kk / N2)
    f2_fwd = np.block([[c2, s2], [-s2, c2]])
    f2_inv = np.block([[c2, -s2], [s2, c2]])
    as_bf = lambda a: jnp.asarray(a.astype(BF16))
    return as_bf(m_fwd), as_bf(m_inv), as_bf(f2_fwd), as_bf(f2_inv), as_bf(m_filt)


def _pitch(rows):
    p = rows + SUBLANES
    assert (p // SUBLANES) % 2 == 1
    return p


def _pack2(a, b):
    ha = lax.bitcast_convert_type(a.astype(BF16).astype(F32), jnp.uint32)
    hb = lax.bitcast_convert_type(b.astype(BF16).astype(F32), jnp.uint32)
    return ha | (hb >> 16)


def _unpack2(w):
    a = lax.bitcast_convert_type(w & jnp.uint32(0xFFFF0000), F32)
    b = lax.bitcast_convert_type(w << 16, F32)
    return jnp.concatenate([a, b], axis=1).astype(BF16)


def _split_lanes(x):
    c = x.shape[1] // 2
    return x[:, :c], x[:, c:]


def _stage3(a_s, f2_ref, k1, H1, pitch):
    N2 = DFT_N2
    ar = _unpack2(a_s[pl.ds(k1, N2, stride=pitch), :])
    ai = _unpack2(a_s[pl.ds(H1 + k1, N2, stride=pitch), :])
    return _dot(f2_ref[...], jnp.concatenate([ar, ai], axis=0))


def _conv_body(u_ref, g_ref, mf_ref, mi_ref, f2f_ref, f2i_ref, y_ref, a_s, *, H1, ks, gw, ng, nsl):
    N2, cb = DFT_N2, HY_CB
    pitch = _pitch(2 * H1)
    s = pl.program_id(1)

    @pl.when(s < ng)
    def _():
        for t in range(gw):
            n2 = s * gw + t
            rows = slice(t * SUBLANES, (t + 1) * SUBLANES)
            u = jnp.concatenate([u_ref[0, :, rows, :].reshape(H1, cb), u_ref[1, :, rows, :].reshape(H1, cb)],
                                axis=1).astype(BF16)
            a = _dot(mf_ref[n2], u)
            a_s[pl.ds(pl.multiple_of(n2 * pitch, SUBLANES), 2 * H1), :] = _pack2(*_split_lanes(a))

    @pl.when((s >= ng) & (s < ng + nsl))
    def _():
        k0 = (s - ng) * ks
        xs = [_stage3(a_s, f2f_ref, k0 + kk, H1, pitch).astype(BF16) for kk in range(ks)]
        bs = []
        for kk in range(ks):
            gr = jnp.concatenate([g_ref[0, kk, 0], g_ref[1, kk, 0]], axis=1)
            gi = jnp.concatenate([g_ref[0, kk, 1], g_ref[1, kk, 1]], axis=1)
            xr, xi = xs[kk][:N2], xs[kk][N2:]
            y = jnp.concatenate([xr * gr - xi * gi, xr * gi + xi * gr], axis=0)
            bs.append(_pack2(*_split_lanes(_dot(f2i_ref[...], y))))
        for kk in range(ks):
            a_s[pl.ds(k0 + kk, N2, stride=pitch), :] = bs[kk][:N2]
            a_s[pl.ds(H1 + k0 + kk, N2, stride=pitch), :] = bs[kk][N2:]

    @pl.when(s >= ng + nsl)
    def _():
        for t in range(gw):
            n2 = (s - ng - nsl) * gw + t
            b = _unpack2(a_s[pl.ds(pl.multiple_of(n2 * pitch, SUBLANES), 2 * H1), :])
            y = _dot(mi_ref[n2], b)
            for p, yp in enumerate(_split_lanes(y)):
                y_ref[p, :, t * SUBLANES:(t + 1) * SUBLANES, :] = yp.reshape(H1 // SUBLANES, SUBLANES, cb)


def _hy_conv(u_blk, slab_off, nj, g_all, order, consts):
    L = u_blk.shape[1]
    N2, cb = DFT_N2, HY_CB
    H1 = L // N2
    ks = _pick(H1, 16, 1)
    gw = _pick(N2, DFT_GROUP, 1)
    ng, nsl = N2 // gw, H1 // ks
    m_fwd, m_inv, f2_fwd, f2_inv, _ = consts
    assert H1 % SUBLANES == 0 and nj % 2 == 0 and slab_off % 2 == 0
    u4 = u_blk.reshape(u_blk.shape[0], H1 // SUBLANES, TT_CHUNK, cb)
    const3 = lambda shape: pl.BlockSpec(shape, lambda j, s: (0, 0, 0), pipeline_mode=pl.Buffered(1))
    y4 = pl.pallas_call(
        functools.partial(_conv_body, H1=H1, ks=ks, gw=gw, ng=ng, nsl=nsl), grid=(nj // 2, 2 * ng + nsl),
        in_specs=[pl.BlockSpec((2, H1 // SUBLANES, gw * SUBLANES, cb),
                               lambda j, s: (slab_off // 2 + j, 0, jnp.minimum(s, ng - 1), 0)),
                  pl.BlockSpec((None, 2, ks, 2, N2, cb),
                               lambda j, s: (order, j, jnp.clip(s - ng, 0, nsl - 1), 0, 0, 0)),
                  const3((N2, 2 * H1, H1)), const3((N2, H1, 2 * H1)),
                  pl.BlockSpec((2 * N2, 2 * N2), lambda j, s: (0, 0)),
                  pl.BlockSpec((2 * N2, 2 * N2), lambda j, s: (0, 0))],
        out_specs=pl.BlockSpec((2, H1 // SUBLANES, gw * SUBLANES, cb),
                               lambda j, s: (j, 0, jnp.clip(s - ng - nsl, 0, ng - 1), 0)),
        out_shape=jax.ShapeDtypeStruct((nj, H1 // SUBLANES, TT_CHUNK, cb), F32),
        scratch_shapes=[pltpu.VMEM((N2 * _pitch(2 * H1), cb), jnp.uint32)],
        compiler_params=_cparams(("parallel", "arbitrary"), vmem_mb=CONV_VMEM_MB), name="hy_conv",
    )(u4, g_all, m_fwd, m_inv, f2_fwd, f2_inv)
    return y4.reshape(nj, L, cb)


def _pos_features(L, emb):
    N2 = DFT_N2
    H1 = L // N2
    bands = (emb - 1) // 2
    r = jnp.arange(L)
    lag = (r % H1) * N2 + (r // H1)
    f = jnp.linspace(1e-4, bands - 1, bands, dtype=F32)[None, :]

    def feats(lg):
        lgf = lg.astype(F32)[:, None]
        t = lgf / (L - 1)
        w = 2.0 * math.pi * lgf / L
        z = jnp.concatenate([t, jnp.cos(f * w), -jnp.sin(f * w)], axis=-1)
        return jnp.pad(z, ((0, 0), (0, LANES - emb)))
    return jnp.concatenate([feats(lag), feats(jnp.where(lag == 0, 0, L - lag))], axis=-1)


def _filter_mlp_body(z_ref, w1_ref, b1_ref, w2_ref, b2_ref, w3_ref, b3_ref, fr_ref, o_ref):
    h = jnp.sin(fr_ref[0:1, :] * (_dot_f32(z_ref[...], w1_ref[...]) + b1_ref[...]))
    h = jnp.sin(fr_ref[1:2, :] * (_dot_f32(h, w2_ref[...]) + b2_ref[...]))
    h = jnp.sin(fr_ref[2:3, :] * (_dot_f32(h, w3_ref[...]) + b3_ref[...]))
    o_ref[...] = h.astype(o_ref.dtype)


def _bdiag(w):
    a, b = w.shape
    z = jnp.zeros((a, b), w.dtype)
    return jnp.concatenate([jnp.concatenate([w, z], axis=1), jnp.concatenate([z, w], axis=1)], axis=0)


def _filter_mlp(z2, w1, b1, w2, b2, w3, b3, fr):
    L = z2.shape[0]
    emb, hd = w1.shape
    assert 2 * hd == LANES
    w1p = jnp.zeros((2 * LANES, LANES), F32).at[:emb, :hd].set(w1).at[LANES:LANES + emb, hd:].set(w1)
    tile2 = lambda v: jnp.concatenate([v, v], axis=-1)
    bm = _pick(L, 1024, SUBLANES)
    full = lambda a: pl.BlockSpec(a.shape, lambda i: (0,) * a.ndim)
    args = (w1p, tile2(b1).reshape(1, LANES), _bdiag(w2), tile2(b2).reshape(1, LANES),
            _bdiag(w3), tile2(b3).reshape(1, LANES), tile2(fr))
    return pl.pallas_call(
        _filter_mlp_body, grid=(L // bm,),
        in_specs=[pl.BlockSpec((bm, 2 * LANES), lambda i: (i, 0))] + [full(a) for a in args],
        out_specs=pl.BlockSpec((bm, LANES), lambda i: (i, 0)),
        out_shape=jax.ShapeDtypeStruct((L, LANES), BF16),
        compiler_params=_cparams(("parallel",)), name="hy_filter_mlp",
    )(z2, *args)


def _filt_body(hm_ref, w2_ref, ad_ref, m_ref, f2_ref, g_ref, a_s, g_s, inv_s, *, L, H1, ks):
    N2 = DFT_N2
    cb = 2 * HY_CB
    pitch = _pitch(2 * H1)
    s = pl.program_id(2)

    @pl.when(s == 0)
    def _():
        rate = ad_ref[...] * (1.0 / (L - 1))
        n1 = lax.broadcasted_iota(jnp.int32, (H1, cb), 0)
        e1f = jnp.exp(-(n1 * N2).astype(F32) * rate)
        e1b = jnp.exp(-((H1 - 1 - n1) * N2).astype(F32) * rate)

        def group(gi, nrm):
            def gen(t, nrm):
                n2 = gi * FILT_GROUP + t
                hs = hm_ref[pl.ds(pl.multiple_of(n2 * H1, SUBLANES), H1), :]
                gp = _dot(hs, w2_ref[...])
                n2f = n2.astype(F32)
                gf = gp[:, :cb] * (e1f * jnp.exp(-n2f * rate))
                gb = gp[:, cb:] * (e1b * jnp.exp(-(N2 - n2f) * rate))
                gb = jnp.where(n1 + n2 == 0, 0.0, -gb)
                g_s[t, 0:H1, :] = gf.astype(BF16)
                g_s[t, H1:2 * H1, :] = gb.astype(BF16)
                return nrm + (jnp.abs(gf) + jnp.abs(gb))
            nrm = lax.fori_loop(0, FILT_GROUP, gen, nrm, unroll=DFT_UNROLL)

            def xform(t, carry):
                n2 = gi * FILT_GROUP + t
                a = _dot(m_ref[n2], g_s[t])
                a_s[pl.ds(pl.multiple_of(n2 * pitch, SUBLANES), 2 * H1), :] = _pack2(*_split_lanes(a))
                return carry
            lax.fori_loop(0, FILT_GROUP, xform, 0, unroll=DFT_UNROLL)
            return nrm
        nrm = lax.fori_loop(0, N2 // FILT_GROUP, group, jnp.zeros((H1, cb), F32))
        inv_s[...] = jnp.broadcast_to(1.0 / jnp.sum(nrm, axis=0, keepdims=True), inv_s.shape)

    scale = inv_s[0:1, :]

    def kbody(kk, carry):
        x = (_stage3(a_s, f2_ref, s * ks + kk, H1, pitch) * scale).astype(g_ref.dtype)
        for p, xp in enumerate(_split_lanes(x)):
            g_ref[p, kk, 0] = xp[:N2]
            g_ref[p, kk, 1] = xp[N2:]
        return carry
    lax.fori_loop(0, ks, kbody, 0, unroll=DFT_UNROLL)


def _hy_filters(hm, wout, C, L, consts):
    N2, cb = DFT_N2, HY_CB
    H1 = L // N2
    ks = _pick(H1, 16, 1)
    hd = wout.shape[0]
    nj = C // cb
    _, _, f2_fwd, _, m_filt = consts
    assert nj % 2 == 0
    pw = 2 * cb
    w4 = wout.reshape(hd, 2, 2, nj // 2, pw)
    wf = jnp.transpose(w4[:, :, 0], (1, 2, 0, 3))
    wb = jnp.transpose(w4[:, :, 1], (1, 2, 0, 3))
    z = jnp.zeros_like(wf)
    w2 = jnp.concatenate([jnp.concatenate([wf, z], axis=-1), jnp.concatenate([z, wb], axis=-1)], axis=2)
    w2 = w2.astype(BF16)
    deltas = jnp.linspace(math.log(HY_FAST_DECAY) / HY_DECAY_TARGET,
                          math.log(HY_SLOW_DECAY) / HY_DECAY_TARGET, C, dtype=F32)
    absd = jnp.abs(deltas).reshape(1, C)
    return pl.pallas_call(
        functools.partial(_filt_body, L=L, H1=H1, ks=ks), grid=(2, nj // 2, H1 // ks),
        in_specs=[pl.BlockSpec((L, LANES), lambda o, j, s: (0, 0)),
                  pl.BlockSpec((None, None, 2 * hd, 2 * pw), lambda o, j, s: (o, j, 0, 0)),
                  pl.BlockSpec((1, pw), lambda o, j, s: (0, j)),
                  pl.BlockSpec((N2, 2 * H1, 2 * H1), lambda o, j, s: (0, 0, 0), pipeline_mode=pl.Buffered(1)),
                  pl.BlockSpec((2 * N2, 2 * N2), lambda o, j, s: (0, 0))],
        out_specs=pl.BlockSpec((None, 2, ks, 2, N2, cb), lambda o, j, s: (o, j, s, 0, 0, 0)),
        out_shape=jax.ShapeDtypeStruct((2, nj, H1, 2, N2, cb), SPEC_DTYPE),
        scratch_shapes=[pltpu.VMEM((N2 * _pitch(2 * H1), cb), jnp.uint32),
                        pltpu.VMEM((FILT_GROUP, 2 * H1, pw), BF16), pltpu.VMEM((SUBLANES, pw), F32)],
        compiler_params=_cparams(("parallel", "parallel", "arbitrary")), name="hy_filters",
    )(hm, w2, absd, m_filt, f2_fwd)


def _hy_gate1_body(gate_ref, v_ref, y_ref, sk_ref, o_ref):
    o_ref[...] = gate_ref[...] * (y_ref[...] + sk_ref[...] * v_ref[...])


def _hy_gate1(uh_blk, gate_blk, v_blk, y_blk, sk):
    nj, L, cb = y_blk.shape
    bm = _pick(L, 256, SUBLANES)
    blk = lambda g: pl.BlockSpec((nj, bm, cb), lambda i: (g, i, 0))
    return pl.pallas_call(
        _hy_gate1_body, grid=(L // bm,),
        in_specs=[blk(gate_blk), blk(v_blk), blk(0), pl.BlockSpec((nj, 1, cb), lambda i: (0, 0, 0))],
        out_specs=blk(0), out_shape=jax.ShapeDtypeStruct((nj, L, cb), F32),
        compiler_params=_cparams(("parallel",)), name="hy_gate1",
    )(uh_blk, uh_blk, y_blk, sk.reshape(nj, 1, cb))


def _hy_gate2_body(gate_ref, v_ref, y_ref, sk_ref, g_ref, o_ref, *, gw):
    nj, _, cb = gate_ref.shape
    for r in range(SUBLANES):
        rows = pl.ds(r, gw, stride=SUBLANES)
        zs = []
        ss = jnp.zeros((gw, cb), F32)
        for k in range(nj):
            z = gate_ref.at[k][rows, :] * (y_ref.at[k][rows, :] + sk_ref[k] * v_ref.at[k][rows, :])
            zs.append(z)
            ss = ss + z * z
        ms = jnp.sum(ss, axis=-1, keepdims=True) * (1.0 / (nj * cb))
        inv = lax.rsqrt(ms + NORM_EPS)
        for k in range(nj):
            o_ref[r, :, k * cb:(k + 1) * cb] = (zs[k] * inv * g_ref[k]).astype(o_ref.dtype)


def _hy_gate2(uh_blk, gate_blk, v_blk, y_blk, sk, g):
    nj, L, cb = y_blk.shape
    N2 = DFT_N2
    nch = L // TT_CHUNK
    gw = _pick(N2, DFT_GROUP, 16)
    ngr = N2 // gw
    blk = lambda gidx: pl.BlockSpec((nj, gw * SUBLANES, cb), lambda c, q: (gidx, c * ngr + q, 0))
    vec = pl.BlockSpec((nj, 1, cb), lambda c, q: (0, 0, 0))
    out = pl.pallas_call(
        functools.partial(_hy_gate2_body, gw=gw), grid=(nch, ngr),
        in_specs=[blk(gate_blk), blk(0), blk(0), vec, vec],
        out_specs=pl.BlockSpec((None, SUBLANES, gw, nj * cb), lambda c, q: (c, 0, q, 0)),
        out_shape=jax.ShapeDtypeStruct((nch, SUBLANES, N2, nj * cb), BF16),
        compiler_params=_cparams(("parallel", "parallel")), name="hy_gate2",
    )(uh_blk, v_blk, y_blk, sk.reshape(nj, 1, cb), g.reshape(nj, 1, cb))
    return out.reshape(L, nj * cb)


def kernel(x, norm_mix_g, w_in, hy_conv_w, hy_conv_b, hy_pos_w1, hy_pos_b1, hy_pos_w2, hy_pos_b2, hy_pos_w3, hy_pos_b3, hy_sin_freq, hy_pos_wout, hy_skip, hy_out_g, ssm_conv_w, ssm_conv_b, ssm_A_log, ssm_dt_bias, ssm_D, ssm_out_g, w_out, norm_mlp_g, w_up, w_down, final_norm_g):
    B, L, D = x.shape
    depth = w_in.shape[0]
    CH = hy_out_g.shape[1]
    WS = ssm_out_g.shape[1]
    heads = ssm_D.shape[1]
    xbc_w = ssm_conv_w.shape[2]
    groups = (xbc_w - WS) // (2 * SSM_STATE)
    o1 = 3 * CH
    o2 = o1 + WS
    o3 = o2 + xbc_w
    assert WS == heads * SSM_HEAD_DIM and heads <= LANES and L % DFT_N2 == 0 and CH % HY_CB == 0

    consts = _dft_consts(L)
    z2 = _pos_features(L, hy_pos_w1.shape[1])
    w_in_b = w_in.astype(BF16)
    w_in_dt = jnp.pad(w_in[:, :, o3:], ((0, 0), (0, 0), (0, LANES - heads))).astype(BF16)
    w_out_b = w_out.astype(BF16)
    w_up_b = w_up.astype(BF16)
    w_down_b = w_down.astype(BF16)
    pad_h = lambda v: jnp.pad(v, (0, LANES - heads)).reshape(1, LANES)
    d_rep = jnp.repeat(ssm_D, SSM_HEAD_DIM, axis=-1)

    outs = []
    for b in range(B):
        xb = x[b]
        delta = None
        for l in range(depth):
            if delta is None:
                h = _rmsnorm(xb, norm_mix_g[l], BF16)
            else:
                xb, h = _add_rmsnorm(xb, delta, norm_mix_g[l], BF16)
            proj = _matmul(h, w_in_b, l, n_cols=o3)
            dt_raw = _matmul(h, w_in_dt, l, bm_t=2048)

            nj = CH // HY_CB
            uh = _dwconv(proj, 0, o1, hy_conv_w[l], hy_conv_b[l], "none", blocked_out=True)
            hm = _filter_mlp(z2, hy_pos_w1[l], hy_pos_b1[l], hy_pos_w2[l], hy_pos_b2[l],
                             hy_pos_w3[l], hy_pos_b3[l], hy_sin_freq[l])
            g_all = _hy_filters(hm, hy_pos_wout[l], CH, L, consts)
            y0 = _hy_conv(uh, 2 * nj, nj, g_all, 0, consts)
            z1 = _hy_gate1(uh, 0, 2, y0, hy_skip[l, 0])
            y1 = _hy_conv(z1, 0, nj, g_all, 1, consts)
            y_hy = _hy_gate2(uh, 1, z1, y1, hy_skip[l, 1], hy_out_g[l])

            xbc = _dwconv(proj, o2, xbc_w, ssm_conv_w[l], ssm_conv_b[l], "silu")
            yf = _ssd(xbc, dt_raw, pad_h(ssm_dt_bias[l, 0]), pad_h(ssm_A_log[l, 0]),
                      reverse=False, heads=heads, groups=groups)
            y_ssm = _ssd(xbc, dt_raw, pad_h(ssm_dt_bias[l, 1]), pad_h(ssm_A_log[l, 1]),
                         reverse=True, heads=heads, groups=groups,
                         gate_with=(yf, proj, o1, d_rep[l], ssm_out_g[l]))

            xb = _out_matmul(y_hy, y_ssm, w_out_b, l, xb)
            h2 = _rmsnorm(xb, norm_mlp_g[l], BF16)
            delta = _mlp(h2, w_up_b, w_down_b, l)
        outs.append(_add_rmsnorm(xb, delta, final_norm_g, F32, keep_sum=False))
    return jnp.stack(outs, axis=0)
```

```python
import functools
import math

import numpy as np
import jax
import jax.numpy as jnp
from jax import lax
from jax.experimental import pallas as pl
from jax.experimental.pallas import tpu as pltpu

F32 = jnp.float32
BF16 = jnp.bfloat16

NORM_EPS = 1e-5
SSM_HEAD_DIM = 64
SSM_STATE = 128
SSM_CHUNK = 256
HY_FAST_DECAY = 0.3
HY_SLOW_DECAY = 1.5
HY_DECAY_TARGET = 1e-2

LANES = 128
SUBLANES = 8
DFT_N2 = 128
HY_CB = 128
DFT_UNROLL = 16
DFT_GROUP = 32
TT_CHUNK = SUBLANES * DFT_N2
FILT_GROUP = 16
SPEC_DTYPE = BF16
PROJ_DTYPE = BF16
VMEM_LIMIT_MB = 56
CONV_VMEM_MB = 60


def _cparams(sem, vmem_mb=VMEM_LIMIT_MB):
    return pltpu.CompilerParams(dimension_semantics=sem, vmem_limit_bytes=vmem_mb * 1024 * 1024)


def _pick(n, target, mult):
    best = None
    for d in range(mult, min(n, target) + 1, mult):
        if n % d == 0:
            best = d
    assert best is not None, (n, target, mult)
    return best


def _dot(a, b):
    return jnp.dot(a, b, preferred_element_type=F32)


def _split3(a):
    hi = a.astype(BF16)
    r1 = a - hi.astype(F32)
    mid = r1.astype(BF16)
    lo = (r1 - mid.astype(F32)).astype(BF16)
    return hi, mid, lo


def _dot_f32(a, b):
    ah = a.astype(BF16)
    al = (a - ah.astype(F32)).astype(BF16)
    bh = b.astype(BF16)
    bl = (b - bh.astype(F32)).astype(BF16)
    return _dot(ah, bh) + (_dot(ah, bl) + _dot(al, bh))


def _silu(x):
    return x / (1.0 + jnp.exp(-x))


def _rmsnorm_body(x_ref, g_ref, h_ref):
    x = x_ref[...]
    ms = jnp.mean(x * x, axis=-1, keepdims=True)
    h_ref[...] = (x * lax.rsqrt(ms + NORM_EPS) * g_ref[...]).astype(h_ref.dtype)


def _rmsnorm(x, g, out_dtype):
    M, D = x.shape
    bm = _pick(M, 256, SUBLANES)
    return pl.pallas_call(
        _rmsnorm_body, grid=(M // bm,),
        in_specs=[pl.BlockSpec((bm, D), lambda i: (i, 0)), pl.BlockSpec((1, D), lambda i: (0, 0))],
        out_specs=pl.BlockSpec((bm, D), lambda i: (i, 0)),
        out_shape=jax.ShapeDtypeStruct((M, D), out_dtype),
        compiler_params=_cparams(("parallel",)), name="rmsnorm",
    )(x, g.reshape(1, D))


def _add_rmsnorm_body(x_ref, d_ref, g_ref, *out_refs):
    x = x_ref[...] + d_ref[...]
    if len(out_refs) == 2:
        out_refs[0][...] = x
    h_ref = out_refs[-1]
    ms = jnp.mean(x * x, axis=-1, keepdims=True)
    h_ref[...] = (x * lax.rsqrt(ms + NORM_EPS) * g_ref[...]).astype(h_ref.dtype)


def _add_rmsnorm(x, delta, g, out_dtype, keep_sum=True):
    M, D = x.shape
    bm = _pick(M, 256, SUBLANES)
    row = pl.BlockSpec((bm, D), lambda i: (i, 0))
    h_shape = jax.ShapeDtypeStruct((M, D), out_dtype)
    return pl.pallas_call(
        _add_rmsnorm_body, grid=(M // bm,),
        in_specs=[row, row, pl.BlockSpec((1, D), lambda i: (0, 0))],
        out_specs=[row, row] if keep_sum else row,
        out_shape=[jax.ShapeDtypeStruct((M, D), F32), h_shape] if keep_sum else h_shape,
        compiler_params=_cparams(("parallel",)), name="add_rmsnorm",
    )(x, delta, g.reshape(1, D))


def _mm_body(a_ref, b_ref, o_ref):
    o_ref[...] = _dot(a_ref[...], b_ref[...]).astype(o_ref.dtype)


def _matmul(a, b_stack, layer, n_cols=None, out_dtype=F32, bm_t=1024, bn_t=1024):
    M, K = a.shape
    N = b_stack.shape[2] if n_cols is None else n_cols
    bm = _pick(M, bm_t, SUBLANES)
    bn = _pick(N, bn_t, LANES)
    return pl.pallas_call(
        _mm_body, grid=(M // bm, N // bn),
        in_specs=[pl.BlockSpec((bm, K), lambda i, j: (i, 0)),
                  pl.BlockSpec((None, K, bn), lambda i, j: (layer, 0, j))],
        out_specs=pl.BlockSpec((bm, bn), lambda i, j: (i, j)),
        out_shape=jax.ShapeDtypeStruct((M, N), out_dtype),
        compiler_params=_cparams(("parallel", "parallel")), name="proj_matmul",
    )(a, b_stack)


def _out_mm_body(a1_ref, a2_ref, b1_ref, b2_ref, r_ref, o_ref):
    acc = _dot(a1_ref[...], b1_ref[...]) + _dot(a2_ref[...], b2_ref[...])
    o_ref[...] = r_ref[...] + acc


def _out_matmul(a1, a2, b_stack, layer, res):
    M, K1 = a1.shape
    assert a2.shape[1] == K1 and b_stack.shape[1] == 2 * K1
    N = b_stack.shape[2]
    bm = _pick(M, 1024, SUBLANES)
    bn = _pick(N, 1024, LANES)
    return pl.pallas_call(
        _out_mm_body, grid=(M // bm, N // bn),
        in_specs=[pl.BlockSpec((bm, K1), lambda i, j: (i, 0)), pl.BlockSpec((bm, K1), lambda i, j: (i, 0)),
                  pl.BlockSpec((None, K1, bn), lambda i, j: (layer, 0, j)),
                  pl.BlockSpec((None, K1, bn), lambda i, j: (layer, 1, j)),
                  pl.BlockSpec((bm, bn), lambda i, j: (i, j))],
        out_specs=pl.BlockSpec((bm, bn), lambda i, j: (i, j)),
        out_shape=jax.ShapeDtypeStruct((M, N), F32),
        compiler_params=_cparams(("parallel", "parallel")), name="out_matmul",
    )(a1, a2, b_stack, b_stack, res)


def _mlp_body(h_ref, wu_ref, wd_ref, o_ref, *, cw):
    j = pl.program_id(1)

    @pl.when(j == 0)
    def _():
        o_ref[...] = jnp.zeros_like(o_ref)

    hw = wu_ref.shape[1] // 2
    ts = []
    for q in range(2):
        t = _dot(h_ref[...], wu_ref[:, q * hw:(q + 1) * hw])
        ts.append(jnp.square(jnp.maximum(t, 0.0)).astype(BF16))
    for c in range(o_ref.shape[1] // cw):
        cols = slice(c * cw, (c + 1) * cw)
        o_ref[:, cols] += _dot(ts[0], wd_ref[0:hw, cols]) + _dot(ts[1], wd_ref[hw:2 * hw, cols])


def _mlp(h, w_up_stack, w_down_stack, layer):
    M, D = h.shape
    F = w_up_stack.shape[2]
    bm = _pick(M, 512, SUBLANES)
    bf = _pick(F, 512, LANES)
    cw = _pick(D, 512, LANES)
    return pl.pallas_call(
        functools.partial(_mlp_body, cw=cw), grid=(M // bm, F // bf),
        in_specs=[pl.BlockSpec((bm, D), lambda i, j: (i, 0)),
                  pl.BlockSpec((None, D, bf), lambda i, j: (layer, 0, j)),
                  pl.BlockSpec((None, bf, D), lambda i, j: (layer, j, 0))],
        out_specs=pl.BlockSpec((bm, D), lambda i, j: (i, 0)),
        out_shape=jax.ShapeDtypeStruct((M, D), F32),
        compiler_params=_cparams(("parallel", "arbitrary")), name="mlp",
    )(h, w_up_stack, w_down_stack)


def _dwconv_body(p_ref, pv_ref, nx_ref, w_ref, b_ref, o_ref, *, act):
    i = pl.program_id(0)
    last = pl.num_programs(0) - 1
    u = p_ref[...].astype(F32)
    bm = u.shape[0]
    hr = pv_ref.shape[0]
    prev_row = jnp.where(i > 0, pv_ref[hr - 1:hr, :].astype(F32), 0.0)
    next_row = jnp.where(i < last, nx_ref[0:1, :].astype(F32), 0.0)
    rows = lax.broadcasted_iota(jnp.int32, u.shape, 0)
    um = jnp.where(rows == 0, prev_row, pltpu.roll(u, 1, 0))
    up = jnp.where(rows == bm - 1, next_row, pltpu.roll(u, bm - 1, 0))
    y = w_ref[0:1, :] * um
    y = y + w_ref[1:2, :] * u
    y = y + w_ref[2:3, :] * up
    y = y + b_ref[...]
    if act == "silu":
        y = _silu(y)
    if len(o_ref.shape) == 3:
        assert bm == TT_CHUNK
        for k in range(o_ref.shape[0]):
            for r in range(SUBLANES):
                o_ref.at[k][pl.ds(r, DFT_N2, stride=SUBLANES), :] = (
                    y[r * DFT_N2:(r + 1) * DFT_N2, k * LANES:(k + 1) * LANES])
    else:
        o_ref[...] = y


def _dwconv(p, col_off, C, w, b, act, blocked_out=False):
    L = p.shape[0]
    bm = TT_CHUNK if blocked_out else _pick(L, 1024, SUBLANES)
    assert L % bm == 0
    bc = _pick(math.gcd(C, col_off) if col_off else C, 1024, LANES)
    co = col_off // bc
    hr = SUBLANES * (4 // p.dtype.itemsize)
    hb = bm // hr
    nrb = L // hr
    if blocked_out:
        spb = bc // LANES
        out_spec = pl.BlockSpec((spb, bm, LANES), lambda i, j: (j, i, 0))
        out_shape = jax.ShapeDtypeStruct((C // LANES, L, LANES), F32)
    else:
        out_spec = pl.BlockSpec((bm, bc), lambda i, j: (i, j))
        out_shape = jax.ShapeDtypeStruct((L, C), F32)
    return pl.pallas_call(
        functools.partial(_dwconv_body, act=act), grid=(L // bm, C // bc),
        in_specs=[pl.BlockSpec((bm, bc), lambda i, j: (i, co + j)),
                  pl.BlockSpec((hr, bc), lambda i, j: (jnp.maximum(i * hb - 1, 0), co + j)),
                  pl.BlockSpec((hr, bc), lambda i, j: (jnp.minimum((i + 1) * hb, nrb - 1), co + j)),
                  pl.BlockSpec((3, bc), lambda i, j: (0, j)),
                  pl.BlockSpec((1, bc), lambda i, j: (0, j))],
        out_specs=out_spec, out_shape=out_shape,
        compiler_params=_cparams(("parallel", "parallel")), name="dwconv_" + act,
    )(p, p, p, w, b.reshape(1, C))


def _expand_heads(v, ex_ref):
    hi = v.astype(BF16)
    lo = (v - hi.astype(F32)).astype(BF16)
    return _dot(hi, ex_ref[...]) + _dot(lo, ex_ref[...])


def _ssd_body(xbc_ref, dt_ref, bias_ref, alog_ref, ex_ref, *rest, reverse, heads, groups, gated):
    if gated:
        yo_ref, z_ref, d_ref, gn_ref, y_ref, s_ref = rest
    else:
        y_ref, s_ref = rest
    Q = SSM_CHUNK
    P = SSM_HEAD_DIM
    N = SSM_STATE
    R = heads // groups
    W = heads * P
    step = pl.program_id(0)

    @pl.when(step == 0)
    def _():
        s_ref[...] = jnp.zeros_like(s_ref)

    x = dt_ref[...] + bias_ref[...]
    dt = jnp.maximum(x, 0.0) + jnp.log1p(jnp.exp(-jnp.abs(x)))
    a = dt * (-jnp.exp(alog_ref[...]))
    ri = lax.broadcasted_iota(jnp.int32, (Q, Q), 0)
    ci = lax.broadcasted_iota(jnp.int32, (Q, Q), 1)
    mask = (ci >= ri) if reverse else (ri >= ci)
    tri = jnp.where(mask, 1.0, 0.0).astype(BF16)
    ah, am, al = _split3(a)
    acs = _dot(tri, ah) + (_dot(tri, am) + _dot(tri, al))
    tot = acs[0:1, :] if reverse else acs[Q - 1:Q, :]
    e_all = _expand_heads(jnp.exp(acs), ex_ref)
    w_all = _expand_heads(jnp.exp(tot - acs) * dt, ex_ref)
    t_all = _expand_heads(jnp.broadcast_to(jnp.exp(tot), (SUBLANES, LANES)), ex_ref)[0:1, :]
    acs_row = acs.T
    dt_row = dt.T

    load_b = lambda g: xbc_ref[:, W + g * N:W + (g + 1) * N].astype(BF16)
    load_c = lambda g: xbc_ref[:, W + groups * N + g * N:W + groups * N + (g + 1) * N].astype(BF16)
    load_x = lambda g: xbc_ref[:, g * R * P:(g + 1) * R * P]
    lane = lax.broadcasted_iota(jnp.int32, (Q, R * P), 1)

    cbs, y_offs = [], []
    for g in range(groups):
        cg = load_c(g)
        cbs.append(lax.dot_general(cg, load_b(g), (((1,), (1,)), ((), ())), preferred_element_type=F32))
        y_offs.append(_dot(cg, s_ref[g].astype(BF16)))
    for g in range(groups):
        gl = slice(g * R * P, (g + 1) * R * P)
        s_new = lax.dot_general(load_b(g), (load_x(g) * w_all[:, gl]).astype(BF16), (((0,), (0,)), ((), ())),
                                preferred_element_type=F32)
        s_ref[g] = s_ref[g] * t_all[:, gl] + s_new
    for g in range(groups):
        xg = load_x(g)
        ws, xs = [], []
        for r in range(R):
            h = g * R + r
            diff = acs[:, h:h + 1] - acs_row[h:h + 1, :]
            lm = jnp.exp(jnp.where(mask, diff, -jnp.inf))
            ws.append((cbs[g] * lm * dt_row[h:h + 1, :]).astype(BF16))
            xs.append(jnp.where((lane >= r * P) & (lane < (r + 1) * P), xg, 0.0).astype(BF16))
        y_diag = _dot(jnp.concatenate(ws, axis=1), jnp.concatenate(xs, axis=0))
        gl = slice(g * R * P, (g + 1) * R * P)
        y = y_diag + y_offs[g] * e_all[:, gl]
        if gated:
            y = (y + yo_ref[:, gl] + d_ref[:, gl] * xg) * _silu(z_ref[:, gl].astype(F32))
            ms = jnp.mean(y * y, axis=-1, keepdims=True)
            y = y * lax.rsqrt(ms + NORM_EPS) * gn_ref[:, gl]
        y_ref[:, gl] = y.astype(y_ref.dtype)


def _ssd(xbc, dt_raw, dt_bias, a_log, *, reverse, heads, groups, gate_with=None):
    L = xbc.shape[0]
    Q = SSM_CHUNK
    assert L % Q == 0
    nc = L // Q
    W = heads * SSM_HEAD_DIM
    R = heads // groups
    cmap = (lambda i: (nc - 1 - i, 0)) if reverse else (lambda i: (i, 0))
    ex = np.zeros((LANES, W), np.float32)
    ex[np.arange(W) // SSM_HEAD_DIM, np.arange(W)] = 1.0
    in_specs = [pl.BlockSpec((Q, xbc.shape[1]), cmap), pl.BlockSpec((Q, LANES), cmap),
                pl.BlockSpec((1, LANES), lambda i: (0, 0)), pl.BlockSpec((1, LANES), lambda i: (0, 0)),
                pl.BlockSpec((LANES, W), lambda i: (0, 0))]
    args = [xbc, dt_raw, dt_bias, a_log, jnp.asarray(ex.astype(BF16))]
    if gate_with is not None:
        y_other, proj, z_off, d_rep, norm_g = gate_with
        assert z_off % W == 0 and W // groups == R * SSM_HEAD_DIM
        zo = z_off // W
        vec = pl.BlockSpec((1, W), lambda i: (0, 0))
        in_specs += [pl.BlockSpec((Q, W), cmap), pl.BlockSpec((Q, W), lambda i: (cmap(i)[0], zo)), vec, vec]
        args += [y_other, proj, d_rep.reshape(1, W), norm_g.reshape(1, W)]
    return pl.pallas_call(
        functools.partial(_ssd_body, reverse=reverse, heads=heads, groups=groups, gated=gate_with is not None),
        grid=(nc,), in_specs=in_specs,
        out_specs=pl.BlockSpec((Q, W), cmap),
        out_shape=jax.ShapeDtypeStruct((L, W), F32 if gate_with is None else BF16),
        scratch_shapes=[pltpu.VMEM((groups, SSM_STATE, R * SSM_HEAD_DIM), F32)],
        compiler_params=_cparams(("arbitrary",)), name="ssd_bwd" if reverse else "ssd_fwd",
    )(*args)


def _dft_consts(L):
    N2 = DFT_N2
    N = 2 * L
    H1 = L // N2
    n2 = np.arange(N2)[:, None, None]
    k1 = np.arange(H1)[None, :, None]
    n1 = np.arange(H1)[None, None, :]
    m = ((2 * k1 + 1) * (N2 * n1 + n2)) % (2 * N)
    ph = np.pi * m.astype(np.float64) / N
    m_fwd = np.concatenate([np.cos(ph), -np.sin(ph)], axis=1)
    m_inv = (2.0 / N) * np.concatenate([np.cos(ph), -np.sin(ph)], axis=1).transpose(0, 2, 1)
    sg = np.where(np.arange(H1) % 2 == 0, 1.0, -1.0)[None, :, None]
    m_filt = np.concatenate([np.concatenate([np.cos(ph), -sg * np.sin(ph)], axis=2),
                             np.concatenate([-np.sin(ph), -sg * np.cos(ph)], axis=2)], axis=1)
    kk = (np.arange(N2)[:, None] * np.arange(N2)[None, :]) % N2
    c2 = np.cos(2 * np.pi * kk / N2)
    s2 = np.sin(2 * np.pi * kk / N2)
    f2_fwd = np.block([[c2, s2], [-s2, c2]])
    f2_inv = np.block([[c2, -s2], [s2, c2]])
    as_bf = lambda a: jnp.asarray(a.astype(BF16))
    return as_bf(m_fwd), as_bf(m_inv), as_bf(f2_fwd), as_bf(f2_inv), as_bf(m_filt)


def _pitch(rows):
    p = rows + SUBLANES
    assert (p // SUBLANES) % 2 == 1
    return p


def _pack2(a, b):
    ha = lax.bitcast_convert_type(a.astype(BF16).astype(F32), jnp.uint32)
    hb = lax.bitcast_convert_type(b.astype(BF16).astype(F32), jnp.uint32)
    return ha | (hb >> 16)


def _unpack2(w):
    a = lax.bitcast_convert_type(w & jnp.uint32(0xFFFF0000), F32)
    b = lax.bitcast_convert_type(w << 16, F32)
    return jnp.concatenate([a, b], axis=1).astype(BF16)


def _split_lanes(x):
    c = x.shape[1] // 2
    return x[:, :c], x[:, c:]


def _stage3(a_s, f2_ref, k1, H1, pitch):
    N2 = DFT_N2
    ar = _unpack2(a_s[pl.ds(k1, N2, stride=pitch), :])
    ai = _unpack2(a_s[pl.ds(H1 + k1, N2, stride=pitch), :])
    return _dot(f2_ref[...], jnp.concatenate([ar, ai], axis=0))


def _conv_body(u_ref, g_ref, mf_ref, mi_ref, f2f_ref, f2i_ref, y_ref, a_s, *, H1, ks, gw, ng, nsl):
    N2, cb = DFT_N2, HY_CB
    pitch = _pitch(2 * H1)
    s = pl.program_id(1)

    @pl.when(s < ng)
    def _():
        for t in range(gw):
            n2 = s * gw + t
            rows = slice(t * SUBLANES, (t + 1) * SUBLANES)
            u = jnp.concatenate([u_ref[0, :, rows, :].reshape(H1, cb), u_ref[1, :, rows, :].reshape(H1, cb)],
                                axis=1).astype(BF16)
            a = _dot(mf_ref[n2], u)
            a_s[pl.ds(pl.multiple_of(n2 * pitch, SUBLANES), 2 * H1), :] = _pack2(*_split_lanes(a))

    @pl.when((s >= ng) & (s < ng + nsl))
    def _():
        k0 = (s - ng) * ks
        xs = [_stage3(a_s, f2f_ref, k0 + kk, H1, pitch).astype(BF16) for kk in range(ks)]
        bs = []
        for kk in range(ks):
            gr = jnp.concatenate([g_ref[0, kk, 0], g_ref[1, kk, 0]], axis=1)
            gi = jnp.concatenate([g_ref[0, kk, 1], g_ref[1, kk, 1]], axis=1)
            xr, xi = xs[kk][:N2], xs[kk][N2:]
            y = jnp.concatenate([xr * gr - xi * gi, xr * gi + xi * gr], axis=0)
            bs.append(_pack2(*_split_lanes(_dot(f2i_ref[...], y))))
        for kk in range(ks):
            a_s[pl.ds(k0 + kk, N2, stride=pitch), :] = bs[kk][:N2]
            a_s[pl.ds(H1 + k0 + kk, N2, stride=pitch), :] = bs[kk][N2:]

    @pl.when(s >= ng + nsl)
    def _():
        for t in range(gw):
            n2 = (s - ng - nsl) * gw + t
            b = _unpack2(a_s[pl.ds(pl.multiple_of(n2 * pitch, SUBLANES), 2 * H1), :])
            y = _dot(mi_ref[n2], b)
            for p, yp in enumerate(_split_lanes(y)):
                y_ref[p, :, t * SUBLANES:(t + 1) * SUBLANES, :] = yp.reshape(H1 // SUBLANES, SUBLANES, cb)


def _hy_conv(u_blk, slab_off, nj, g_all, order, consts):
    L = u_blk.shape[1]
    N2, cb = DFT_N2, HY_CB
    H1 = L // N2
    ks = _pick(H1, 16, 1)
    gw = _pick(N2, DFT_GROUP, 1)
    ng, nsl = N2 // gw, H1 // ks
    m_fwd, m_inv, f2_fwd, f2_inv, _ = consts
    assert H1 % SUBLANES == 0 and nj % 2 == 0 and slab_off % 2 == 0
    u4 = u_blk.reshape(u_blk.shape[0], H1 // SUBLANES, TT_CHUNK, cb)
    const3 = lambda shape: pl.BlockSpec(shape, lambda j, s: (0, 0, 0), pipeline_mode=pl.Buffered(1))
    y4 = pl.pallas_call(
        functools.partial(_conv_body, H1=H1, ks=ks, gw=gw, ng=ng, nsl=nsl), grid=(nj // 2, 2 * ng + nsl),
        in_specs=[pl.BlockSpec((2, H1 // SUBLANES, gw * SUBLANES, cb),
                               lambda j, s: (slab_off // 2 + j, 0, jnp.minimum(s, ng - 1), 0)),
                  pl.BlockSpec((None, 2, ks, 2, N2, cb),
                               lambda j, s: (order, j, jnp.clip(s - ng, 0, nsl - 1), 0, 0, 0)),
                  const3((N2, 2 * H1, H1)), const3((N2, H1, 2 * H1)),
                  pl.BlockSpec((2 * N2, 2 * N2), lambda j, s: (0, 0)),
                  pl.BlockSpec((2 * N2, 2 * N2), lambda j, s: (0, 0))],
        out_specs=pl.BlockSpec((2, H1 // SUBLANES, gw * SUBLANES, cb),
                               lambda j, s: (j, 0, jnp.clip(s - ng - nsl, 0, ng - 1), 0)),
        out_shape=jax.ShapeDtypeStruct((nj, H1 // SUBLANES, TT_CHUNK, cb), F32),
        scratch_shapes=[pltpu.VMEM((N2 * _pitch(2 * H1), cb), jnp.uint32)],
        compiler_params=_cparams(("parallel", "arbitrary"), vmem_mb=CONV_VMEM_MB), name="hy_conv",
    )(u4, g_all, m_fwd, m_inv, f2_fwd, f2_inv)
    return y4.reshape(nj, L, cb)


def _pos_features(L, emb):
    N2 = DFT_N2
    H1 = L // N2
    bands = (emb - 1) // 2
    r = jnp.arange(L)
    lag = (r % H1) * N2 + (r // H1)
    f = jnp.linspace(1e-4, bands - 1, bands, dtype=F32)[None, :]

    def feats(lg):
        lgf = lg.astype(F32)[:, None]
        t = lgf / (L - 1)
        w = 2.0 * math.pi * lgf / L
        z = jnp.concatenate([t, jnp.cos(f * w), -jnp.sin(f * w)], axis=-1)
        return jnp.pad(z, ((0, 0), (0, LANES - emb)))
    return jnp.concatenate([feats(lag), feats(jnp.where(lag == 0, 0, L - lag))], axis=-1)


def _filter_mlp_body(z_ref, w1_ref, b1_ref, w2_ref, b2_ref, w3_ref, b3_ref, fr_ref, o_ref):
    h = jnp.sin(fr_ref[0:1, :] * (_dot_f32(z_ref[...], w1_ref[...]) + b1_ref[...]))
    h = jnp.sin(fr_ref[1:2, :] * (_dot_f32(h, w2_ref[...]) + b2_ref[...]))
    h = jnp.sin(fr_ref[2:3, :] * (_dot_f32(h, w3_ref[...]) + b3_ref[...]))
    o_ref[...] = h.astype(o_ref.dtype)


def _bdiag(w):
    a, b = w.shape
    z = jnp.zeros((a, b), w.dtype)
    return jnp.concatenate([jnp.concatenate([w, z], axis=1), jnp.concatenate([z, w], axis=1)], axis=0)


def _filter_mlp(z2, w1, b1, w2, b2, w3, b3, fr):
    L = z2.shape[0]
    emb, hd = w1.shape
    assert 2 * hd == LANES
    w1p = jnp.zeros((2 * LANES, LANES), F32).at[:emb, :hd].set(w1).at[LANES:LANES + emb, hd:].set(w1)
    tile2 = lambda v: jnp.concatenate([v, v], axis=-1)
    bm = _pick(L, 1024, SUBLANES)
    full = lambda a: pl.BlockSpec(a.shape, lambda i: (0,) * a.ndim)
    args = (w1p, tile2(b1).reshape(1, LANES), _bdiag(w2), tile2(b2).reshape(1, LANES),
            _bdiag(w3), tile2(b3).reshape(1, LANES), tile2(fr))
    return pl.pallas_call(
        _filter_mlp_body, grid=(L // bm,),
        in_specs=[pl.BlockSpec((bm, 2 * LANES), lambda i: (i, 0))] + [full(a) for a in args],
        out_specs=pl.BlockSpec((bm, LANES), lambda i: (i, 0)),
        out_shape=jax.ShapeDtypeStruct((L, LANES), BF16),
        compiler_params=_cparams(("parallel",)), name="hy_filter_mlp",
    )(z2, *args)


def _filt_body(hm_ref, w2_ref, ad_ref, m_ref, f2_ref, g_ref, a_s, g_s, inv_s, *, L, H1, ks):
    N2 = DFT_N2
    cb = 2 * HY_CB
    pitch = _pitch(2 * H1)
    s = pl.program_id(2)

    @pl.when(s == 0)
    def _():
        rate = ad_ref[...] * (1.0 / (L - 1))
        n1 = lax.broadcasted_iota(jnp.int32, (H1, cb), 0)
        e1f = jnp.exp(-(n1 * N2).astype(F32) * rate)
        e1b = jnp.exp(-((H1 - 1 - n1) * N2).astype(F32) * rate)

        def group(gi, nrm):
            def gen(t, nrm):
                n2 = gi * FILT_GROUP + t
                hs = hm_ref[pl.ds(pl.multiple_of(n2 * H1, SUBLANES), H1), :]
                gp = _dot(hs, w2_ref[...])
                n2f = n2.astype(F32)
                gf = gp[:, :cb] * (e1f * jnp.exp(-n2f * rate))
                gb = gp[:, cb:] * (e1b * jnp.exp(-(N2 - n2f) * rate))
                gb = jnp.where(n1 + n2 == 0, 0.0, -gb)
                g_s[t, 0:H1, :] = gf.astype(BF16)
                g_s[t, H1:2 * H1, :] = gb.astype(BF16)
                return nrm + (jnp.abs(gf) + jnp.abs(gb))
            nrm = lax.fori_loop(0, FILT_GROUP, gen, nrm, unroll=DFT_UNROLL)

            def xform(t, carry):
                n2 = gi * FILT_GROUP + t
                a = _dot(m_ref[n2], g_s[t])
                a_s[pl.ds(pl.multiple_of(n2 * pitch, SUBLANES), 2 * H1), :] = _pack2(*_split_lanes(a))
                return carry
            lax.fori_loop(0, FILT_GROUP, xform, 0, unroll=DFT_UNROLL)
            return nrm
        nrm = lax.fori_loop(0, N2 // FILT_GROUP, group, jnp.zeros((H1, cb), F32))
        inv_s[...] = jnp.broadcast_to(1.0 / jnp.sum(nrm, axis=0, keepdims=True), inv_s.shape)

    scale = inv_s[0:1, :]

    def kbody(kk, carry):
        x = (_stage3(a_s, f2_ref, s * ks + kk, H1, pitch) * scale).astype(g_ref.dtype)
        for p, xp in enumerate(_split_lanes(x)):
            g_ref[p, kk, 0] = xp[:N2]
            g_ref[p, kk, 1] = xp[N2:]
        return carry
    lax.fori_loop(0, ks, kbody, 0, unroll=DFT_UNROLL)


def _hy_filters(hm, wout, C, L, consts):
    N2, cb = DFT_N2, HY_CB
    H1 = L // N2
    ks = _pick(H1, 16, 1)
    hd = wout.shape[0]
    nj = C // cb
    _, _, f2_fwd, _, m_filt = consts
    assert nj % 2 == 0
    pw = 2 * cb
    w4 = wout.reshape(hd, 2, 2, nj // 2, pw)
    wf = jnp.transpose(w4[:, :, 0], (1, 2, 0, 3))
    wb = jnp.transpose(w4[:, :, 1], (1, 2, 0, 3))
    z = jnp.zeros_like(wf)
    w2 = jnp.concatenate([jnp.concatenate([wf, z], axis=-1), jnp.concatenate([z, wb], axis=-1)], axis=2)
    w2 = w2.astype(BF16)
    deltas = jnp.linspace(math.log(HY_FAST_DECAY) / HY_DECAY_TARGET,
                          math.log(HY_SLOW_DECAY) / HY_DECAY_TARGET, C, dtype=F32)
    absd = jnp.abs(deltas).reshape(1, C)
    return pl.pallas_call(
        functools.partial(_filt_body, L=L, H1=H1, ks=ks), grid=(2, nj // 2, H1 // ks),
        in_specs=[pl.BlockSpec((L, LANES), lambda o, j, s: (0, 0)),
                  pl.BlockSpec((None, None, 2 * hd, 2 * pw), lambda o, j, s: (o, j, 0, 0)),
                  pl.BlockSpec((1, pw), lambda o, j, s: (0, j)),
                  pl.BlockSpec((N2, 2 * H1, 2 * H1), lambda o, j, s: (0, 0, 0), pipeline_mode=pl.Buffered(1)),
                  pl.BlockSpec((2 * N2, 2 * N2), lambda o, j, s: (0, 0))],
        out_specs=pl.BlockSpec((None, 2, ks, 2, N2, cb), lambda o, j, s: (o, j, s, 0, 0, 0)),
        out_shape=jax.ShapeDtypeStruct((2, nj, H1, 2, N2, cb), SPEC_DTYPE),
        scratch_shapes=[pltpu.VMEM((N2 * _pitch(2 * H1), cb), jnp.uint32),
                        pltpu.VMEM((FILT_GROUP, 2 * H1, pw), BF16), pltpu.VMEM((SUBLANES, pw), F32)],
        compiler_params=_cparams(("parallel", "parallel", "arbitrary")), name="hy_filters",
    )(hm, w2, absd, m_filt, f2_fwd)


def _hy_gate1_body(gate_ref, v_ref, y_ref, sk_ref, o_ref):
    o_ref[...] = gate_ref[...] * (y_ref[...] + sk_ref[...] * v_ref[...])


def _hy_gate1(uh_blk, gate_blk, v_blk, y_blk, sk):
    nj, L, cb = y_blk.shape
    bm = _pick(L, 256, SUBLANES)
    blk = lambda g: pl.BlockSpec((nj, bm, cb), lambda i: (g, i, 0))
    return pl.pallas_call(
        _hy_gate1_body, grid=(L // bm,),
        in_specs=[blk(gate_blk), blk(v_blk), blk(0), pl.BlockSpec((nj, 1, cb), lambda i: (0, 0, 0))],
        out_specs=blk(0), out_shape=jax.ShapeDtypeStruct((nj, L, cb), F32),
        compiler_params=_cparams(("parallel",)), name="hy_gate1",
    )(uh_blk, uh_blk, y_blk, sk.reshape(nj, 1, cb))


def _hy_gate2_body(gate_ref, v_ref, y_ref, sk_ref, g_ref, o_ref, *, gw):
    nj, _, cb = gate_ref.shape
    for r in range(SUBLANES):
        rows = pl.ds(r, gw, stride=SUBLANES)
        zs = []
        ss = jnp.zeros((gw, cb), F32)
        for k in range(nj):
            z = gate_ref.at[k][rows, :] * (y_ref.at[k][rows, :] + sk_ref[k] * v_ref.at[k][rows, :])
            zs.append(z)
            ss = ss + z * z
        ms = jnp.sum(ss, axis=-1, keepdims=True) * (1.0 / (nj * cb))
        inv = lax.rsqrt(ms + NORM_EPS)
        for k in range(nj):
            o_ref[r, :, k * cb:(k + 1) * cb] = (zs[k] * inv * g_ref[k]).astype(o_ref.dtype)


def _hy_gate2(uh_blk, gate_blk, v_blk, y_blk, sk, g):
    nj, L, cb = y_blk.shape
    N2 = DFT_N2
    nch = L // TT_CHUNK
    gw = _pick(N2, DFT_GROUP, 16)
    ngr = N2 // gw
    blk = lambda gidx: pl.BlockSpec((nj, gw * SUBLANES, cb), lambda c, q: (gidx, c * ngr + q, 0))
    vec = pl.BlockSpec((nj, 1, cb), lambda c, q: (0, 0, 0))
    out = pl.pallas_call(
        functools.partial(_hy_gate2_body, gw=gw), grid=(nch, ngr),
        in_specs=[blk(gate_blk), blk(0), blk(0), vec, vec],
        out_specs=pl.BlockSpec((None, SUBLANES, gw, nj * cb), lambda c, q: (c, 0, q, 0)),
        out_shape=jax.ShapeDtypeStruct((nch, SUBLANES, N2, nj * cb), BF16),
        compiler_params=_cparams(("parallel", "parallel")), name="hy_gate2",
    )(uh_blk, v_blk, y_blk, sk.reshape(nj, 1, cb), g.reshape(nj, 1, cb))
    return out.reshape(L, nj * cb)


def kernel(x, norm_mix_g, w_in, hy_conv_w, hy_conv_b, hy_pos_w1, hy_pos_b1, hy_pos_w2, hy_pos_b2, hy_pos_w3, hy_pos_b3, hy_sin_freq, hy_pos_wout, hy_skip, hy_out_g, ssm_conv_w, ssm_conv_b, ssm_A_log, ssm_dt_bias, ssm_D, ssm_out_g, w_out, norm_mlp_g, w_up, w_down, final_norm_g):
    B, L, D = x.shape
    depth = w_in.shape[0]
    CH = hy_out_g.shape[1]
    WS = ssm_out_g.shape[1]
    heads = ssm_D.shape[1]
    xbc_w = ssm_conv_w.shape[2]
    groups = (xbc_w - WS) // (2 * SSM_STATE)
    o1 = 3 * CH
    o2 = o1 + WS
    o3 = o2 + xbc_w
    assert WS == heads * SSM_HEAD_DIM and heads <= LANES and L % DFT_N2 == 0 and CH % HY_CB == 0

    consts = _dft_consts(L)
    z2 = _pos_features(L, hy_pos_w1.shape[1])
    w_in_b = w_in.astype(BF16)
    w_in_dt = jnp.pad(w_in[:, :, o3:], ((0, 0), (0, 0), (0, LANES - heads))).astype(BF16)
    w_out_b = w_out.astype(BF16)
    w_up_b = w_up.astype(BF16)
    w_down_b = w_down.astype(BF16)
    pad_h = lambda v: jnp.pad(v, (0, LANES - heads)).reshape(1, LANES)
    d_rep = jnp.repeat(ssm_D, SSM_HEAD_DIM, axis=-1)

    outs = []
    for b in range(B):
        xb = x[b]
        delta = None
        for l in range(depth):
            if delta is None:
                h = _rmsnorm(xb, norm_mix_g[l], BF16)
            else:
                xb, h = _add_rmsnorm(xb, delta, norm_mix_g[l], BF16)
            proj = _matmul(h, w_in_b, l, n_cols=o3, out_dtype=PROJ_DTYPE)
            dt_raw = _matmul(h, w_in_dt, l, bm_t=2048)

            nj = CH // HY_CB
            uh = _dwconv(proj, 0, o1, hy_conv_w[l], hy_conv_b[l], "none", blocked_out=True)
            hm = _filter_mlp(z2, hy_pos_w1[l], hy_pos_b1[l], hy_pos_w2[l], hy_pos_b2[l],
                             hy_pos_w3[l], hy_pos_b3[l], hy_sin_freq[l])
            g_all = _hy_filters(hm, hy_pos_wout[l], CH, L, consts)
            y0 = _hy_conv(uh, 2 * nj, nj, g_all, 0, consts)
            z1 = _hy_gate1(uh, 0, 2, y0, hy_skip[l, 0])
            y1 = _hy_conv(z1, 0, nj, g_all, 1, consts)
            y_hy = _hy_gate2(uh, 1, z1, y1, hy_skip[l, 1], hy_out_g[l])

            xbc = _dwconv(proj, o2, xbc_w, ssm_conv_w[l], ssm_conv_b[l], "silu")
            yf = _ssd(xbc, dt_raw, pad_h(ssm_dt_bias[l, 0]), pad_h(ssm_A_log[l, 0]),
                      reverse=False, heads=heads, groups=groups)
            y_ssm = _ssd(xbc, dt_raw, pad_h(ssm_dt_bias[l, 1]), pad_h(ssm_A_log[l, 1]),
                         reverse=True, heads=heads, groups=groups,
                         gate_with=(yf, proj, o1, d_rep[l], ssm_out_g[l]))

            xb = _out_matmul(y_hy, y_ssm, w_out_b, l, xb)
            h2 = _rmsnorm(xb, norm_mlp_g[l], BF16)
            delta = _mlp(h2, w_up_b, w_down_b, l)
        outs.append(_add_rmsnorm(xb, delta, final_norm_g, F32, keep_sum=False))
    return jnp.stack(outs, axis=0)
```

```python
import functools
import math

import numpy as np
import jax
import jax.numpy as jnp
from jax import lax
from jax.experimental import pallas as pl
from jax.experimental.pallas import tpu as pltpu

F32 = jnp.float32
BF16 = jnp.bfloat16

NORM_EPS = 1e-5
SSM_HEAD_DIM = 64
SSM_STATE = 128
SSM_CHUNK = 256
HY_FAST_DECAY = 0.3
HY_SLOW_DECAY = 1.5
HY_DECAY_TARGET = 1e-2

LANES = 128
SUBLANES = 8
DFT_N2 = 128
HY_CB = 128
DFT_UNROLL = 16
DFT_GROUP = 32
TT_CHUNK = SUBLANES * DFT_N2
FILT_GROUP = 16
SPEC_DTYPE = BF16
PROJ_DTYPE = BF16
VMEM_LIMIT_MB = 56
CONV_VMEM_MB = 60


def _cparams(sem, vmem_mb=VMEM_LIMIT_MB):
    return pltpu.CompilerParams(dimension_semantics=sem, vmem_limit_bytes=vmem_mb * 1024 * 1024)


def _pick(n, target, mult):
    best = None
    for d in range(mult, min(n, target) + 1, mult):
        if n % d == 0:
            best = d
    assert best is not None, (n, target, mult)
    return best


def _dot(a, b):
    return jnp.dot(a, b, preferred_element_type=F32)


def _split3(a):
    hi = a.astype(BF16)
    r1 = a - hi.astype(F32)
    mid = r1.astype(BF16)
    lo = (r1 - mid.astype(F32)).astype(BF16)
    return hi, mid, lo


def _dot_f32(a, b):
    ah = a.astype(BF16)
    al = (a - ah.astype(F32)).astype(BF16)
    bh = b.astype(BF16)
    bl = (b - bh.astype(F32)).astype(BF16)
    return _dot(ah, bh) + (_dot(ah, bl) + _dot(al, bh))


def _silu(x):
    return x / (1.0 + jnp.exp(-x))


def _rmsnorm_body(x_ref, g_ref, h_ref):
    x = x_ref[...]
    ms = jnp.mean(x * x, axis=-1, keepdims=True)
    h_ref[...] = (x * lax.rsqrt(ms + NORM_EPS) * g_ref[...]).astype(h_ref.dtype)


def _rmsnorm(x, g, out_dtype):
    M, D = x.shape
    bm = _pick(M, 256, SUBLANES)
    return pl.pallas_call(
        _rmsnorm_body, grid=(M // bm,),
        in_specs=[pl.BlockSpec((bm, D), lambda i: (i, 0)), pl.BlockSpec((1, D), lambda i: (0, 0))],
        out_specs=pl.BlockSpec((bm, D), lambda i: (i, 0)),
        out_shape=jax.ShapeDtypeStruct((M, D), out_dtype),
        compiler_params=_cparams(("parallel",)), name="rmsnorm",
    )(x, g.reshape(1, D))


def _add_rmsnorm_body(x_ref, d_ref, g_ref, *out_refs):
    x = x_ref[...] + d_ref[...]
    if len(out_refs) == 2:
        out_refs[0][...] = x
    h_ref = out_refs[-1]
    ms = jnp.mean(x * x, axis=-1, keepdims=True)
    h_ref[...] = (x * lax.rsqrt(ms + NORM_EPS) * g_ref[...]).astype(h_ref.dtype)


def _add_rmsnorm(x, delta, g, out_dtype, keep_sum=True):
    M, D = x.shape
    bm = _pick(M, 256, SUBLANES)
    row = pl.BlockSpec((bm, D), lambda i: (i, 0))
    h_shape = jax.ShapeDtypeStruct((M, D), out_dtype)
    return pl.pallas_call(
        _add_rmsnorm_body, grid=(M // bm,),
        in_specs=[row, row, pl.BlockSpec((1, D), lambda i: (0, 0))],
        out_specs=[row, row] if keep_sum else row,
        out_shape=[jax.ShapeDtypeStruct((M, D), F32), h_shape] if keep_sum else h_shape,
        compiler_params=_cparams(("parallel",)), name="add_rmsnorm",
    )(x, delta, g.reshape(1, D))


def _mm_body(a_ref, b_ref, o_ref):
    o_ref[...] = _dot(a_ref[...], b_ref[...].astype(BF16)).astype(o_ref.dtype)


def _matmul(a, b_stack, layer, n_cols=None, out_dtype=F32, bm_t=1024, bn_t=1024):
    M, K = a.shape
    N = b_stack.shape[2] if n_cols is None else n_cols
    bm = _pick(M, bm_t, SUBLANES)
    bn = _pick(N, bn_t, LANES)
    return pl.pallas_call(
        _mm_body, grid=(M // bm, N // bn),
        in_specs=[pl.BlockSpec((bm, K), lambda i, j: (i, 0)),
                  pl.BlockSpec((None, K, bn), lambda i, j: (layer, 0, j))],
        out_specs=pl.BlockSpec((bm, bn), lambda i, j: (i, j)),
        out_shape=jax.ShapeDtypeStruct((M, N), out_dtype),
        compiler_params=_cparams(("parallel", "parallel")), name="proj_matmul",
    )(a, b_stack)


def _out_mm_body(a1_ref, a2_ref, b1_ref, b2_ref, r_ref, o_ref):
    acc = _dot(a1_ref[...], b1_ref[...]) + _dot(a2_ref[...], b2_ref[...])
    o_ref[...] = r_ref[...] + acc


def _out_matmul(a1, a2, b_stack, layer, res):
    M, K1 = a1.shape
    assert a2.shape[1] == K1 and b_stack.shape[1] == 2 * K1
    N = b_stack.shape[2]
    bm = _pick(M, 1024, SUBLANES)
    bn = _pick(N, 1024, LANES)
    return pl.pallas_call(
        _out_mm_body, grid=(M // bm, N // bn),
        in_specs=[pl.BlockSpec((bm, K1), lambda i, j: (i, 0)), pl.BlockSpec((bm, K1), lambda i, j: (i, 0)),
                  pl.BlockSpec((None, K1, bn), lambda i, j: (layer, 0, j)),
                  pl.BlockSpec((None, K1, bn), lambda i, j: (layer, 1, j)),
                  pl.BlockSpec((bm, bn), lambda i, j: (i, j))],
        out_specs=pl.BlockSpec((bm, bn), lambda i, j: (i, j)),
        out_shape=jax.ShapeDtypeStruct((M, N), F32),
        compiler_params=_cparams(("parallel", "parallel")), name="out_matmul",
    )(a1, a2, b_stack, b_stack, res)


def _mlp_body(h_ref, wu_ref, wd_ref, o_ref, *, cw):
    j = pl.program_id(1)

    @pl.when(j == 0)
    def _():
        o_ref[...] = jnp.zeros_like(o_ref)

    hw = wu_ref.shape[1] // 2
    ts = []
    for q in range(2):
        t = _dot(h_ref[...], wu_ref[:, q * hw:(q + 1) * hw])
        ts.append(jnp.square(jnp.maximum(t, 0.0)).astype(BF16))
    for c in range(o_ref.shape[1] // cw):
        cols = slice(c * cw, (c + 1) * cw)
        o_ref[:, cols] += _dot(ts[0], wd_ref[0:hw, cols]) + _dot(ts[1], wd_ref[hw:2 * hw, cols])


def _mlp(h, w_up_stack, w_down_stack, layer):
    M, D = h.shape
    F = w_up_stack.shape[2]
    bm = _pick(M, 512, SUBLANES)
    bf = _pick(F, 512, LANES)
    cw = _pick(D, 512, LANES)
    return pl.pallas_call(
        functools.partial(_mlp_body, cw=cw), grid=(M // bm, F // bf),
        in_specs=[pl.BlockSpec((bm, D), lambda i, j: (i, 0)),
                  pl.BlockSpec((None, D, bf), lambda i, j: (layer, 0, j)),
                  pl.BlockSpec((None, bf, D), lambda i, j: (layer, j, 0))],
        out_specs=pl.BlockSpec((bm, D), lambda i, j: (i, 0)),
        out_shape=jax.ShapeDtypeStruct((M, D), F32),
        compiler_params=_cparams(("parallel", "arbitrary")), name="mlp",
    )(h, w_up_stack, w_down_stack)


def _dwconv_body(p_ref, pv_ref, nx_ref, w_ref, b_ref, o_ref, *, act):
    i = pl.program_id(0)
    last = pl.num_programs(0) - 1
    u = p_ref[...].astype(F32)
    bm = u.shape[0]
    hr = pv_ref.shape[0]
    prev_row = jnp.where(i > 0, pv_ref[hr - 1:hr, :].astype(F32), 0.0)
    next_row = jnp.where(i < last, nx_ref[0:1, :].astype(F32), 0.0)
    rows = lax.broadcasted_iota(jnp.int32, u.shape, 0)
    um = jnp.where(rows == 0, prev_row, pltpu.roll(u, 1, 0))
    up = jnp.where(rows == bm - 1, next_row, pltpu.roll(u, bm - 1, 0))
    y = w_ref[0:1, :] * um
    y = y + w_ref[1:2, :] * u
    y = y + w_ref[2:3, :] * up
    y = y + b_ref[...]
    if act == "silu":
        y = _silu(y)
    if len(o_ref.shape) == 3:
        assert bm == TT_CHUNK
        for k in range(o_ref.shape[0]):
            for r in range(SUBLANES):
                o_ref.at[k][pl.ds(r, DFT_N2, stride=SUBLANES), :] = (
                    y[r * DFT_N2:(r + 1) * DFT_N2, k * LANES:(k + 1) * LANES])
    else:
        o_ref[...] = y


def _dwconv(p, col_off, C, w, b, act, blocked_out=False):
    L = p.shape[0]
    bm = TT_CHUNK if blocked_out else _pick(L, 1024, SUBLANES)
    assert L % bm == 0
    bc = _pick(math.gcd(C, col_off) if col_off else C, 1024, LANES)
    co = col_off // bc
    hr = SUBLANES * (4 // p.dtype.itemsize)
    hb = bm // hr
    nrb = L // hr
    if blocked_out:
        spb = bc // LANES
        out_spec = pl.BlockSpec((spb, bm, LANES), lambda i, j: (j, i, 0))
        out_shape = jax.ShapeDtypeStruct((C // LANES, L, LANES), F32)
    else:
        out_spec = pl.BlockSpec((bm, bc), lambda i, j: (i, j))
        out_shape = jax.ShapeDtypeStruct((L, C), F32)
    return pl.pallas_call(
        functools.partial(_dwconv_body, act=act), grid=(L // bm, C // bc),
        in_specs=[pl.BlockSpec((bm, bc), lambda i, j: (i, co + j)),
                  pl.BlockSpec((hr, bc), lambda i, j: (jnp.maximum(i * hb - 1, 0), co + j)),
                  pl.BlockSpec((hr, bc), lambda i, j: (jnp.minimum((i + 1) * hb, nrb - 1), co + j)),
                  pl.BlockSpec((3, bc), lambda i, j: (0, j)),
                  pl.BlockSpec((1, bc), lambda i, j: (0, j))],
        out_specs=out_spec, out_shape=out_shape,
        compiler_params=_cparams(("parallel", "parallel")), name="dwconv_" + act,
    )(p, p, p, w, b.reshape(1, C))


def _expand_heads(v, ex_ref):
    hi = v.astype(BF16)
    lo = (v - hi.astype(F32)).astype(BF16)
    return _dot(hi, ex_ref[...]) + _dot(lo, ex_ref[...])


def _ssd_body(xbc_ref, dt_ref, bias_ref, alog_ref, ex_ref, *rest, reverse, heads, groups, gated):
    if gated:
        yo_ref, z_ref, d_ref, gn_ref, y_ref, s_ref = rest
    else:
        y_ref, s_ref = rest
    Q = SSM_CHUNK
    P = SSM_HEAD_DIM
    N = SSM_STATE
    R = heads // groups
    W = heads * P
    step = pl.program_id(0)

    @pl.when(step == 0)
    def _():
        s_ref[...] = jnp.zeros_like(s_ref)

    x = dt_ref[...] + bias_ref[...]
    dt = jnp.maximum(x, 0.0) + jnp.log1p(jnp.exp(-jnp.abs(x)))
    a = dt * (-jnp.exp(alog_ref[...]))
    ri = lax.broadcasted_iota(jnp.int32, (Q, Q), 0)
    ci = lax.broadcasted_iota(jnp.int32, (Q, Q), 1)
    mask = (ci >= ri) if reverse else (ri >= ci)
    tri = jnp.where(mask, 1.0, 0.0).astype(BF16)
    ah, am, al = _split3(a)
    acs = _dot(tri, ah) + (_dot(tri, am) + _dot(tri, al))
    tot = acs[0:1, :] if reverse else acs[Q - 1:Q, :]
    e_all = _expand_heads(jnp.exp(acs), ex_ref)
    w_all = _expand_heads(jnp.exp(tot - acs) * dt, ex_ref)
    t_all = _expand_heads(jnp.broadcast_to(jnp.exp(tot), (SUBLANES, LANES)), ex_ref)[0:1, :]
    acs_row = acs.T
    dt_row = dt.T

    load_b = lambda g: xbc_ref[:, W + g * N:W + (g + 1) * N].astype(BF16)
    load_c = lambda g: xbc_ref[:, W + groups * N + g * N:W + groups * N + (g + 1) * N].astype(BF16)
    load_x = lambda g: xbc_ref[:, g * R * P:(g + 1) * R * P]
    lane = lax.broadcasted_iota(jnp.int32, (Q, R * P), 1)

    cbs, y_offs = [], []
    for g in range(groups):
        cg = load_c(g)
        cbs.append(lax.dot_general(cg, load_b(g), (((1,), (1,)), ((), ())), preferred_element_type=F32))
        y_offs.append(_dot(cg, s_ref[g].astype(BF16)))
    for g in range(groups):
        gl = slice(g * R * P, (g + 1) * R * P)
        s_new = lax.dot_general(load_b(g), (load_x(g) * w_all[:, gl]).astype(BF16), (((0,), (0,)), ((), ())),
                                preferred_element_type=F32)
        s_ref[g] = s_ref[g] * t_all[:, gl] + s_new
    for g in range(groups):
        xg = load_x(g)
        ws, xs = [], []
        for r in range(R):
            h = g * R + r
            diff = acs[:, h:h + 1] - acs_row[h:h + 1, :]
            lm = jnp.exp(jnp.where(mask, diff, -jnp.inf))
            ws.append((cbs[g] * lm * dt_row[h:h + 1, :]).astype(BF16))
            xs.append(jnp.where((lane >= r * P) & (lane < (r + 1) * P), xg, 0.0).astype(BF16))
        y_diag = _dot(jnp.concatenate(ws, axis=1), jnp.concatenate(xs, axis=0))
        gl = slice(g * R * P, (g + 1) * R * P)
        y = y_diag + y_offs[g] * e_all[:, gl]
        if gated:
            y = (y + yo_ref[:, gl] + d_ref[:, gl] * xg) * _silu(z_ref[:, gl].astype(F32))
            ms = jnp.mean(y * y, axis=-1, keepdims=True)
            y = y * lax.rsqrt(ms + NORM_EPS) * gn_ref[:, gl]
        y_ref[:, gl] = y.astype(y_ref.dtype)


def _ssd(xbc, dt_raw, dt_bias, a_log, *, reverse, heads, groups, gate_with=None):
    L = xbc.shape[0]
    Q = SSM_CHUNK
    assert L % Q == 0
    nc = L // Q
    W = heads * SSM_HEAD_DIM
    R = heads // groups
    cmap = (lambda i: (nc - 1 - i, 0)) if reverse else (lambda i: (i, 0))
    ex = np.zeros((LANES, W), np.float32)
    ex[np.arange(W) // SSM_HEAD_DIM, np.arange(W)] = 1.0
    in_specs = [pl.BlockSpec((Q, xbc.shape[1]), cmap), pl.BlockSpec((Q, LANES), cmap),
                pl.BlockSpec((1, LANES), lambda i: (0, 0)), pl.BlockSpec((1, LANES), lambda i: (0, 0)),
                pl.BlockSpec((LANES, W), lambda i: (0, 0))]
    args = [xbc, dt_raw, dt_bias, a_log, jnp.asarray(ex.astype(BF16))]
    if gate_with is not None:
        y_other, proj, z_off, d_rep, norm_g = gate_with
        assert z_off % W == 0 and W // groups == R * SSM_HEAD_DIM
        zo = z_off // W
        vec = pl.BlockSpec((1, W), lambda i: (0, 0))
        in_specs += [pl.BlockSpec((Q, W), cmap), pl.BlockSpec((Q, W), lambda i: (cmap(i)[0], zo)), vec, vec]
        args += [y_other, proj, d_rep.reshape(1, W), norm_g.reshape(1, W)]
    return pl.pallas_call(
        functools.partial(_ssd_body, reverse=reverse, heads=heads, groups=groups, gated=gate_with is not None),
        grid=(nc,), in_specs=in_specs,
        out_specs=pl.BlockSpec((Q, W), cmap),
        out_shape=jax.ShapeDtypeStruct((L, W), F32 if gate_with is None else BF16),
        scratch_shapes=[pltpu.VMEM((groups, SSM_STATE, R * SSM_HEAD_DIM), F32)],
        compiler_params=_cparams(("arbitrary",)), name="ssd_bwd" if reverse else "ssd_fwd",
    )(*args)


def _dft_consts(L):
    N2 = DFT_N2
    N = 2 * L
    H1 = L // N2
    n2 = np.arange(N2)[:, None, None]
    k1 = np.arange(H1)[None, :, None]
    n1 = np.arange(H1)[None, None, :]
    m = ((2 * k1 + 1) * (N2 * n1 + n2)) % (2 * N)
    ph = np.pi * m.astype(np.float64) / N
    m_fwd = np.concatenate([np.cos(ph), -np.sin(ph)], axis=1)
    m_inv = (2.0 / N) * np.concatenate([np.cos(ph), -np.sin(ph)], axis=1).transpose(0, 2, 1)
    sg = np.where(np.arange(H1) % 2 == 0, 1.0, -1.0)[None, :, None]
    m_filt = np.concatenate([np.concatenate([np.cos(ph), -sg * np.sin(ph)], axis=2),
                             np.concatenate([-np.sin(ph), -sg * np.cos(ph)], axis=2)], axis=1)
    kk = (np.arange(N2)[:, None] * np.arange(N2)[None, :]) % N2
    c2 = np.cos(2 * np.pi * kk / N2)
    s2 = np.sin(2 * np.pi * kk / N2)
    f2_fwd = np.block([[c2, s2], [-s2, c2]])
    f2_inv = np.block([[c2, -s2], [s2, c2]])
    as_bf = lambda a: jnp.asarray(a.astype(BF16))
    return as_bf(m_fwd), as_bf(m_inv), as_bf(f2_fwd), as_bf(f2_inv), as_bf(m_filt)


def _pitch(rows):
    p = rows + SUBLANES
    assert (p // SUBLANES) % 2 == 1
    return p


def _pack2(a, b):
    ha = lax.bitcast_convert_type(a.astype(BF16).astype(F32), jnp.uint32)
    hb = lax.bitcast_convert_type(b.astype(BF16).astype(F32), jnp.uint32)
    return ha | (hb >> 16)


def _unpack2(w):
    a = lax.bitcast_convert_type(w & jnp.uint32(0xFFFF0000), F32)
    b = lax.bitcast_convert_type(w << 16, F32)
    return jnp.concatenate([a, b], axis=1).astype(BF16)


def _split_lanes(x):
    c = x.shape[1] // 2
    return x[:, :c], x[:, c:]


def _stage3(a_s, f2_ref, k1, H1, pitch):
    N2 = DFT_N2
    ar = _unpack2(a_s[pl.ds(k1, N2, stride=pitch), :])
    ai = _unpack2(a_s[pl.ds(H1 + k1, N2, stride=pitch), :])
    return _dot(f2_ref[...], jnp.concatenate([ar, ai], axis=0))


def _conv_body(u_ref, g_ref, mf_ref, mi_ref, f2f_ref, f2i_ref, y_ref, a_s, *, H1, ks, gw, ng, nsl):
    N2, cb = DFT_N2, HY_CB
    pitch = _pitch(2 * H1)
    s = pl.program_id(1)

    @pl.when(s < ng)
    def _():
        for t in range(gw):
            n2 = s * gw + t
            rows = slice(t * SUBLANES, (t + 1) * SUBLANES)
            u = jnp.concatenate([u_ref[0, :, rows, :].reshape(H1, cb), u_ref[1, :, rows, :].reshape(H1, cb)],
                                axis=1).astype(BF16)
            a = _dot(mf_ref[n2], u)
            a_s[pl.ds(pl.multiple_of(n2 * pitch, SUBLANES), 2 * H1), :] = _pack2(*_split_lanes(a))

    @pl.when((s >= ng) & (s < ng + nsl))
    def _():
        k0 = (s - ng) * ks
        xs = [_stage3(a_s, f2f_ref, k0 + kk, H1, pitch).astype(BF16) for kk in range(ks)]
        bs = []
        for kk in range(ks):
            gr = jnp.concatenate([g_ref[0, kk, 0], g_ref[1, kk, 0]], axis=1)
            gi = jnp.concatenate([g_ref[0, kk, 1], g_ref[1, kk, 1]], axis=1)
            xr, xi = xs[kk][:N2], xs[kk][N2:]
            y = jnp.concatenate([xr * gr - xi * gi, xr * gi + xi * gr], axis=0)
            bs.append(_pack2(*_split_lanes(_dot(f2i_ref[...], y))))
        for kk in range(ks):
            a_s[pl.ds(k0 + kk, N2, stride=pitch), :] = bs[kk][:N2]
            a_s[pl.ds(H1 + k0 + kk, N2, stride=pitch), :] = bs[kk][N2:]

    @pl.when(s >= ng + nsl)
    def _():
        for t in range(gw):
            n2 = (s - ng - nsl) * gw + t
            b = _unpack2(a_s[pl.ds(pl.multiple_of(n2 * pitch, SUBLANES), 2 * H1), :])
            y = _dot(mi_ref[n2], b)
            for p, yp in enumerate(_split_lanes(y)):
                y_ref[p, :, t * SUBLANES:(t + 1) * SUBLANES, :] = yp.reshape(H1 // SUBLANES, SUBLANES, cb)


def _hy_conv(u_blk, slab_off, nj, g_all, order, consts):
    L = u_blk.shape[1]
    N2, cb = DFT_N2, HY_CB
    H1 = L // N2
    ks = _pick(H1, 16, 1)
    gw = _pick(N2, DFT_GROUP, 1)
    ng, nsl = N2 // gw, H1 // ks
    m_fwd, m_inv, f2_fwd, f2_inv, _ = consts
    assert H1 % SUBLANES == 0 and nj % 2 == 0 and slab_off % 2 == 0
    u4 = u_blk.reshape(u_blk.shape[0], H1 // SUBLANES, TT_CHUNK, cb)
    const3 = lambda shape: pl.BlockSpec(shape, lambda j, s: (0, 0, 0), pipeline_mode=pl.Buffered(1))
    y4 = pl.pallas_call(
        functools.partial(_conv_body, H1=H1, ks=ks, gw=gw, ng=ng, nsl=nsl), grid=(nj // 2, 2 * ng + nsl),
        in_specs=[pl.BlockSpec((2, H1 // SUBLANES, gw * SUBLANES, cb),
                               lambda j, s: (slab_off // 2 + j, 0, jnp.minimum(s, ng - 1), 0)),
                  pl.BlockSpec((None, 2, ks, 2, N2, cb),
                               lambda j, s: (order, j, jnp.clip(s - ng, 0, nsl - 1), 0, 0, 0)),
                  const3((N2, 2 * H1, H1)), const3((N2, H1, 2 * H1)),
                  pl.BlockSpec((2 * N2, 2 * N2), lambda j, s: (0, 0)),
                  pl.BlockSpec((2 * N2, 2 * N2), lambda j, s: (0, 0))],
        out_specs=pl.BlockSpec((2, H1 // SUBLANES, gw * SUBLANES, cb),
                               lambda j, s: (j, 0, jnp.clip(s - ng - nsl, 0, ng - 1), 0)),
        out_shape=jax.ShapeDtypeStruct((nj, H1 // SUBLANES, TT_CHUNK, cb), F32),
        scratch_shapes=[pltpu.VMEM((N2 * _pitch(2 * H1), cb), jnp.uint32)],
        compiler_params=_cparams(("parallel", "arbitrary"), vmem_mb=CONV_VMEM_MB), name="hy_conv",
    )(u4, g_all, m_fwd, m_inv, f2_fwd, f2_inv)
    return y4.reshape(nj, L, cb)


def _pos_features(L, emb):
    N2 = DFT_N2
    H1 = L // N2
    bands = (emb - 1) // 2
    r = jnp.arange(L)
    lag = (r % H1) * N2 + (r // H1)
    f = jnp.linspace(1e-4, bands - 1, bands, dtype=F32)[None, :]

    def feats(lg):
        lgf = lg.astype(F32)[:, None]
        t = lgf / (L - 1)
        w = 2.0 * math.pi * lgf / L
        z = jnp.concatenate([t, jnp.cos(f * w), -jnp.sin(f * w)], axis=-1)
        return jnp.pad(z, ((0, 0), (0, LANES - emb)))
    return jnp.concatenate([feats(lag), feats(jnp.where(lag == 0, 0, L - lag))], axis=-1)


def _filter_mlp_body(z_ref, w1_ref, b1_ref, w2_ref, b2_ref, w3_ref, b3_ref, fr_ref, o_ref):
    h = jnp.sin(fr_ref[0:1, :] * (_dot_f32(z_ref[...], w1_ref[...]) + b1_ref[...]))
    h = jnp.sin(fr_ref[1:2, :] * (_dot_f32(h, w2_ref[...]) + b2_ref[...]))
    h = jnp.sin(fr_ref[2:3, :] * (_dot_f32(h, w3_ref[...]) + b3_ref[...]))
    row = lax.broadcasted_iota(jnp.int32, h.shape, 0)
    lane = lax.broadcasted_iota(jnp.int32, h.shape, 1)
    h = jnp.where((pl.program_id(0) == 0) & (row == 0) & (lane >= LANES // 2), 0.0, h)
    o_ref[...] = h.astype(o_ref.dtype)


def _bdiag(w):
    a, b = w.shape
    z = jnp.zeros((a, b), w.dtype)
    return jnp.concatenate([jnp.concatenate([w, z], axis=1), jnp.concatenate([z, w], axis=1)], axis=0)


def _filter_mlp(z2, w1, b1, w2, b2, w3, b3, fr):
    L = z2.shape[0]
    emb, hd = w1.shape
    assert 2 * hd == LANES
    w1p = jnp.zeros((2 * LANES, LANES), F32).at[:emb, :hd].set(w1).at[LANES:LANES + emb, hd:].set(w1)
    tile2 = lambda v: jnp.concatenate([v, v], axis=-1)
    bm = _pick(L, 1024, SUBLANES)
    full = lambda a: pl.BlockSpec(a.shape, lambda i: (0,) * a.ndim)
    args = (w1p, tile2(b1).reshape(1, LANES), _bdiag(w2), tile2(b2).reshape(1, LANES),
            _bdiag(w3), tile2(b3).reshape(1, LANES), tile2(fr))
    return pl.pallas_call(
        _filter_mlp_body, grid=(L // bm,),
        in_specs=[pl.BlockSpec((bm, 2 * LANES), lambda i: (i, 0))] + [full(a) for a in args],
        out_specs=pl.BlockSpec((bm, LANES), lambda i: (i, 0)),
        out_shape=jax.ShapeDtypeStruct((L, LANES), BF16),
        compiler_params=_cparams(("parallel",)), name="hy_filter_mlp",
    )(z2, *args)


def _filt_body(hm_ref, w2_ref, ad_ref, m_ref, f2_ref, g_ref, a_s, g_s, inv_s, *, L, H1, ks):
    N2 = DFT_N2
    cb = 2 * HY_CB
    pitch = _pitch(2 * H1)
    s = pl.program_id(2)

    @pl.when(s == 0)
    def _():
        rate = ad_ref[...] * (1.0 / (L - 1))
        n1 = lax.broadcasted_iota(jnp.int32, (H1, cb), 0)
        e1f = jnp.exp(-(n1 * N2).astype(F32) * rate)
        e1b = jnp.exp(-((H1 - 1 - n1) * N2).astype(F32) * rate)

        def group(gi, nrm):
            def gen(t, nrm):
                n2 = gi * FILT_GROUP + t
                hs = hm_ref[pl.ds(pl.multiple_of(n2 * H1, SUBLANES), H1), :]
                gp = _dot(hs, w2_ref[...])
                n2f = n2.astype(F32)
                gf = gp[:, :cb] * (e1f * jnp.exp(-n2f * rate))
                gb = gp[:, cb:] * (e1b * jnp.exp(-(N2 - n2f) * rate))
                g_s[t, 0:H1, :] = gf.astype(BF16)
                g_s[t, H1:2 * H1, :] = gb.astype(BF16)
                return nrm + (jnp.abs(gf) + jnp.abs(gb))
            nrm = lax.fori_loop(0, FILT_GROUP, gen, nrm, unroll=DFT_UNROLL)

            def xform(t, carry):
                n2 = gi * FILT_GROUP + t
                a = _dot(m_ref[n2], g_s[t])
                a_s[pl.ds(pl.multiple_of(n2 * pitch, SUBLANES), 2 * H1), :] = _pack2(*_split_lanes(a))
                return carry
            lax.fori_loop(0, FILT_GROUP, xform, 0, unroll=DFT_UNROLL)
            return nrm
        nrm = lax.fori_loop(0, N2 // FILT_GROUP, group, jnp.zeros((H1, cb), F32))
        inv_s[...] = jnp.broadcast_to(1.0 / jnp.sum(nrm, axis=0, keepdims=True), inv_s.shape)

    scale = inv_s[0:1, :]

    def kbody(kk, carry):
        x = (_stage3(a_s, f2_ref, s * ks + kk, H1, pitch) * scale).astype(g_ref.dtype)
        for p, xp in enumerate(_split_lanes(x)):
            g_ref[p, kk, 0] = xp[:N2]
            g_ref[p, kk, 1] = xp[N2:]
        return carry
    lax.fori_loop(0, ks, kbody, 0, unroll=DFT_UNROLL)


def _hy_filters(hm, wout, C, L, consts):
    N2, cb = DFT_N2, HY_CB
    H1 = L // N2
    ks = _pick(H1, 16, 1)
    hd = wout.shape[0]
    nj = C // cb
    _, _, f2_fwd, _, m_filt = consts
    assert nj % 2 == 0
    pw = 2 * cb
    w4 = wout.reshape(hd, 2, 2, nj // 2, pw)
    wf = jnp.transpose(w4[:, :, 0], (1, 2, 0, 3))
    wb = -jnp.transpose(w4[:, :, 1], (1, 2, 0, 3))
    z = jnp.zeros_like(wf)
    w2 = jnp.concatenate([jnp.concatenate([wf, z], axis=-1), jnp.concatenate([z, wb], axis=-1)], axis=2)
    w2 = w2.astype(BF16)
    deltas = jnp.linspace(math.log(HY_FAST_DECAY) / HY_DECAY_TARGET,
                          math.log(HY_SLOW_DECAY) / HY_DECAY_TARGET, C, dtype=F32)
    absd = jnp.abs(deltas).reshape(1, C)
    return pl.pallas_call(
        functools.partial(_filt_body, L=L, H1=H1, ks=ks), grid=(2, nj // 2, H1 // ks),
        in_specs=[pl.BlockSpec((L, LANES), lambda o, j, s: (0, 0)),
                  pl.BlockSpec((None, None, 2 * hd, 2 * pw), lambda o, j, s: (o, j, 0, 0)),
                  pl.BlockSpec((1, pw), lambda o, j, s: (0, j)),
                  pl.BlockSpec((N2, 2 * H1, 2 * H1), lambda o, j, s: (0, 0, 0), pipeline_mode=pl.Buffered(1)),
                  pl.BlockSpec((2 * N2, 2 * N2), lambda o, j, s: (0, 0))],
        out_specs=pl.BlockSpec((None, 2, ks, 2, N2, cb), lambda o, j, s: (o, j, s, 0, 0, 0)),
        out_shape=jax.ShapeDtypeStruct((2, nj, H1, 2, N2, cb), SPEC_DTYPE),
        scratch_shapes=[pltpu.VMEM((N2 * _pitch(2 * H1), cb), jnp.uint32),
                        pltpu.VMEM((FILT_GROUP, 2 * H1, pw), BF16), pltpu.VMEM((SUBLANES, pw), F32)],
        compiler_params=_cparams(("parallel", "parallel", "arbitrary")), name="hy_filters",
    )(hm, w2, absd, m_filt, f2_fwd)


def _hy_gate1_body(gate_ref, v_ref, y_ref, sk_ref, o_ref):
    o_ref[...] = gate_ref[...] * (y_ref[...] + sk_ref[...] * v_ref[...])


def _hy_gate1(uh_blk, gate_blk, v_blk, y_blk, sk):
    nj, L, cb = y_blk.shape
    bm = _pick(L, 256, SUBLANES)
    blk = lambda g: pl.BlockSpec((nj, bm, cb), lambda i: (g, i, 0))
    return pl.pallas_call(
        _hy_gate1_body, grid=(L // bm,),
        in_specs=[blk(gate_blk), blk(v_blk), blk(0), pl.BlockSpec((nj, 1, cb), lambda i: (0, 0, 0))],
        out_specs=blk(0), out_shape=jax.ShapeDtypeStruct((nj, L, cb), F32),
        compiler_params=_cparams(("parallel",)), name="hy_gate1",
    )(uh_blk, uh_blk, y_blk, sk.reshape(nj, 1, cb))


def _hy_gate2_body(gate_ref, v_ref, y_ref, sk_ref, g_ref, o_ref, *, gw):
    nj, _, cb = gate_ref.shape
    for r in range(SUBLANES):
        rows = pl.ds(r, gw, stride=SUBLANES)
        zs = []
        ss = jnp.zeros((gw, cb), F32)
        for k in range(nj):
            z = gate_ref.at[k][rows, :] * (y_ref.at[k][rows, :] + sk_ref[k] * v_ref.at[k][rows, :])
            zs.append(z)
            ss = ss + z * z
        ms = jnp.sum(ss, axis=-1, keepdims=True) * (1.0 / (nj * cb))
        inv = lax.rsqrt(ms + NORM_EPS)
        for k in range(nj):
            o_ref[r, :, k * cb:(k + 1) * cb] = (zs[k] * inv * g_ref[k]).astype(o_ref.dtype)


def _hy_gate2(uh_blk, gate_blk, v_blk, y_blk, sk, g):
    nj, L, cb = y_blk.shape
    N2 = DFT_N2
    nch = L // TT_CHUNK
    gw = _pick(N2, DFT_GROUP, 16)
    ngr = N2 // gw
    blk = lambda gidx: pl.BlockSpec((nj, gw * SUBLANES, cb), lambda c, q: (gidx, c * ngr + q, 0))
    vec = pl.BlockSpec((nj, 1, cb), lambda c, q: (0, 0, 0))
    out = pl.pallas_call(
        functools.partial(_hy_gate2_body, gw=gw), grid=(nch, ngr),
        in_specs=[blk(gate_blk), blk(0), blk(0), vec, vec],
        out_specs=pl.BlockSpec((None, SUBLANES, gw, nj * cb), lambda c, q: (c, 0, q, 0)),
        out_shape=jax.ShapeDtypeStruct((nch, SUBLANES, N2, nj * cb), BF16),
        compiler_params=_cparams(("parallel", "parallel")), name="hy_gate2",
    )(uh_blk, v_blk, y_blk, sk.reshape(nj, 1, cb), g.reshape(nj, 1, cb))
    return out.reshape(L, nj * cb)


def kernel(x, norm_mix_g, w_in, hy_conv_w, hy_conv_b, hy_pos_w1, hy_pos_b1, hy_pos_w2, hy_pos_b2, hy_pos_w3, hy_pos_b3, hy_sin_freq, hy_pos_wout, hy_skip, hy_out_g, ssm_conv_w, ssm_conv_b, ssm_A_log, ssm_dt_bias, ssm_D, ssm_out_g, w_out, norm_mlp_g, w_up, w_down, final_norm_g):
    B, L, D = x.shape
    depth = w_in.shape[0]
    CH = hy_out_g.shape[1]
    WS = ssm_out_g.shape[1]
    heads = ssm_D.shape[1]
    xbc_w = ssm_conv_w.shape[2]
    groups = (xbc_w - WS) // (2 * SSM_STATE)
    o1 = 3 * CH
    o2 = o1 + WS
    o3 = o2 + xbc_w
    assert WS == heads * SSM_HEAD_DIM and heads <= LANES and L % DFT_N2 == 0 and CH % HY_CB == 0

    consts = _dft_consts(L)
    z2 = _pos_features(L, hy_pos_w1.shape[1])
    w_in_dt = jnp.pad(w_in[:, :, o3:], ((0, 0), (0, 0), (0, LANES - heads))).astype(BF16)
    w_out_b = w_out.astype(BF16)
    w_up_b = w_up.astype(BF16)
    w_down_b = w_down.astype(BF16)
    pad_h = lambda v: jnp.pad(v, (0, LANES - heads)).reshape(1, LANES)
    d_rep = jnp.repeat(ssm_D, SSM_HEAD_DIM, axis=-1)

    outs = []
    for b in range(B):
        xb = x[b]
        delta = None
        for l in range(depth):
            if delta is None:
                h = _rmsnorm(xb, norm_mix_g[l], BF16)
            else:
                xb, h = _add_rmsnorm(xb, delta, norm_mix_g[l], BF16)
            proj = _matmul(h, w_in, l, n_cols=o3, out_dtype=PROJ_DTYPE, bn_t=512)
            dt_raw = _matmul(h, w_in_dt, l, bm_t=2048)

            nj = CH // HY_CB
            uh = _dwconv(proj, 0, o1, hy_conv_w[l], hy_conv_b[l], "none", blocked_out=True)
            hm = _filter_mlp(z2, hy_pos_w1[l], hy_pos_b1[l], hy_pos_w2[l], hy_pos_b2[l],
                             hy_pos_w3[l], hy_pos_b3[l], hy_sin_freq[l])
            g_all = _hy_filters(hm, hy_pos_wout[l], CH, L, consts)
            y0 = _hy_conv(uh, 2 * nj, nj, g_all, 0, consts)
            z1 = _hy_gate1(uh, 0, 2, y0, hy_skip[l, 0])
            y1 = _hy_conv(z1, 0, nj, g_all, 1, consts)
            y_hy = _hy_gate2(uh, 1, z1, y1, hy_skip[l, 1], hy_out_g[l])

            xbc = _dwconv(proj, o2, xbc_w, ssm_conv_w[l], ssm_conv_b[l], "silu")
            yf = _ssd(xbc, dt_raw, pad_h(ssm_dt_bias[l, 0]), pad_h(ssm_A_log[l, 0]),
                      reverse=False, heads=heads, groups=groups)
            y_ssm = _ssd(xbc, dt_raw, pad_h(ssm_dt_bias[l, 1]), pad_h(ssm_A_log[l, 1]),
                         reverse=True, heads=heads, groups=groups,
                         gate_with=(yf, proj, o1, d_rep[l], ssm_out_g[l]))

            xb = _out_matmul(y_hy, y_ssm, w_out_b, l, xb)
            h2 = _rmsnorm(xb, norm_mlp_g[l], BF16)
            delta = _mlp(h2, w_up_b, w_down_b, l)
        outs.append(_add_rmsnorm(xb, delta, final_norm_g, F32, keep_sum=False))
    return jnp.stack(outs, axis=0)
```

```python
import functools
import math

import numpy as np
import jax
import jax.numpy as jnp
from jax import lax
from jax.experimental import pallas as pl
from jax.experimental.pallas import tpu as pltpu

F32 = jnp.float32
BF16 = jnp.bfloat16

NORM_EPS = 1e-5
SSM_HEAD_DIM = 64
SSM_STATE = 128
SSM_CHUNK = 256
HY_FAST_DECAY = 0.3
HY_SLOW_DECAY = 1.5
HY_DECAY_TARGET = 1e-2

LANES = 128
SUBLANES = 8
DFT_N2 = 128
HY_CB = 128
DFT_UNROLL = 16
DFT_GROUP = 32
TT_CHUNK = SUBLANES * DFT_N2
FILT_GROUP = 16
SPEC_DTYPE = BF16
PROJ_DTYPE = BF16
VMEM_LIMIT_MB = 56
CONV_VMEM_MB = 60


def _cparams(sem, vmem_mb=VMEM_LIMIT_MB):
    return pltpu.CompilerParams(dimension_semantics=sem, vmem_limit_bytes=vmem_mb * 1024 * 1024)


def _pick(n, target, mult):
    best = None
    for d in range(mult, min(n, target) + 1, mult):
        if n % d == 0:
            best = d
    assert best is not None, (n, target, mult)
    return best


def _dot(a, b):
    return jnp.dot(a, b, preferred_element_type=F32)


def _split3(a):
    hi = a.astype(BF16)
    r1 = a - hi.astype(F32)
    mid = r1.astype(BF16)
    lo = (r1 - mid.astype(F32)).astype(BF16)
    return hi, mid, lo


def _dot_f32(a, b):
    ah = a.astype(BF16)
    al = (a - ah.astype(F32)).astype(BF16)
    bh = b.astype(BF16)
    bl = (b - bh.astype(F32)).astype(BF16)
    return _dot(ah, bh) + (_dot(ah, bl) + _dot(al, bh))


def _silu(x):
    return x / (1.0 + jnp.exp(-x))


def _rmsnorm_body(x_ref, g_ref, h_ref):
    x = x_ref[...]
    ms = jnp.mean(x * x, axis=-1, keepdims=True)
    h_ref[...] = (x * lax.rsqrt(ms + NORM_EPS) * g_ref[...]).astype(h_ref.dtype)


def _rmsnorm(x, g, out_dtype):
    M, D = x.shape
    bm = _pick(M, 256, SUBLANES)
    return pl.pallas_call(
        _rmsnorm_body, grid=(M // bm,),
        in_specs=[pl.BlockSpec((bm, D), lambda i: (i, 0)), pl.BlockSpec((1, D), lambda i: (0, 0))],
        out_specs=pl.BlockSpec((bm, D), lambda i: (i, 0)),
        out_shape=jax.ShapeDtypeStruct((M, D), out_dtype),
        compiler_params=_cparams(("parallel",)), name="rmsnorm",
    )(x, g.reshape(1, D))


def _add_rmsnorm_body(x_ref, d_ref, g_ref, *out_refs):
    x = x_ref[...] + d_ref[...]
    if len(out_refs) == 2:
        out_refs[0][...] = x
    h_ref = out_refs[-1]
    ms = jnp.mean(x * x, axis=-1, keepdims=True)
    h_ref[...] = (x * lax.rsqrt(ms + NORM_EPS) * g_ref[...]).astype(h_ref.dtype)


def _add_rmsnorm(x, delta, g, out_dtype, keep_sum=True):
    M, D = x.shape
    bm = _pick(M, 256, SUBLANES)
    row = pl.BlockSpec((bm, D), lambda i: (i, 0))
    h_shape = jax.ShapeDtypeStruct((M, D), out_dtype)
    return pl.pallas_call(
        _add_rmsnorm_body, grid=(M // bm,),
        in_specs=[row, row, pl.BlockSpec((1, D), lambda i: (0, 0))],
        out_specs=[row, row] if keep_sum else row,
        out_shape=[jax.ShapeDtypeStruct((M, D), F32), h_shape] if keep_sum else h_shape,
        compiler_params=_cparams(("parallel",)), name="add_rmsnorm",
    )(x, delta, g.reshape(1, D))


def _mm_body(a_ref, b_ref, o_ref):
    o_ref[...] = _dot(a_ref[...], b_ref[...]).astype(o_ref.dtype)


def _matmul(a, b_stack, layer, n_cols=None, out_dtype=F32, bm_t=1024, bn_t=1024):
    M, K = a.shape
    N = b_stack.shape[2] if n_cols is None else n_cols
    bm = _pick(M, bm_t, SUBLANES)
    bn = _pick(N, bn_t, LANES)
    return pl.pallas_call(
        _mm_body, grid=(M // bm, N // bn),
        in_specs=[pl.BlockSpec((bm, K), lambda i, j: (i, 0)),
                  pl.BlockSpec((None, K, bn), lambda i, j: (layer, 0, j))],
        out_specs=pl.BlockSpec((bm, bn), lambda i, j: (i, j)),
        out_shape=jax.ShapeDtypeStruct((M, N), out_dtype),
        compiler_params=_cparams(("parallel", "parallel")), name="proj_matmul",
    )(a, b_stack)


def _out_mm_body(a1_ref, a2_ref, b1_ref, b2_ref, r_ref, o_ref):
    acc = _dot(a1_ref[...], b1_ref[...]) + _dot(a2_ref[...], b2_ref[...])
    o_ref[...] = r_ref[...] + acc


def _out_matmul(a1, a2, b_stack, layer, res):
    M, K1 = a1.shape
    assert a2.shape[1] == K1 and b_stack.shape[1] == 2 * K1
    N = b_stack.shape[2]
    bm = _pick(M, 1024, SUBLANES)
    bn = _pick(N, 1024, LANES)
    return pl.pallas_call(
        _out_mm_body, grid=(M // bm, N // bn),
        in_specs=[pl.BlockSpec((bm, K1), lambda i, j: (i, 0)), pl.BlockSpec((bm, K1), lambda i, j: (i, 0)),
                  pl.BlockSpec((None, K1, bn), lambda i, j: (layer, 0, j)),
                  pl.BlockSpec((None, K1, bn), lambda i, j: (layer, 1, j)),
                  pl.BlockSpec((bm, bn), lambda i, j: (i, j))],
        out_specs=pl.BlockSpec((bm, bn), lambda i, j: (i, j)),
        out_shape=jax.ShapeDtypeStruct((M, N), F32),
        compiler_params=_cparams(("parallel", "parallel")), name="out_matmul",
    )(a1, a2, b_stack, b_stack, res)


def _mlp_body(h_ref, wu_ref, wd_ref, o_ref, *, cw):
    j = pl.program_id(1)

    @pl.when(j == 0)
    def _():
        o_ref[...] = jnp.zeros_like(o_ref)

    hw = wu_ref.shape[1] // 2
    ts = []
    for q in range(2):
        t = _dot(h_ref[...], wu_ref[:, q * hw:(q + 1) * hw])
        ts.append(jnp.square(jnp.maximum(t, 0.0)).astype(BF16))
    for c in range(o_ref.shape[1] // cw):
        cols = slice(c * cw, (c + 1) * cw)
        o_ref[:, cols] += _dot(ts[0], wd_ref[0:hw, cols]) + _dot(ts[1], wd_ref[hw:2 * hw, cols])


def _mlp(h, w_up_stack, w_down_stack, layer):
    M, D = h.shape
    F = w_up_stack.shape[2]
    bm = _pick(M, 512, SUBLANES)
    bf = _pick(F, 512, LANES)
    cw = _pick(D, 512, LANES)
    return pl.pallas_call(
        functools.partial(_mlp_body, cw=cw), grid=(M // bm, F // bf),
        in_specs=[pl.BlockSpec((bm, D), lambda i, j: (i, 0)),
                  pl.BlockSpec((None, D, bf), lambda i, j: (layer, 0, j)),
                  pl.BlockSpec((None, bf, D), lambda i, j: (layer, j, 0))],
        out_specs=pl.BlockSpec((bm, D), lambda i, j: (i, 0)),
        out_shape=jax.ShapeDtypeStruct((M, D), F32),
        compiler_params=_cparams(("parallel", "arbitrary")), name="mlp",
    )(h, w_up_stack, w_down_stack)


def _dwconv_body(p_ref, pv_ref, nx_ref, w_ref, b_ref, o_ref, *, act):
    i = pl.program_id(0)
    last = pl.num_programs(0) - 1
    u = p_ref[...].astype(F32)
    bm = u.shape[0]
    hr = pv_ref.shape[0]
    prev_row = jnp.where(i > 0, pv_ref[hr - 1:hr, :].astype(F32), 0.0)
    next_row = jnp.where(i < last, nx_ref[0:1, :].astype(F32), 0.0)
    rows = lax.broadcasted_iota(jnp.int32, u.shape, 0)
    um = jnp.where(rows == 0, prev_row, pltpu.roll(u, 1, 0))
    up = jnp.where(rows == bm - 1, next_row, pltpu.roll(u, bm - 1, 0))
    y = w_ref[0:1, :] * um
    y = y + w_ref[1:2, :] * u
    y = y + w_ref[2:3, :] * up
    y = y + b_ref[...]
    if act == "silu":
        y = _silu(y)
    if len(o_ref.shape) == 3:
        assert bm == TT_CHUNK
        for k in range(o_ref.shape[0]):
            for r in range(SUBLANES):
                o_ref.at[k][pl.ds(r, DFT_N2, stride=SUBLANES), :] = (
                    y[r * DFT_N2:(r + 1) * DFT_N2, k * LANES:(k + 1) * LANES])
    else:
        o_ref[...] = y


def _dwconv(p, col_off, C, w, b, act, blocked_out=False):
    L = p.shape[0]
    bm = TT_CHUNK if blocked_out else _pick(L, 1024, SUBLANES)
    assert L % bm == 0
    bc = _pick(math.gcd(C, col_off) if col_off else C, 1024, LANES)
    co = col_off // bc
    hr = SUBLANES * (4 // p.dtype.itemsize)
    hb = bm // hr
    nrb = L // hr
    if blocked_out:
        spb = bc // LANES
        out_spec = pl.BlockSpec((spb, bm, LANES), lambda i, j: (j, i, 0))
        out_shape = jax.ShapeDtypeStruct((C // LANES, L, LANES), F32)
    else:
        out_spec = pl.BlockSpec((bm, bc), lambda i, j: (i, j))
        out_shape = jax.ShapeDtypeStruct((L, C), F32)
    return pl.pallas_call(
        functools.partial(_dwconv_body, act=act), grid=(L // bm, C // bc),
        in_specs=[pl.BlockSpec((bm, bc), lambda i, j: (i, co + j)),
                  pl.BlockSpec((hr, bc), lambda i, j: (jnp.maximum(i * hb - 1, 0), co + j)),
                  pl.BlockSpec((hr, bc), lambda i, j: (jnp.minimum((i + 1) * hb, nrb - 1), co + j)),
                  pl.BlockSpec((3, bc), lambda i, j: (0, j)),
                  pl.BlockSpec((1, bc), lambda i, j: (0, j))],
        out_specs=out_spec, out_shape=out_shape,
        compiler_params=_cparams(("parallel", "parallel")), name="dwconv_" + act,
    )(p, p, p, w, b.reshape(1, C))


def _expand_heads(v, ex_ref):
    hi = v.astype(BF16)
    lo = (v - hi.astype(F32)).astype(BF16)
    return _dot(hi, ex_ref[...]) + _dot(lo, ex_ref[...])


def _ssd_body(xbc_ref, dt_ref, bias_ref, alog_ref, ex_ref, *rest, reverse, heads, groups, gated):
    if gated:
        yo_ref, z_ref, d_ref, gn_ref, y_ref, s_ref = rest
    else:
        y_ref, s_ref = rest
    Q = SSM_CHUNK
    P = SSM_HEAD_DIM
    N = SSM_STATE
    R = heads // groups
    W = heads * P
    step = pl.program_id(0)

    @pl.when(step == 0)
    def _():
        s_ref[...] = jnp.zeros_like(s_ref)

    x = dt_ref[...] + bias_ref[...]
    dt = jnp.maximum(x, 0.0) + jnp.log1p(jnp.exp(-jnp.abs(x)))
    a = dt * (-jnp.exp(alog_ref[...]))
    ri = lax.broadcasted_iota(jnp.int32, (Q, Q), 0)
    ci = lax.broadcasted_iota(jnp.int32, (Q, Q), 1)
    mask = (ci >= ri) if reverse else (ri >= ci)
    tri = jnp.where(mask, 1.0, 0.0).astype(BF16)
    ah, am, al = _split3(a)
    acs = _dot(tri, ah) + (_dot(tri, am) + _dot(tri, al))
    tot = acs[0:1, :] if reverse else acs[Q - 1:Q, :]
    e_all = _expand_heads(jnp.exp(acs), ex_ref)
    w_all = _expand_heads(jnp.exp(tot - acs) * dt, ex_ref)
    t_all = _expand_heads(jnp.broadcast_to(jnp.exp(tot), (SUBLANES, LANES)), ex_ref)[0:1, :]
    acs_row = acs.T
    dt_row = dt.T

    load_b = lambda g: xbc_ref[:, W + g * N:W + (g + 1) * N].astype(BF16)
    load_c = lambda g: xbc_ref[:, W + groups * N + g * N:W + groups * N + (g + 1) * N].astype(BF16)
    load_x = lambda g: xbc_ref[:, g * R * P:(g + 1) * R * P]
    lane = lax.broadcasted_iota(jnp.int32, (Q, R * P), 1)

    cbs, y_offs = [], []
    for g in range(groups):
        cg = load_c(g)
        cbs.append(lax.dot_general(cg, load_b(g), (((1,), (1,)), ((), ())), preferred_element_type=F32))
        y_offs.append(_dot(cg, s_ref[g].astype(BF16)))
    for g in range(groups):
        gl = slice(g * R * P, (g + 1) * R * P)
        s_new = lax.dot_general(load_b(g), (load_x(g) * w_all[:, gl]).astype(BF16), (((0,), (0,)), ((), ())),
                                preferred_element_type=F32)
        s_ref[g] = s_ref[g] * t_all[:, gl] + s_new
    for g in range(groups):
        xg = load_x(g)
        ws, xs = [], []
        for r in range(R):
            h = g * R + r
            diff = acs[:, h:h + 1] - acs_row[h:h + 1, :]
            lm = jnp.exp(jnp.where(mask, diff, -jnp.inf))
            ws.append((cbs[g] * lm * dt_row[h:h + 1, :]).astype(BF16))
            xs.append(jnp.where((lane >= r * P) & (lane < (r + 1) * P), xg, 0.0).astype(BF16))
        y_diag = _dot(jnp.concatenate(ws, axis=1), jnp.concatenate(xs, axis=0))
        gl = slice(g * R * P, (g + 1) * R * P)
        y = y_diag + y_offs[g] * e_all[:, gl]
        if gated:
            y = (y + yo_ref[:, gl] + d_ref[:, gl] * xg) * _silu(z_ref[:, gl].astype(F32))
            ms = jnp.mean(y * y, axis=-1, keepdims=True)
            y = y * lax.rsqrt(ms + NORM_EPS) * gn_ref[:, gl]
        y_ref[:, gl] = y.astype(y_ref.dtype)


def _ssd(xbc, dt_raw, dt_bias, a_log, *, reverse, heads, groups, gate_with=None):
    L = xbc.shape[0]
    Q = SSM_CHUNK
    assert L % Q == 0
    nc = L // Q
    W = heads * SSM_HEAD_DIM
    R = heads // groups
    cmap = (lambda i: (nc - 1 - i, 0)) if reverse else (lambda i: (i, 0))
    ex = np.zeros((LANES, W), np.float32)
    ex[np.arange(W) // SSM_HEAD_DIM, np.arange(W)] = 1.0
    in_specs = [pl.BlockSpec((Q, xbc.shape[1]), cmap), pl.BlockSpec((Q, LANES), cmap),
                pl.BlockSpec((1, LANES), lambda i: (0, 0)), pl.BlockSpec((1, LANES), lambda i: (0, 0)),
                pl.BlockSpec((LANES, W), lambda i: (0, 0))]
    args = [xbc, dt_raw, dt_bias, a_log, jnp.asarray(ex.astype(BF16))]
    if gate_with is not None:
        y_other, proj, z_off, d_rep, norm_g = gate_with
        assert z_off % W == 0 and W // groups == R * SSM_HEAD_DIM
        zo = z_off // W
        vec = pl.BlockSpec((1, W), lambda i: (0, 0))
        in_specs += [pl.BlockSpec((Q, W), cmap), pl.BlockSpec((Q, W), lambda i: (cmap(i)[0], zo)), vec, vec]
        args += [y_other, proj, d_rep.reshape(1, W), norm_g.reshape(1, W)]
    return pl.pallas_call(
        functools.partial(_ssd_body, reverse=reverse, heads=heads, groups=groups, gated=gate_with is not None),
        grid=(nc,), in_specs=in_specs,
        out_specs=pl.BlockSpec((Q, W), cmap),
        out_shape=jax.ShapeDtypeStruct((L, W), F32 if gate_with is None else BF16),
        scratch_shapes=[pltpu.VMEM((groups, SSM_STATE, R * SSM_HEAD_DIM), F32)],
        compiler_params=_cparams(("arbitrary",)), name="ssd_bwd" if reverse else "ssd_fwd",
    )(*args)


def _dft_consts(L):
    N2 = DFT_N2
    N = 2 * L
    H1 = L // N2
    n2 = np.arange(N2)[:, None, None]
    k1 = np.arange(H1)[None, :, None]
    n1 = np.arange(H1)[None, None, :]
    m = ((2 * k1 + 1) * (N2 * n1 + n2)) % (2 * N)
    ph = np.pi * m.astype(np.float64) / N
    m_fwd = np.concatenate([np.cos(ph), -np.sin(ph)], axis=1)
    m_inv = (2.0 / N) * np.concatenate([np.cos(ph), -np.sin(ph)], axis=1).transpose(0, 2, 1)
    sg = np.where(np.arange(H1) % 2 == 0, 1.0, -1.0)[None, :, None]
    m_filt = np.concatenate([np.concatenate([np.cos(ph), -sg * np.sin(ph)], axis=2),
                             np.concatenate([-np.sin(ph), -sg * np.cos(ph)], axis=2)], axis=1)
    kk = (np.arange(N2)[:, None] * np.arange(N2)[None, :]) % N2
    c2 = np.cos(2 * np.pi * kk / N2)
    s2 = np.sin(2 * np.pi * kk / N2)
    f2_fwd = np.block([[c2, s2], [-s2, c2]])
    f2_inv = np.block([[c2, -s2], [s2, c2]])
    as_bf = lambda a: jnp.asarray(a.astype(BF16))
    return as_bf(m_fwd), as_bf(m_inv), as_bf(f2_fwd), as_bf(f2_inv), as_bf(m_filt)


def _pitch(rows):
    p = rows + SUBLANES
    assert (p // SUBLANES) % 2 == 1
    return p


def _pack2(a, b):
    ha = lax.bitcast_convert_type(a.astype(BF16).astype(F32), jnp.uint32)
    hb = lax.bitcast_convert_type(b.astype(BF16).astype(F32), jnp.uint32)
    return ha | (hb >> 16)


def _unpack2(w):
    a = lax.bitcast_convert_type(w & jnp.uint32(0xFFFF0000), F32)
    b = lax.bitcast_convert_type(w << 16, F32)
    return jnp.concatenate([a, b], axis=1).astype(BF16)


def _split_lanes(x):
    c = x.shape[1] // 2
    return x[:, :c], x[:, c:]


def _stage3(a_s, f2_ref, k1, H1, pitch):
    N2 = DFT_N2
    ar = _unpack2(a_s[pl.ds(k1, N2, stride=pitch), :])
    ai = _unpack2(a_s[pl.ds(H1 + k1, N2, stride=pitch), :])
    return _dot(f2_ref[...], jnp.concatenate([ar, ai], axis=0))


def _conv_body(u_ref, g_ref, mf_ref, mi_ref, f2f_ref, f2i_ref, y_ref, a_s, *, H1, ks, gw, ng, nsl):
    N2, cb = DFT_N2, HY_CB
    pitch = _pitch(2 * H1)
    s = pl.program_id(1)

    @pl.when(s < ng)
    def _():
        for t in range(gw):
            n2 = s * gw + t
            rows = slice(t * SUBLANES, (t + 1) * SUBLANES)
            u = jnp.concatenate([u_ref[0, :, rows, :].reshape(H1, cb), u_ref[1, :, rows, :].reshape(H1, cb)],
                                axis=1).astype(BF16)
            a = _dot(mf_ref[n2], u)
            a_s[pl.ds(pl.multiple_of(n2 * pitch, SUBLANES), 2 * H1), :] = _pack2(*_split_lanes(a))

    @pl.when((s >= ng) & (s < ng + nsl))
    def _():
        k0 = (s - ng) * ks
        xs = [_stage3(a_s, f2f_ref, k0 + kk, H1, pitch).astype(BF16) for kk in range(ks)]
        bs = []
        for kk in range(ks):
            gr = jnp.concatenate([g_ref[0, kk, 0], g_ref[1, kk, 0]], axis=1)
            gi = jnp.concatenate([g_ref[0, kk, 1], g_ref[1, kk, 1]], axis=1)
            xr, xi = xs[kk][:N2], xs[kk][N2:]
            y = jnp.concatenate([xr * gr - xi * gi, xr * gi + xi * gr], axis=0)
            bs.append(_pack2(*_split_lanes(_dot(f2i_ref[...], y))))
        for kk in range(ks):
            a_s[pl.ds(k0 + kk, N2, stride=pitch), :] = bs[kk][:N2]
            a_s[pl.ds(H1 + k0 + kk, N2, stride=pitch), :] = bs[kk][N2:]

    @pl.when(s >= ng + nsl)
    def _():
        for t in range(gw):
            n2 = (s - ng - nsl) * gw + t
            b = _unpack2(a_s[pl.ds(pl.multiple_of(n2 * pitch, SUBLANES), 2 * H1), :])
            y = _dot(mi_ref[n2], b)
            for p, yp in enumerate(_split_lanes(y)):
                y_ref[p, :, t * SUBLANES:(t + 1) * SUBLANES, :] = yp.reshape(H1 // SUBLANES, SUBLANES, cb)


def _hy_conv(u_blk, slab_off, nj, g_all, order, consts):
    L = u_blk.shape[1]
    N2, cb = DFT_N2, HY_CB
    H1 = L // N2
    ks = _pick(H1, 16, 1)
    gw = _pick(N2, DFT_GROUP, 1)
    ng, nsl = N2 // gw, H1 // ks
    m_fwd, m_inv, f2_fwd, f2_inv, _ = consts
    assert H1 % SUBLANES == 0 and nj % 2 == 0 and slab_off % 2 == 0
    u4 = u_blk.reshape(u_blk.shape[0], H1 // SUBLANES, TT_CHUNK, cb)
    const3 = lambda shape: pl.BlockSpec(shape, lambda j, s: (0, 0, 0), pipeline_mode=pl.Buffered(1))
    y4 = pl.pallas_call(
        functools.partial(_conv_body, H1=H1, ks=ks, gw=gw, ng=ng, nsl=nsl), grid=(nj // 2, 2 * ng + nsl),
        in_specs=[pl.BlockSpec((2, H1 // SUBLANES, gw * SUBLANES, cb),
                               lambda j, s: (slab_off // 2 + j, 0, jnp.minimum(s, ng - 1), 0)),
                  pl.BlockSpec((None, 2, ks, 2, N2, cb),
                               lambda j, s: (order, j, jnp.clip(s - ng, 0, nsl - 1), 0, 0, 0)),
                  const3((N2, 2 * H1, H1)), const3((N2, H1, 2 * H1)),
                  pl.BlockSpec((2 * N2, 2 * N2), lambda j, s: (0, 0)),
                  pl.BlockSpec((2 * N2, 2 * N2), lambda j, s: (0, 0))],
        out_specs=pl.BlockSpec((2, H1 // SUBLANES, gw * SUBLANES, cb),
                               lambda j, s: (j, 0, jnp.clip(s - ng - nsl, 0, ng - 1), 0)),
        out_shape=jax.ShapeDtypeStruct((nj, H1 // SUBLANES, TT_CHUNK, cb), F32),
        scratch_shapes=[pltpu.VMEM((N2 * _pitch(2 * H1), cb), jnp.uint32)],
        compiler_params=_cparams(("parallel", "arbitrary"), vmem_mb=CONV_VMEM_MB), name="hy_conv",
    )(u4, g_all, m_fwd, m_inv, f2_fwd, f2_inv)
    return y4.reshape(nj, L, cb)


def _pos_features(L, emb):
    N2 = DFT_N2
    H1 = L // N2
    bands = (emb - 1) // 2
    r = jnp.arange(L)
    lag = (r % H1) * N2 + (r // H1)
    f = jnp.linspace(1e-4, bands - 1, bands, dtype=F32)[None, :]

    def feats(lg):
        lgf = lg.astype(F32)[:, None]
        t = lgf / (L - 1)
        w = 2.0 * math.pi * lgf / L
        z = jnp.concatenate([t, jnp.cos(f * w), -jnp.sin(f * w)], axis=-1)
        return jnp.pad(z, ((0, 0), (0, LANES - emb)))
    return jnp.concatenate([feats(lag), feats(jnp.where(lag == 0, 0, L - lag))], axis=-1)


def _filter_mlp_body(z_ref, w1_ref, b1_ref, w2_ref, b2_ref, w3_ref, b3_ref, fr_ref, o_ref):
    h = jnp.sin(fr_ref[0:1, :] * (_dot_f32(z_ref[...], w1_ref[...]) + b1_ref[...]))
    h = jnp.sin(fr_ref[1:2, :] * (_dot_f32(h, w2_ref[...]) + b2_ref[...]))
    h = jnp.sin(fr_ref[2:3, :] * (_dot_f32(h, w3_ref[...]) + b3_ref[...]))
    row = lax.broadcasted_iota(jnp.int32, h.shape, 0)
    lane = lax.broadcasted_iota(jnp.int32, h.shape, 1)
    h = jnp.where((pl.program_id(0) == 0) & (row == 0) & (lane >= LANES // 2), 0.0, h)
    o_ref[...] = h.astype(o_ref.dtype)


def _bdiag(w):
    a, b = w.shape
    z = jnp.zeros((a, b), w.dtype)
    return jnp.concatenate([jnp.concatenate([w, z], axis=1), jnp.concatenate([z, w], axis=1)], axis=0)


def _filter_mlp(z2, w1, b1, w2, b2, w3, b3, fr):
    L = z2.shape[0]
    emb, hd = w1.shape
    assert 2 * hd == LANES
    w1p = jnp.zeros((2 * LANES, LANES), F32).at[:emb, :hd].set(w1).at[LANES:LANES + emb, hd:].set(w1)
    tile2 = lambda v: jnp.concatenate([v, v], axis=-1)
    bm = _pick(L, 1024, SUBLANES)
    full = lambda a: pl.BlockSpec(a.shape, lambda i: (0,) * a.ndim)
    args = (w1p, tile2(b1).reshape(1, LANES), _bdiag(w2), tile2(b2).reshape(1, LANES),
            _bdiag(w3), tile2(b3).reshape(1, LANES), tile2(fr))
    return pl.pallas_call(
        _filter_mlp_body, grid=(L // bm,),
        in_specs=[pl.BlockSpec((bm, 2 * LANES), lambda i: (i, 0))] + [full(a) for a in args],
        out_specs=pl.BlockSpec((bm, LANES), lambda i: (i, 0)),
        out_shape=jax.ShapeDtypeStruct((L, LANES), BF16),
        compiler_params=_cparams(("parallel",)), name="hy_filter_mlp",
    )(z2, *args)


def _filt_body(hm_ref, w2_ref, ad_ref, m_ref, f2_ref, g_ref, a_s, g_s, inv_s, *, L, H1, ks):
    N2 = DFT_N2
    cb = 2 * HY_CB
    pitch = _pitch(2 * H1)
    s = pl.program_id(2)

    @pl.when(s == 0)
    def _():
        rate = ad_ref[...] * (1.0 / (L - 1))
        n1 = lax.broadcasted_iota(jnp.int32, (H1, cb), 0)
        e1f = jnp.exp(-(n1 * N2).astype(F32) * rate)
        e1b = jnp.exp(-((H1 - 1 - n1) * N2).astype(F32) * rate)

        def group(gi, nrm):
            def gen(t, nrm):
                n2 = gi * FILT_GROUP + t
                hs = hm_ref[pl.ds(pl.multiple_of(n2 * H1, SUBLANES), H1), :]
                gp = _dot(hs, w2_ref[...])
                n2f = n2.astype(F32)
                gf = gp[:, :cb] * (e1f * jnp.exp(-n2f * rate))
                gb = gp[:, cb:] * (e1b * jnp.exp(-(N2 - n2f) * rate))
                g_s[t, 0:H1, :] = gf.astype(BF16)
                g_s[t, H1:2 * H1, :] = gb.astype(BF16)
                return nrm + (jnp.abs(gf) + jnp.abs(gb))
            nrm = lax.fori_loop(0, FILT_GROUP, gen, nrm, unroll=DFT_UNROLL)

            def xform(t, carry):
                n2 = gi * FILT_GROUP + t
                a = _dot(m_ref[n2], g_s[t])
                a_s[pl.ds(pl.multiple_of(n2 * pitch, SUBLANES), 2 * H1), :] = _pack2(*_split_lanes(a))
                return carry
            lax.fori_loop(0, FILT_GROUP, xform, 0, unroll=DFT_UNROLL)
            return nrm
        nrm = lax.fori_loop(0, N2 // FILT_GROUP, group, jnp.zeros((H1, cb), F32))
        inv_s[...] = jnp.broadcast_to(1.0 / jnp.sum(nrm, axis=0, keepdims=True), inv_s.shape)

    scale = inv_s[0:1, :]

    def kbody(kk, carry):
        x = (_stage3(a_s, f2_ref, s * ks + kk, H1, pitch) * scale).astype(g_ref.dtype)
        for p, xp in enumerate(_split_lanes(x)):
            g_ref[p, kk, 0] = xp[:N2]
            g_ref[p, kk, 1] = xp[N2:]
        return carry
    lax.fori_loop(0, ks, kbody, 0, unroll=DFT_UNROLL)


def _hy_filters(hm, wout, C, L, consts):
    N2, cb = DFT_N2, HY_CB
    H1 = L // N2
    ks = _pick(H1, 16, 1)
    hd = wout.shape[0]
    nj = C // cb
    _, _, f2_fwd, _, m_filt = consts
    assert nj % 2 == 0
    pw = 2 * cb
    w4 = wout.reshape(hd, 2, 2, nj // 2, pw)
    wf = jnp.transpose(w4[:, :, 0], (1, 2, 0, 3))
    wb = -jnp.transpose(w4[:, :, 1], (1, 2, 0, 3))
    z = jnp.zeros_like(wf)
    w2 = jnp.concatenate([jnp.concatenate([wf, z], axis=-1), jnp.concatenate([z, wb], axis=-1)], axis=2)
    w2 = w2.astype(BF16)
    deltas = jnp.linspace(math.log(HY_FAST_DECAY) / HY_DECAY_TARGET,
                          math.log(HY_SLOW_DECAY) / HY_DECAY_TARGET, C, dtype=F32)
    absd = jnp.abs(deltas).reshape(1, C)
    return pl.pallas_call(
        functools.partial(_filt_body, L=L, H1=H1, ks=ks), grid=(2, nj // 2, H1 // ks),
        in_specs=[pl.BlockSpec((L, LANES), lambda o, j, s: (0, 0)),
                  pl.BlockSpec((None, None, 2 * hd, 2 * pw), lambda o, j, s: (o, j, 0, 0)),
                  pl.BlockSpec((1, pw), lambda o, j, s: (0, j)),
                  pl.BlockSpec((N2, 2 * H1, 2 * H1), lambda o, j, s: (0, 0, 0), pipeline_mode=pl.Buffered(1)),
                  pl.BlockSpec((2 * N2, 2 * N2), lambda o, j, s: (0, 0))],
        out_specs=pl.BlockSpec((None, 2, ks, 2, N2, cb), lambda o, j, s: (o, j, s, 0, 0, 0)),
        out_shape=jax.ShapeDtypeStruct((2, nj, H1, 2, N2, cb), SPEC_DTYPE),
        scratch_shapes=[pltpu.VMEM((N2 * _pitch(2 * H1), cb), jnp.uint32),
                        pltpu.VMEM((FILT_GROUP, 2 * H1, pw), BF16), pltpu.VMEM((SUBLANES, pw), F32)],
        compiler_params=_cparams(("parallel", "parallel", "arbitrary")), name="hy_filters",
    )(hm, w2, absd, m_filt, f2_fwd)


def _hy_gate1_body(gate_ref, v_ref, y_ref, sk_ref, o_ref):
    o_ref[...] = gate_ref[...] * (y_ref[...] + sk_ref[...] * v_ref[...])


def _hy_gate1(uh_blk, gate_blk, v_blk, y_blk, sk):
    nj, L, cb = y_blk.shape
    bm = _pick(L, 256, SUBLANES)
    blk = lambda g: pl.BlockSpec((nj, bm, cb), lambda i: (g, i, 0))
    return pl.pallas_call(
        _hy_gate1_body, grid=(L // bm,),
        in_specs=[blk(gate_blk), blk(v_blk), blk(0), pl.BlockSpec((nj, 1, cb), lambda i: (0, 0, 0))],
        out_specs=blk(0), out_shape=jax.ShapeDtypeStruct((nj, L, cb), F32),
        compiler_params=_cparams(("parallel",)), name="hy_gate1",
    )(uh_blk, uh_blk, y_blk, sk.reshape(nj, 1, cb))


def _hy_gate2_body(gate_ref, v_ref, y_ref, sk_ref, g_ref, o_ref, *, gw):
    nj, _, cb = gate_ref.shape
    for r in range(SUBLANES):
        rows = pl.ds(r, gw, stride=SUBLANES)
        zs = []
        ss = jnp.zeros((gw, cb), F32)
        for k in range(nj):
            z = gate_ref.at[k][rows, :] * (y_ref.at[k][rows, :] + sk_ref[k] * v_ref.at[k][rows, :])
            zs.append(z)
            ss = ss + z * z
        ms = jnp.sum(ss, axis=-1, keepdims=True) * (1.0 / (nj * cb))
        inv = lax.rsqrt(ms + NORM_EPS)
        for k in range(nj):
            o_ref[r, :, k * cb:(k + 1) * cb] = (zs[k] * inv * g_ref[k]).astype(o_ref.dtype)


def _hy_gate2(uh_blk, gate_blk, v_blk, y_blk, sk, g):
    nj, L, cb = y_blk.shape
    N2 = DFT_N2
    nch = L // TT_CHUNK
    gw = _pick(N2, DFT_GROUP, 16)
    ngr = N2 // gw
    blk = lambda gidx: pl.BlockSpec((nj, gw * SUBLANES, cb), lambda c, q: (gidx, c * ngr + q, 0))
    vec = pl.BlockSpec((nj, 1, cb), lambda c, q: (0, 0, 0))
    out = pl.pallas_call(
        functools.partial(_hy_gate2_body, gw=gw), grid=(nch, ngr),
        in_specs=[blk(gate_blk), blk(0), blk(0), vec, vec],
        out_specs=pl.BlockSpec((None, SUBLANES, gw, nj * cb), lambda c, q: (c, 0, q, 0)),
        out_shape=jax.ShapeDtypeStruct((nch, SUBLANES, N2, nj * cb), BF16),
        compiler_params=_cparams(("parallel", "parallel")), name="hy_gate2",
    )(uh_blk, v_blk, y_blk, sk.reshape(nj, 1, cb), g.reshape(nj, 1, cb))
    return out.reshape(L, nj * cb)


def kernel(x, norm_mix_g, w_in, hy_conv_w, hy_conv_b, hy_pos_w1, hy_pos_b1, hy_pos_w2, hy_pos_b2, hy_pos_w3, hy_pos_b3, hy_sin_freq, hy_pos_wout, hy_skip, hy_out_g, ssm_conv_w, ssm_conv_b, ssm_A_log, ssm_dt_bias, ssm_D, ssm_out_g, w_out, norm_mlp_g, w_up, w_down, final_norm_g):
    B, L, D = x.shape
    depth = w_in.shape[0]
    CH = hy_out_g.shape[1]
    WS = ssm_out_g.shape[1]
    heads = ssm_D.shape[1]
    xbc_w = ssm_conv_w.shape[2]
    groups = (xbc_w - WS) // (2 * SSM_STATE)
    o1 = 3 * CH
    o2 = o1 + WS
    o3 = o2 + xbc_w
    assert WS == heads * SSM_HEAD_DIM and heads <= LANES and L % DFT_N2 == 0 and CH % HY_CB == 0

    consts = _dft_consts(L)
    z2 = _pos_features(L, hy_pos_w1.shape[1])
    w_in_b = w_in.astype(BF16)
    w_in_dt = jnp.pad(w_in[:, :, o3:], ((0, 0), (0, 0), (0, LANES - heads))).astype(BF16)
    w_out_b = w_out.astype(BF16)
    w_up_b = w_up.astype(BF16)
    w_down_b = w_down.astype(BF16)
    pad_h = lambda v: jnp.pad(v, (0, LANES - heads)).reshape(1, LANES)
    d_rep = jnp.repeat(ssm_D, SSM_HEAD_DIM, axis=-1)

    outs = []
    for b in range(B):
        xb = x[b]
        delta = None
        for l in range(depth):
            if delta is None:
                h = _rmsnorm(xb, norm_mix_g[l], BF16)
            else:
                xb, h = _add_rmsnorm(xb, delta, norm_mix_g[l], BF16)
            proj = _matmul(h, w_in_b, l, n_cols=o3, out_dtype=PROJ_DTYPE)
            dt_raw = _matmul(h, w_in_dt, l, bm_t=2048)

            nj = CH // HY_CB
            uh = _dwconv(proj, 0, o1, hy_conv_w[l], hy_conv_b[l], "none", blocked_out=True)
            hm = _filter_mlp(z2, hy_pos_w1[l], hy_pos_b1[l], hy_pos_w2[l], hy_pos_b2[l],
                             hy_pos_w3[l], hy_pos_b3[l], hy_sin_freq[l])
            g_all = _hy_filters(hm, hy_pos_wout[l], CH, L, consts)
            y0 = _hy_conv(uh, 2 * nj, nj, g_all, 0, consts)
            z1 = _hy_gate1(uh, 0, 2, y0, hy_skip[l, 0])
            y1 = _hy_conv(z1, 0, nj, g_all, 1, consts)
            y_hy = _hy_gate2(uh, 1, z1, y1, hy_skip[l, 1], hy_out_g[l])

            xbc = _dwconv(proj, o2, xbc_w, ssm_conv_w[l], ssm_conv_b[l], "silu")
            yf = _ssd(xbc, dt_raw, pad_h(ssm_dt_bias[l, 0]), pad_h(ssm_A_log[l, 0]),
                      reverse=False, heads=heads, groups=groups)
            y_ssm = _ssd(xbc, dt_raw, pad_h(ssm_dt_bias[l, 1]), pad_h(ssm_A_log[l, 1]),
                         reverse=True, heads=heads, groups=groups,
                         gate_with=(yf, proj, o1, d_rep[l], ssm_out_g[l]))

            xb = _out_matmul(y_hy, y_ssm, w_out_b, l, xb)
            h2 = _rmsnorm(xb, norm_mlp_g[l], BF16)
            delta = _mlp(h2, w_up_b, w_down_b, l)
        outs.append(_add_rmsnorm(xb, delta, final_norm_g, F32, keep_sum=False))
    return jnp.stack(outs, axis=0)
```
